```python
import math, functools
import jax, jax.numpy as jnp
from jax import lax
import numpy as np

D_MODEL = 2048
BATCH = 2
SEQ = 4096
DEPTH = 4
DEC_BATCH = 8
DEC_SEQ = 4
PAST_LEN = 16384
PAGE_SIZE = 128

POOL_WIDTH = D_MODEL // 4
POOL_WINDOWS = (2, 4, 8, 16)
POOL_GROUP = POOL_WIDTH // len(POOL_WINDOWS)
POOL_STATE = max(POOL_WINDOWS) - 1
HG_WIDTH = 3 * D_MODEL // 8
HG_KDIM = 128
HG_VDIM = 128
HG_HEADS = HG_WIDTH // HG_VDIM
HG_CHUNK = 64
LB_FLOOR = 1e-30
AT_WIDTH = D_MODEL - POOL_WIDTH - HG_WIDTH
AT_VDIM = 128
AT_HEADS = AT_WIDTH // AT_VDIM
AT_QKDIM = AT_VDIM // 2
ROT_DIM = AT_QKDIM // 4
ROPE_THETA = 500000.0
Q_BLOCK = 128
MASK_VALUE = -1e30
D_FF = 11 * D_MODEL // 4
CONV_W = 3
EPS = 1e-6
IN_SIZES = (POOL_WIDTH, HG_HEADS * HG_KDIM, HG_HEADS * HG_KDIM, HG_WIDTH, HG_WIDTH,
            AT_HEADS * 2 * AT_QKDIM, AT_HEADS * 2 * AT_QKDIM, AT_WIDTH)
IN_OFFSETS = tuple(sum(IN_SIZES[:i + 1]) for i in range(len(IN_SIZES) - 1))
IN_WIDTH = sum(IN_SIZES)

kernel_name = "hybrid_pool_hgrn2_diffattn_decoder_step"

F32 = jnp.float32


def rmsnorm(x, g):
    xf = x.astype(F32)
    y = xf * lax.rsqrt(jnp.mean(xf * xf, axis=-1, keepdims=True) + EPS)
    return (y * g.astype(F32)).astype(x.dtype)


def rotary(x, pos):
    half = ROT_DIM // 2
    freqs = jnp.power(ROPE_THETA, -jnp.arange(0, ROT_DIM, 2, dtype=F32) / ROT_DIM)
    ang = pos.astype(F32)[:, None] * freqs[None, :]
    cos = jnp.cos(ang)[None, :, None, None, :]
    sin = jnp.sin(ang)[None, :, None, None, :]
    xr = x[..., :ROT_DIM].astype(F32)
    x1, x2 = xr[..., :half], xr[..., half:]
    rot = jnp.concatenate([x1 * cos - x2 * sin, x2 * cos + x1 * sin], axis=-1)
    return jnp.concatenate([rot.astype(x.dtype), x[..., ROT_DIM:]], axis=-1)


def pool_mixer(xa, prev, pos, w_pool, scale):
    B, T, _ = xa.shape
    xf = jnp.concatenate([prev.astype(xa.dtype), xa], axis=1)
    c = jnp.cumsum(xf.astype(F32), axis=1)
    c = jnp.concatenate([jnp.zeros((B, 1, POOL_WIDTH), F32), c], axis=1)
    P = POOL_STATE
    means = []
    for g, w in enumerate(POOL_WINDOWS):
        sl = slice(g * POOL_GROUP, (g + 1) * POOL_GROUP)
        win = c[:, P + 1:P + 1 + T, sl] - c[:, P + 1 - w:P + 1 - w + T, sl]
        cnt = jnp.minimum(w, pos + 1).astype(F32)
        means.append(win / cnt[None, :, None])
    d = jnp.concatenate(means, axis=-1) - xa.astype(F32)
    d = d.reshape(B, T, len(POOL_WINDOWS), POOL_GROUP)
    y = jnp.einsum('btgc,gcd->btgd', d, w_pool.astype(F32)).reshape(B, T, POOL_WIDTH)
    return y * scale.astype(F32), xf[:, -POOL_STATE:]


def hgrn2_chunked(q, log_f, k, v, S0):
    B, T, H, DK = q.shape
    DV = v.shape[-1]
    C = min(HG_CHUNK, T)
    n = -(-T // C)
    pad = n * C - T
    def prep(a):
        a = jnp.pad(a.astype(F32), ((0, 0), (0, pad), (0, 0), (0, 0)))
        return a.reshape(B, n, C, H, a.shape[-1]).swapaxes(0, 1)
    causal = jnp.tril(jnp.ones((C, C), dtype=bool))[None, :, :, None, None]
    def step(S, inp):
        qc, lfc, kc, vc = inp
        b = jnp.cumsum(lfc, axis=1)
        o_inter = jnp.einsum('bthk,bhkv->bthv', qc * jnp.exp(b), S)
        diff = b[:, :, None] - b[:, None, :]
        decay = jnp.where(causal, jnp.exp(jnp.where(causal, diff, 0.0)), 0.0)
        att = jnp.einsum('bthk,bshk,btshk->bhts', qc, kc, decay)
        o_intra = jnp.einsum('bhts,bshv->bthv', att, vc)
        bC = b[:, -1]
        kd = kc * jnp.exp(bC[:, None] - b)
        S_new = jnp.exp(bC)[..., None] * S + jnp.einsum('bshk,bshv->bhkv', kd, vc)
        return S_new, o_inter + o_intra
    S, o = lax.scan(step, S0.astype(F32), (prep(q), prep(log_f), prep(k), prep(v)))
    o = o.swapaxes(0, 1).reshape(B, n * C, H, DV)[:, :T]
    return o, S.astype(S0.dtype)


def diff_attn_core(q, k, v, mask, lam):
    s = jnp.einsum('bqhcd,bkhcd->bchqk', q, k).astype(F32) * (AT_QKDIM ** -0.5)
    s = jnp.where(mask[None, None, None], s, MASK_VALUE)
    p = jax.nn.softmax(s, axis=-1)
    a = p[:, 0] - lam * p[:, 1]
    return jnp.einsum('bhqk,bkhv->bqhv', a, v.astype(F32))


def prompt_attend(q, k, v, lam):
    B, T = q.shape[:2]
    nb = T // Q_BLOCK
    qb = q.reshape(B, nb, Q_BLOCK, AT_HEADS, 2, AT_QKDIM).swapaxes(0, 1)
    kpos = jnp.arange(T)
    def blk(args):
        qi, i = args
        qpos = i * Q_BLOCK + jnp.arange(Q_BLOCK)
        return diff_attn_core(qi, k, v, kpos[None, :] <= qpos[:, None], lam)
    o = lax.map(blk, (qb, jnp.arange(nb)))
    return o.swapaxes(0, 1).reshape(B, T, AT_HEADS, AT_VDIM)


def sample_attend(q, k, v, lam, k_past, v_past):
    P, T = k_past.shape[1], q.shape[1]
    k_all = jnp.concatenate([k_past.astype(k.dtype), k], axis=1)
    v_all = jnp.concatenate([v_past.astype(v.dtype), v], axis=1)
    kpos = jnp.arange(P + T)
    qpos = P + jnp.arange(T)
    return diff_attn_core(q, k_all, v_all, kpos[None, :] <= qpos[:, None], lam)


def conv_ffn(h, prev, w_gate, w_up, conv_w, conv_b, w_down):
    T = h.shape[1]
    u = h @ w_gate
    uc = jnp.concatenate([prev.astype(u.dtype), u], axis=1)
    c = conv_b
    for j in range(CONV_W):
        c = c + conv_w[j] * uc[:, j:j + T]
    y = (jax.nn.silu(c) * (h @ w_up)) @ w_down
    return y, uc[:, -(CONV_W - 1):]


def block(x, pos, prev_pool, prev_S, prev_conv, attend, l, p):
    dt = x.dtype
    B, T = x.shape[:2]
    h = rmsnorm(x, p['norm_mix_pre'][l])
    proj = h @ p['w_in'][l]
    xa, hq, hf, hi, hg, aq, ak, av = jnp.split(proj, IN_OFFSETS, axis=-1)
    a_out, pool_new = pool_mixer(xa, prev_pool, pos, p['pool_w'][l], p['pool_scale'][l])
    lb = p['hgrn_lb'][l].reshape(HG_HEADS, HG_KDIM)
    z = hf.reshape(B, T, HG_HEADS, HG_KDIM).astype(F32)
    log_f = jnp.logaddexp(jnp.log(jnp.maximum(lb, LB_FLOOR)), jnp.log1p(-lb) + jax.nn.log_sigmoid(z))
    k_in = -jnp.expm1(log_f)
    o_b, S_new = hgrn2_chunked(hq.reshape(B, T, HG_HEADS, HG_KDIM), log_f, k_in,
                               hi.reshape(B, T, HG_HEADS, HG_VDIM), prev_S)
    o_b = rmsnorm(o_b, p['hgrn_norm'][l].reshape(HG_HEADS, HG_VDIM)) * \
        jax.nn.silu(hg.reshape(B, T, HG_HEADS, HG_VDIM).astype(F32))
    q = rotary(aq.reshape(B, T, AT_HEADS, 2, AT_QKDIM), pos)
    k = rotary(ak.reshape(B, T, AT_HEADS, 2, AT_QKDIM), pos)
    v = av.reshape(B, T, AT_HEADS, AT_VDIM)
    lam_init = 0.8 - 0.6 * math.exp(-0.3 * l)
    lam = (jnp.exp(jnp.sum(p['lam_q1'][l].astype(F32) * p['lam_k1'][l].astype(F32)))
           - jnp.exp(jnp.sum(p['lam_q2'][l].astype(F32) * p['lam_k2'][l].astype(F32))) + lam_init)
    o_c = attend(q, k, v, lam)
    o_c = rmsnorm(o_c, p['attn_subln'][l].reshape(AT_HEADS, AT_VDIM)) * (1.0 - lam_init)
    mix = jnp.concatenate([a_out.astype(dt), o_b.reshape(B, T, HG_WIDTH).astype(dt),
                           o_c.reshape(B, T, AT_WIDTH).astype(dt)], axis=-1)
    x = x + rmsnorm(mix @ p['w_out'][l], p['norm_mix_post'][l])
    h = rmsnorm(x, p['norm_ffn_pre'][l])
    f, conv_new = conv_ffn(h, prev_conv, p['w_gate'][l], p['w_up'][l], p['conv_w'][l],
                           p['conv_b'][l], p['w_down'][l])
    x = x + rmsnorm(f, p['norm_ffn_post'][l])
    return x, (k.reshape(B, T, AT_HEADS, 2 * AT_QKDIM), v, pool_new, S_new, conv_new)


def setup_inputs(seed: int = 0) -> dict:
    key = jax.random.key(seed)
    ks = iter(jax.random.split(key, 40))
    n_pages = PAST_LEN // PAGE_SIZE
    n_pool = (5 * DEC_BATCH * n_pages) // 4
    def nrm(shape, scale=1.0):
        return jax.random.normal(next(ks), shape, F32) * scale
    def gain(shape):
        return 1.0 + 0.1 * jax.random.normal(next(ks), shape, F32)
    perm = jax.random.permutation(next(ks), n_pool)
    page_table = perm[:DEC_BATCH * n_pages].reshape(DEC_BATCH, n_pages).astype(jnp.int32)
    return {
        'x_prompt': nrm((BATCH, SEQ, D_MODEL)),
        'x_sample': nrm((DEC_BATCH, DEC_SEQ, D_MODEL)),
        'cache_k': nrm((DEPTH, n_pool, PAGE_SIZE, AT_HEADS, 2 * AT_QKDIM)),
        'cache_v': nrm((DEPTH, n_pool, PAGE_SIZE, AT_HEADS, AT_VDIM)),
        'state_pool': nrm((DEPTH, DEC_BATCH, POOL_STATE, POOL_WIDTH)),
        'state_hgrn': nrm((DEPTH, DEC_BATCH, HG_HEADS, HG_KDIM, HG_VDIM), 0.5),
        'state_conv': nrm((DEPTH, DEC_BATCH, CONV_W - 1, D_FF)),
        'page_table': page_table,
        'w_in': nrm((DEPTH, D_MODEL, IN_WIDTH), D_MODEL ** -0.5),
        'w_out': nrm((DEPTH, D_MODEL, D_MODEL), D_MODEL ** -0.5),
        'norm_mix_pre': gain((DEPTH, D_MODEL)),
        'norm_mix_post': gain((DEPTH, D_MODEL)),
        'norm_ffn_pre': gain((DEPTH, D_MODEL)),
        'norm_ffn_post': gain((DEPTH, D_MODEL)),
        'pool_w': nrm((DEPTH, len(POOL_WINDOWS), POOL_GROUP, POOL_GROUP), POOL_GROUP ** -0.5),
        'pool_scale': gain((DEPTH, POOL_WIDTH)),
        'hgrn_lower_bounds': gain((DEPTH, HG_HEADS * HG_KDIM)),
        'hgrn_norm': gain((DEPTH, HG_WIDTH)),
        'attn_lam_q1': nrm((DEPTH, AT_QKDIM), 0.1),
        'attn_lam_k1': nrm((DEPTH, AT_QKDIM), 0.1),
        'attn_lam_q2': nrm((DEPTH, AT_QKDIM), 0.1),
        'attn_lam_k2': nrm((DEPTH, AT_QKDIM), 0.1),
        'attn_subln': gain((DEPTH, AT_WIDTH)),
        'ffn_w_gate': nrm((DEPTH, D_MODEL, D_FF), D_MODEL ** -0.5),
        'ffn_w_up': nrm((DEPTH, D_MODEL, D_FF), D_MODEL ** -0.5),
        'ffn_conv_w': nrm((DEPTH, CONV_W, D_FF), CONV_W ** -0.5),
        'ffn_conv_b': nrm((DEPTH, D_FF), 0.02),
        'ffn_w_down': nrm((DEPTH, D_FF, D_MODEL), D_FF ** -0.5),
    }


def reference(x_prompt, x_sample, cache_k, cache_v, state_pool, state_hgrn, state_conv, page_table,
              w_in, w_out, norm_mix_pre, norm_mix_post, norm_ffn_pre, norm_ffn_post,
              pool_w, pool_scale, hgrn_lower_bounds, hgrn_norm,
              attn_lam_q1, attn_lam_k1, attn_lam_q2, attn_lam_k2, attn_subln,
              ffn_w_gate, ffn_w_up, ffn_conv_w, ffn_conv_b, ffn_w_down):
    sm = jax.nn.softmax(hgrn_lower_bounds.astype(F32), axis=0)
    lbs = jnp.cumsum(sm, axis=0) - sm[0]
    p = {'w_in': w_in, 'w_out': w_out, 'norm_mix_pre': norm_mix_pre, 'norm_mix_post': norm_mix_post,
         'norm_ffn_pre': norm_ffn_pre, 'norm_ffn_post': norm_ffn_post, 'pool_w': pool_w,
         'pool_scale': pool_scale, 'hgrn_lb': lbs, 'hgrn_norm': hgrn_norm,
         'lam_q1': attn_lam_q1, 'lam_k1': attn_lam_k1, 'lam_q2': attn_lam_q2, 'lam_k2': attn_lam_k2,
         'attn_subln': attn_subln, 'w_gate': ffn_w_gate, 'w_up': ffn_w_up, 'conv_w': ffn_conv_w,
         'conv_b': ffn_conv_b, 'w_down': ffn_w_down}
    B, T = x_prompt.shape[:2]
    DB, TS = x_sample.shape[:2]
    P = page_table.shape[1] * cache_k.shape[2]
    pos_p = jnp.arange(T)
    pos_s = P + jnp.arange(TS)
    zp_pool = jnp.zeros((B, POOL_STATE, POOL_WIDTH), x_prompt.dtype)
    zp_S = jnp.zeros((B, HG_HEADS, HG_KDIM, HG_VDIM), F32)
    zp_conv = jnp.zeros((B, CONV_W - 1, D_FF), x_prompt.dtype)
    xp, xs = x_prompt, x_sample
    st_p, st_s = [], []
    for l in range(DEPTH):
        xp, sp = block(xp, pos_p, zp_pool, zp_S, zp_conv, prompt_attend, l, p)
        kp = cache_k[l][page_table].reshape(DB, P, AT_HEADS, 2, AT_QKDIM)
        vp = cache_v[l][page_table].reshape(DB, P, AT_HEADS, AT_VDIM)
        attend = functools.partial(sample_attend, k_past=kp, v_past=vp)
        xs, ss = block(xs, pos_s, state_pool[l], state_hgrn[l], state_conv[l], attend, l, p)
        st_p.append(sp)
        st_s.append(ss)
    k_prompt = jnp.stack([s[0] for s in st_p])
    v_prompt = jnp.stack([s[1] for s in st_p])
    pool_prompt = jnp.stack([s[2] for s in st_p])
    hgrn_prompt = jnp.stack([s[3] for s in st_p])
    conv_prompt = jnp.stack([s[4] for s in st_p])
    k_sample = jnp.stack([s[0] for s in st_s])
    v_sample = jnp.stack([s[1] for s in st_s])
    pool_sample = jnp.stack([s[2] for s in st_s])
    hgrn_sample = jnp.stack([s[3] for s in st_s])
    conv_sample = jnp.stack([s[4] for s in st_s])
    return (xp, xs, k_prompt, v_prompt, pool_prompt, hgrn_prompt, conv_prompt,
            k_sample, v_sample, pool_sample, hgrn_sample, conv_sample)
```

```python
import functools
import math

import numpy as np
import jax
import jax.numpy as jnp
from jax import lax
from jax.experimental import pallas as pl
from jax.experimental.pallas import tpu as pltpu

F32 = jnp.float32
BF16 = jnp.bfloat16

LANES = 128
SUBLANES = 8
VMEM_LIMIT = 48 * 1024 * 1024

POOL_WINDOWS = (2, 4, 8, 16)
POOL_STATE = max(POOL_WINDOWS) - 1
HEAD = 128
QK_HALF = 64
ROT_DIM = 16
ROPE_THETA = 500000.0
LB_FLOOR = 1e-30
MASK_VALUE = -1e30
CONV_W = 3
EPS = 1e-6
HG_CHUNK = 64
HG_DIAG = 8
PAGES_PER_STEP = 8


def _params(sem):
    return pltpu.CompilerParams(dimension_semantics=sem, vmem_limit_bytes=VMEM_LIMIT)


def _rms(x, g):
    return x * lax.rsqrt(jnp.mean(x * x, axis=-1, keepdims=True) + EPS) * g


def _sigmoid_pair(z):
    e = jnp.exp(-jnp.abs(z))
    r = 1.0 / (1.0 + e)
    er = e * r
    pos = z >= 0
    return jnp.where(pos, r, er), jnp.where(pos, er, r)


def _norm_kernel(x_ref, g_ref, h_ref):
    h_ref[...] = _rms(x_ref[...], g_ref[...]).astype(BF16)


def rmsnorm_bf16(x, g, tm):
    M, D = x.shape
    return pl.pallas_call(
        _norm_kernel,
        grid=(M // tm,),
        in_specs=[pl.BlockSpec((tm, D), lambda i: (i, 0)),
                  pl.BlockSpec((1, D), lambda i: (0, 0))],
        out_specs=pl.BlockSpec((tm, D), lambda i: (i, 0)),
        out_shape=jax.ShapeDtypeStruct((M, D), BF16),
        compiler_params=_params(("parallel",)),
        name="rmsnorm",
    )(x, g)


def _mm_kernel(x_ref, w_ref, o_ref):
    o_ref[...] = jnp.dot(x_ref[...], w_ref[...], preferred_element_type=F32)


def matmul(x, w, tm, tn, name):
    M, K = x.shape
    N = w.shape[1]
    return pl.pallas_call(
        _mm_kernel,
        grid=(M // tm, N // tn),
        in_specs=[pl.BlockSpec((tm, K), lambda i, j: (i, 0)),
                  pl.BlockSpec((K, tn), lambda i, j: (0, j))],
        out_specs=pl.BlockSpec((tm, tn), lambda i, j: (i, j)),
        out_shape=jax.ShapeDtypeStruct((M, N), F32),
        compiler_params=_params(("parallel", "arbitrary")),
        name=name,
    )(x, w)


def _pool_kernel(xa_ref, prev_ref, wp_ref, sc_ref, y_ref, st_ref, buf, *, tt, pos0):
    t = pl.program_id(1)
    P = POOL_STATE + 1

    @pl.when(t == 0)
    def _():
        buf[0:1, :] = jnp.zeros((1, buf.shape[1]), F32)
        buf[1:P, :] = prev_ref[...]

    buf[P:P + tt, :] = xa_ref[...]
    pos = pos0 + t * tt + lax.broadcasted_iota(jnp.int32, (tt, LANES), 0)
    for g, w in enumerate(POOL_WINDOWS):
        cols = slice(g * LANES, (g + 1) * LANES)
        x = buf[P:P + tt, cols]
        acc = x
        for j in range(1, w):
            acc = acc + buf[P - j:P - j + tt, cols]
        cnt = jnp.minimum(w, pos + 1).astype(F32)
        d = acc / cnt - x
        y = jnp.dot(d.astype(BF16), wp_ref[g].astype(BF16), preferred_element_type=F32)
        y_ref[:, cols] = (y * sc_ref[:, cols]).astype(BF16)
    tail = buf[tt:tt + P, :]
    st_ref[...] = tail[1:, :]
    buf[0:P, :] = tail


def pool_mixer(xa, prev, wp, scale, pos0):
    B, T, W = xa.shape
    tt = min(T, 512)
    return pl.pallas_call(
        functools.partial(_pool_kernel, tt=tt, pos0=pos0),
        grid=(B, T // tt),
        in_specs=[pl.BlockSpec((None, tt, W), lambda b, t: (b, t, 0)),
                  pl.BlockSpec((None, POOL_STATE, W), lambda b, t: (b, 0, 0)),
                  pl.BlockSpec((len(POOL_WINDOWS), LANES, LANES), lambda b, t: (0, 0, 0)),
                  pl.BlockSpec((1, W), lambda b, t: (0, 0))],
        out_specs=[pl.BlockSpec((None, tt, W), lambda b, t: (b, t, 0)),
                   pl.BlockSpec((None, POOL_STATE, W), lambda b, t: (b, 0, 0))],
        out_shape=[jax.ShapeDtypeStruct((B, T, W), BF16),
                   jax.ShapeDtypeStruct((B, POOL_STATE, W), F32)],
        scratch_shapes=[pltpu.VMEM((POOL_STATE + 1 + tt, W), F32)],
        compiler_params=_params(("parallel", "arbitrary")),
        name="pool_mixer",
    )(xa, prev, wp, scale)


def _rope_kernel(q_ref, k_ref, fl_ref, s1_ref, s2_ref, qs_ref, kr_ref, kb_ref, *, tt, pos0, heads):
    t = pl.program_id(1)
    pos = (pos0 + t * tt + lax.broadcasted_iota(jnp.int32, (tt, LANES), 0)).astype(F32)
    ang = pos * fl_ref[...]
    c = jnp.cos(ang)
    s = jnp.sin(ang)
    sa = s * s1_ref[...]
    sb = s * s2_ref[...]
    half = ROT_DIM // 2
    for h in range(heads):
        cols = slice(h * HEAD, (h + 1) * HEAD)
        xq = q_ref[:, cols]
        yq = xq * c + pltpu.roll(xq, LANES - half, axis=1) * sa + pltpu.roll(xq, half, axis=1) * sb
        qs_ref[:, cols] = (yq * (QK_HALF ** -0.5)).astype(BF16)
        xk = k_ref[:, cols]
        yk = xk * c + pltpu.roll(xk, LANES - half, axis=1) * sa + pltpu.roll(xk, half, axis=1) * sb
        kr_ref[:, cols] = yk
        kb_ref[:, cols] = yk.astype(BF16)


def rope(main, qblk, kblk, width, tabs, pos0):
    B, T, _ = main.shape
    tt = min(T, 512)
    heads = width // HEAD
    tab = pl.BlockSpec((1, LANES), lambda b, t: (0, 0))
    blk = lambda c: pl.BlockSpec((None, tt, width), lambda b, t: (b, t, c))
    return pl.pallas_call(
        functools.partial(_rope_kernel, tt=tt, pos0=pos0, heads=heads),
        grid=(B, T // tt),
        in_specs=[blk(qblk), blk(kblk), tab, tab, tab],
        out_specs=[blk(0), blk(0), blk(0)],
        out_shape=[jax.ShapeDtypeStruct((B, T, width), BF16),
                   jax.ShapeDtypeStruct((B, T, width), F32),
                   jax.ShapeDtypeStruct((B, T, width), BF16)],
        compiler_params=_params(("parallel", "parallel")),
        name="rope",
    )(main, main, *tabs)


def _hgrn_kernel(q_ref, f_ref, i_ref, g_ref, s0_ref, lb_ref, gn_ref, o_ref, so_ref, st_ref,
                 *, tt, chunk, heads, valid):
    t = pl.program_id(1)
    C = chunk
    levels = [n for n in (64, 32, 16) if n <= C]

    @pl.when(t == 0)
    def _():
        for h in range(heads):
            st_ref[h] = s0_ref[h].T

    row = lax.broadcasted_iota(jnp.int32, (C, LANES), 0)
    rr = lax.broadcasted_iota(jnp.int32, (C, C), 0)
    cc = lax.broadcasted_iota(jnp.int32, (C, C), 1)
    lvl_masks = []
    for n in levels:
        sh = int(math.log2(n))
        same = (rr >> sh) == (cc >> sh)
        up = (rr & (n - 1)) >= n // 2
        lo = (cc & (n - 1)) < n // 2
        lvl_masks.append(jnp.where(same, jnp.where(up, jnp.where(lo, 1.0, 0.0), 0.0), 0.0))

    def chunk_body(c, carry):
        r0 = pl.multiple_of(c * C, C)
        rows = pl.ds(r0, C)
        for h in range(heads):
            cols = slice(h * HEAD, (h + 1) * HEAD)
            q = q_ref[rows, cols]
            z = f_ref[rows, cols]
            v = i_ref[rows, cols]
            gate = g_ref[rows, cols]
            lb = lb_ref[:, cols]
            lbm = jnp.maximum(lb, LB_FLOOR)
            oml = 1.0 - lb
            sg, sgn = _sigmoid_pair(z)
            lf = jnp.log(lbm + oml * sg)
            kin = oml * sgn - (lbm - lb)
            if valid is not None:
                ok = (r0 + row) < valid
                lf = jnp.where(ok, lf, 0.0)
                kin = jnp.where(ok, kin, 0.0)
            b = lf
            s = 1
            while s < C:
                b = b + jnp.where(row >= s, pltpu.roll(b, s, axis=0), 0.0)
                s *= 2
            St = st_ref[h]
            qd = q * jnp.exp(b)
            o = lax.dot_general(qd.astype(BF16), St.astype(BF16), (((1,), (1,)), ((), ())),
                                preferred_element_type=F32)
            if levels:
                att = jnp.zeros((C, C), F32)
                for n, msk in zip(levels, lvl_masks):
                    pieces = []
                    for m in range(C // n):
                        rb = m * n + n // 2 - 1
                        pieces.append(jnp.broadcast_to(b[rb:rb + 1, :], (n, LANES)))
                    R = pieces[0] if len(pieces) == 1 else jnp.concatenate(pieces, axis=0)
                    qn = q * jnp.exp(jnp.minimum(b - R, 0.0))
                    kn = kin * jnp.exp(jnp.minimum(R - b, 0.0))
                    a = lax.dot_general(qn.astype(BF16), kn.astype(BF16), (((1,), (1,)), ((), ())),
                                        preferred_element_type=F32)
                    att = att + a * msk
                o = o + jnp.dot(att.astype(BF16), v.astype(BF16), preferred_element_type=F32)
            rin = row & (HG_DIAG - 1)
            for j in range(HG_DIAG):
                if j == 0:
                    cj = jnp.sum(q * kin, axis=-1, keepdims=True)
                    o = o + cj * v
                else:
                    okj = rin >= j
                    kj = pltpu.roll(kin, j, axis=0)
                    bj = pltpu.roll(b, j, axis=0)
                    vj = pltpu.roll(v, j, axis=0)
                    e = jnp.exp(jnp.where(okj, b - bj, 0.0))
                    cj = jnp.sum(jnp.where(okj, q * kj * e, 0.0), axis=-1, keepdims=True)
                    o = o + cj * vj
            bl = b[C - 1:C, :]
            kd = kin * jnp.exp(bl - b)
            upd = jnp.dot(v.T.astype(BF16), kd.astype(BF16), preferred_element_type=F32)
            st_ref[h] = St * jnp.exp(bl) + upd
            sgg, _ = _sigmoid_pair(gate)
            o_ref[rows, cols] = (_rms(o, gn_ref[:, cols]) * (gate * sgg)).astype(BF16)
        return carry

    lax.fori_loop(0, tt // C, chunk_body, 0)

    @pl.when(t == pl.num_programs(1) - 1)
    def _():
        for h in range(heads):
            so_ref[h] = st_ref[h].T


def hgrn2(main, s0, lb, gn, chunk, valid):
    B, T, _ = main.shape
    heads = s0.shape[1]
    W = heads * HEAD
    tt = min(T, 256)
    blk = lambda c: pl.BlockSpec((None, tt, W), lambda b, t: (b, t, c))
    vec = pl.BlockSpec((1, W), lambda b, t: (0, 0))
    sblk = pl.BlockSpec((None, heads, HEAD, HEAD), lambda b, t: (b, 0, 0, 0))
    return pl.pallas_call(
        functools.partial(_hgrn_kernel, tt=tt, chunk=chunk, heads=heads, valid=valid),
        grid=(B, T // tt),
        in_specs=[blk(0), blk(1), blk(2), blk(3), sblk, vec, vec],
        out_specs=[blk(0), sblk],
        out_shape=[jax.ShapeDtypeStruct((B, T, W), BF16),
                   jax.ShapeDtypeStruct(s0.shape, F32)],
        scratch_shapes=[pltpu.VMEM((heads, HEAD, HEAD), F32)],
        compiler_params=_params(("parallel", "arbitrary")),
        name="hgrn2",
    )(main, main, main, main, s0, lb, gn)


def _lambda(lq1, lk1, lq2, lk2, lam_init):
    a = jnp.exp(jnp.sum(lq1 * lk1, axis=-1, keepdims=True))
    b = jnp.exp(jnp.sum(lq2 * lk2, axis=-1, keepdims=True))
    return a - b + lam_init


def _pattn_kernel(qi_ref, kj_ref, q_ref, k_ref, v_ref, lq1, lk1, lq2, lk2, sub_ref, o_ref,
                  qst, m_ref, l_ref, acc_ref, *, tq, lam_init):
    n = pl.program_id(2)
    qi = qi_ref[n]
    kj = kj_ref[n]

    @pl.when(kj == 0)
    def _():
        q = q_ref[...]
        lane = lax.broadcasted_iota(jnp.int32, q.shape, 1)
        zero = jnp.zeros_like(q)
        qst[0:tq, :] = jnp.where(lane < QK_HALF, q, zero)
        qst[tq:2 * tq, :] = jnp.where(lane >= QK_HALF, q, zero)
        m_ref[...] = jnp.full(m_ref.shape, MASK_VALUE, F32)
        l_ref[...] = jnp.zeros(l_ref.shape, F32)
        acc_ref[...] = jnp.zeros(acc_ref.shape, F32)

    s = lax.dot_general(qst[...], k_ref[...], (((1,), (1,)), ((), ())), preferred_element_type=F32)
    rq = lax.broadcasted_iota(jnp.int32, s.shape, 0)
    rq = jnp.where(rq >= tq, rq - tq, rq) + qi * tq
    ck = lax.broadcasted_iota(jnp.int32, s.shape, 1) + kj * tq
    s = jnp.where(ck <= rq, s, MASK_VALUE)
    m_prev = m_ref[...]
    m_next = jnp.maximum(m_prev, jnp.max(s, axis=1, keepdims=True))
    p = jnp.exp(s - m_next[:, 0:1])
    alpha = jnp.exp(m_prev - m_next)
    l_ref[...] = alpha * l_ref[...] + jnp.sum(p, axis=1, keepdims=True)
    acc_ref[...] = alpha * acc_ref[...] + jnp.dot(p.astype(BF16), v_ref[...].astype(BF16),
                                                  preferred_element_type=F32)
    m_ref[...] = m_next

    @pl.when(kj == qi)
    def _():
        lam = _lambda(lq1[...], lk1[...], lq2[...], lk2[...], lam_init)
        o1 = acc_ref[0:tq, :] / l_ref[0:tq, :]
        o2 = acc_ref[tq:2 * tq, :] / l_ref[tq:2 * tq, :]
        o = o1 - lam * o2
        o_ref[...] = (_rms(o, sub_ref[...]) * (1.0 - lam_init)).astype(BF16)


def prompt_attention(qs, kb, main, vblk0, lams, subln, lam_init):
    B, T, W = qs.shape
    heads = W // HEAD
    tq = min(T, 512)
    nq = T // tq
    pairs = [(i, j) for i in range(nq) for j in range(i + 1)]
    qi = jnp.asarray(np.array([p[0] for p in pairs], np.int32))
    kj = jnp.asarray(np.array([p[1] for p in pairs], np.int32))
    lam_spec = pl.BlockSpec((1, QK_HALF), lambda b, h, n, qi, kj: (0, 0))
    grid_spec = pltpu.PrefetchScalarGridSpec(
        num_scalar_prefetch=2,
        grid=(B, heads, len(pairs)),
        in_specs=[pl.BlockSpec((None, tq, HEAD), lambda b, h, n, qi, kj: (b, qi[n], h)),
                  pl.BlockSpec((None, tq, HEAD), lambda b, h, n, qi, kj: (b, kj[n], h)),
                  pl.BlockSpec((None, tq, HEAD), lambda b, h, n, qi, kj: (b, kj[n], vblk0 + h)),
                  lam_spec, lam_spec, lam_spec, lam_spec,
                  pl.BlockSpec((1, HEAD), lambda b, h, n, qi, kj: (0, h))],
        out_specs=pl.BlockSpec((None, tq, HEAD), lambda b, h, n, qi, kj: (b, qi[n], h)),
        scratch_shapes=[pltpu.VMEM((2 * tq, HEAD), BF16),
                        pltpu.VMEM((2 * tq, HEAD), F32),
                        pltpu.VMEM((2 * tq, HEAD), F32),
                        pltpu.VMEM((2 * tq, HEAD), F32)],
    )
    return pl.pallas_call(
        functools.partial(_pattn_kernel, tq=tq, lam_init=lam_init),
        grid_spec=grid_spec,
        out_shape=jax.ShapeDtypeStruct((B, T, W), BF16),
        compiler_params=_params(("parallel", "parallel", "arbitrary")),
        name="prompt_attention",
    )(qi, kj, qs, kb, main, *lams, subln)


def _sattn_kernel(pt_ref, q_ref, kn_ref, vn_ref, lq1, lk1, lq2, lk2, sub_ref, *rest,
                  heads, tnew, lam_init, pps):
    k_refs = rest[:pps]
    v_refs = rest[pps:2 * pps]
    o_ref = rest[2 * pps]
    qbd, m_ref, l_ref, acc_ref = rest[2 * pps + 1:]
    j = pl.program_id(1)
    R = heads * 2 * SUBLANES
    W = heads * HEAD

    @pl.when(j == 0)
    def _():
        q = q_ref[...]
        qrep = jnp.concatenate([q] * (heads * 2), axis=0)
        lane = lax.broadcasted_iota(jnp.int32, (R, W), 1)
        rowi = lax.broadcasted_iota(jnp.int32, (R, W), 0)
        qbd[...] = jnp.where((lane >> 6) == (rowi >> 3), qrep, 0.0)
        m_ref[...] = jnp.full(m_ref.shape, MASK_VALUE, F32)
        l_ref[...] = jnp.zeros(l_ref.shape, F32)
        acc_ref[...] = jnp.zeros(acc_ref.shape, F32)

    qb = qbd[...].astype(BF16)
    ss = []
    for p in range(pps):
        ss.append(lax.dot_general(qb, k_refs[p][...].astype(BF16), (((1,), (1,)), ((), ())),
                                  preferred_element_type=F32))
    m_prev = m_ref[...]
    m_cur = ss[0].max(axis=1, keepdims=True)
    for p in range(1, pps):
        m_cur = jnp.maximum(m_cur, ss[p].max(axis=1, keepdims=True))
    m_next = jnp.maximum(m_prev, m_cur)
    alpha = jnp.exp(m_prev - m_next)
    l_new = alpha * l_ref[...]
    acc = alpha[:, 0:1] * acc_ref[...]
    for p in range(pps):
        pr = jnp.exp(ss[p] - m_next[:, 0:1])
        l_new = l_new + jnp.sum(pr, axis=1, keepdims=True)
        acc = acc + jnp.dot(pr.astype(BF16), v_refs[p][...].astype(BF16), preferred_element_type=F32)
    m_ref[...] = m_next
    l_ref[...] = l_new
    acc_ref[...] = acc

    @pl.when(j == pl.num_programs(1) - 1)
    def _():
        qf = qbd[...]
        tok = lax.broadcasted_iota(jnp.int32, (R, LANES), 0) & (SUBLANES - 1)
        m_p = m_ref[...]
        sn = []
        for jn in range(tnew):
            sj = jnp.sum(qf * kn_ref[jn:jn + 1, :], axis=1, keepdims=True)
            sn.append(jnp.where(tok >= jn, sj, MASK_VALUE))
        m_n = m_p
        for sj in sn:
            m_n = jnp.maximum(m_n, sj)
        al = jnp.exp(m_p - m_n)
        l_f = al * l_ref[...]
        acc_f = al[:, 0:1] * acc_ref[...]
        for jn in range(tnew):
            pj = jnp.exp(sn[jn] - m_n)
            l_f = l_f + pj
            acc_f = acc_f + pj[:, 0:1] * vn_ref[jn:jn + 1, :]
        lam = _lambda(lq1[...], lk1[...], lq2[...], lk2[...], lam_init)
        for h in range(heads):
            cols = slice(h * HEAD, (h + 1) * HEAD)
            r1 = slice(h * 2 * SUBLANES, h * 2 * SUBLANES + SUBLANES)
            r2 = slice(h * 2 * SUBLANES + SUBLANES, (h + 1) * 2 * SUBLANES)
            o1 = acc_f[r1, cols] / l_f[r1, :]
            o2 = acc_f[r2, cols] / l_f[r2, :]
            o = o1 - lam * o2
            o_ref[:, cols] = (_rms(o, sub_ref[:, cols]) * (1.0 - lam_init)).astype(BF16)


def sample_attention(qs8, kn8, vn8, cache_k, cache_v, layer, page_table, lams, subln, lam_init, tnew):
    DB, _, W = qs8.shape
    heads = W // HEAD
    page = cache_k.shape[2]
    n_pages = page_table.shape[1]
    pps = math.gcd(PAGES_PER_STEP, n_pages)
    R = heads * 2 * SUBLANES
    tok_spec = pl.BlockSpec((None, SUBLANES, W), lambda b, j, pt: (b, 0, 0))
    lam_spec = pl.BlockSpec((1, QK_HALF), lambda b, j, pt: (0, 0))

    def page_spec(p):
        return pl.BlockSpec((None, None, page, W), lambda b, j, pt: (layer, pt[b, j * pps + p], 0, 0))

    grid_spec = pltpu.PrefetchScalarGridSpec(
        num_scalar_prefetch=1,
        grid=(DB, n_pages // pps),
        in_specs=[tok_spec, tok_spec, tok_spec, lam_spec, lam_spec, lam_spec, lam_spec,
                  pl.BlockSpec((1, W), lambda b, j, pt: (0, 0))]
                 + [page_spec(p) for p in range(pps)] + [page_spec(p) for p in range(pps)],
        out_specs=tok_spec,
        scratch_shapes=[pltpu.VMEM((R, W), F32),
                        pltpu.VMEM((R, LANES), F32),
                        pltpu.VMEM((R, LANES), F32),
                        pltpu.VMEM((R, W), F32)],
    )
    return pl.pallas_call(
        functools.partial(_sattn_kernel, heads=heads, tnew=tnew, lam_init=lam_init, pps=pps),
        grid_spec=grid_spec,
        out_shape=jax.ShapeDtypeStruct((DB, SUBLANES, W), BF16),
        compiler_params=_params(("parallel", "arbitrary")),
        name="sample_attention",
    )(page_table, qs8, kn8, vn8, *lams, subln, *([cache_k] * pps), *([cache_v] * pps))


def _oproj_kernel(a_ref, b_ref, c_ref, w_ref, x_ref, gpost_ref, gnext_ref, xo_ref, h_ref, *, wa, wb):
    y = jnp.dot(a_ref[...], w_ref[0:wa, :], preferred_element_type=F32)
    y = y + jnp.dot(b_ref[...], w_ref[wa:wa + wb, :], preferred_element_type=F32)
    y = y + jnp.dot(c_ref[...], w_ref[wa + wb:, :], preferred_element_type=F32)
    x = x_ref[...] + _rms(y, gpost_ref[...])
    xo_ref[...] = x
    h_ref[...] = _rms(x, gnext_ref[...]).astype(BF16)


def out_proj(a, b, c, w, x, gpost, gnext, tm):
    M, D = x.shape
    wa, wb, wc = a.shape[1], b.shape[1], c.shape[1]
    row = lambda width: pl.BlockSpec((tm, width), lambda i: (i, 0))
    vec = pl.BlockSpec((1, D), lambda i: (0, 0))
    return pl.pallas_call(
        functools.partial(_oproj_kernel, wa=wa, wb=wb),
        grid=(M // tm,),
        in_specs=[row(wa), row(wb), row(wc), pl.BlockSpec((D, D), lambda i: (0, 0)), row(D), vec, vec],
        out_specs=[row(D), row(D)],
        out_shape=[jax.ShapeDtypeStruct((M, D), F32), jax.ShapeDtypeStruct((M, D), BF16)],
        compiler_params=_params(("parallel",)),
        name="out_proj",
    )(a, b, c, w, x, gpost, gnext)


def _ffn_up_seq_kernel(h_ref, wg_ref, wu_ref, cw_ref, cb_ref, prev_ref, act_ref, st_ref, buf, *, tm):
    t = pl.program_id(2)
    h = h_ref[...]
    u = jnp.dot(h, wg_ref[...], preferred_element_type=F32)
    up = jnp.dot(h, wu_ref[...], preferred_element_type=F32)

    @pl.when(t == 0)
    def _():
        buf[0:SUBLANES - 2, :] = jnp.zeros((SUBLANES - 2, buf.shape[1]), F32)
        buf[SUBLANES - 2:SUBLANES, :] = prev_ref[...]

    buf[SUBLANES:SUBLANES + tm, :] = u
    c = cb_ref[...] + cw_ref[2:3, :] * u
    c = c + cw_ref[1:2, :] * buf[SUBLANES - 1:SUBLANES - 1 + tm, :]
    c = c + cw_ref[0:1, :] * buf[SUBLANES - 2:SUBLANES - 2 + tm, :]
    sg, _ = _sigmoid_pair(c)
    act_ref[...] = (c * sg * up).astype(BF16)
    tail = buf[tm:tm + SUBLANES, :]
    st_ref[...] = tail[SUBLANES - 2:, :]
    buf[0:SUBLANES, :] = tail


def ffn_up_seq(h3, wg, wu, cw, cb, prev, tn):
    B, T, D = h3.shape
    Fd = wg.shape[1]
    tm = min(T, 512)
    wspec = pl.BlockSpec((D, tn), lambda j, b, t: (0, j))
    return pl.pallas_call(
        functools.partial(_ffn_up_seq_kernel, tm=tm),
        grid=(Fd // tn, B, T // tm),
        in_specs=[pl.BlockSpec((None, tm, D), lambda j, b, t: (b, t, 0)), wspec, wspec,
                  pl.BlockSpec((CONV_W, tn), lambda j, b, t: (0, j)),
                  pl.BlockSpec((1, tn), lambda j, b, t: (0, j)),
                  pl.BlockSpec((None, CONV_W - 1, tn), lambda j, b, t: (b, 0, j))],
        out_specs=[pl.BlockSpec((None, tm, tn), lambda j, b, t: (b, t, j)),
                   pl.BlockSpec((None, CONV_W - 1, tn), lambda j, b, t: (b, 0, j))],
        out_shape=[jax.ShapeDtypeStruct((B, T, Fd), BF16),
                   jax.ShapeDtypeStruct((B, CONV_W - 1, Fd), F32)],
        scratch_shapes=[pltpu.VMEM((SUBLANES + tm, tn), F32)],
        compiler_params=_params(("parallel", "parallel", "arbitrary")),
        name="ffn_up_prompt",
    )(h3, wg, wu, cw, cb, prev)


def _ffn_up_short_kernel(h_ref, wg_ref, wu_ref, cw_ref, cb_ref, p1_ref, p2_ref, act_ref, u_ref, *, T):
    h = h_ref[...]
    u = jnp.dot(h, wg_ref[...], preferred_element_type=F32)
    up = jnp.dot(h, wu_ref[...], preferred_element_type=F32)
    u_ref[...] = u
    r = lax.broadcasted_iota(jnp.int32, u.shape, 0) & (T - 1)
    um1 = jnp.where(r >= 1, pltpu.roll(u, 1, axis=0), p1_ref[...])
    um2 = jnp.where(r >= 2, pltpu.roll(u, 2, axis=0), p2_ref[...])
    c = cb_ref[...] + cw_ref[2:3, :] * u + cw_ref[1:2, :] * um1 + cw_ref[0:1, :] * um2
    sg, _ = _sigmoid_pair(c)
    act_ref[...] = (c * sg * up).astype(BF16)


def ffn_up_short(h, wg, wu, cw, cb, p1, p2, T, tn):
    M, D = h.shape
    Fd = wg.shape[1]
    wspec = pl.BlockSpec((D, tn), lambda j: (0, j))
    col = pl.BlockSpec((M, tn), lambda j: (0, j))
    return pl.pallas_call(
        functools.partial(_ffn_up_short_kernel, T=T),
        grid=(Fd // tn,),
        in_specs=[pl.BlockSpec((M, D), lambda j: (0, 0)), wspec, wspec,
                  pl.BlockSpec((CONV_W, tn), lambda j: (0, j)),
                  pl.BlockSpec((1, tn), lambda j: (0, j)), col, col],
        out_specs=[col, col],
        out_shape=[jax.ShapeDtypeStruct((M, Fd), BF16), jax.ShapeDtypeStruct((M, Fd), F32)],
        compiler_params=_params(("parallel",)),
        name="ffn_up_sample",
    )(h, wg, wu, cw, cb, p1, p2)


def _ffn_down_kernel(a_ref, w_ref, x_ref, gpost_ref, gnext_ref, xo_ref, h_ref, acc_ref):
    k = pl.program_id(1)

    @pl.when(k == 0)
    def _():
        acc_ref[...] = jnp.zeros(acc_ref.shape, F32)

    acc_ref[...] += jnp.dot(a_ref[...], w_ref[...], preferred_element_type=F32)

    @pl.when(k == pl.num_programs(1) - 1)
    def _():
        x = x_ref[...] + _rms(acc_ref[...], gpost_ref[...])
        xo_ref[...] = x
        h_ref[...] = _rms(x, gnext_ref[...]).astype(BF16)


def ffn_down(act, w, x, gpost, gnext, tm, tk):
    M, D = x.shape
    Fd = act.shape[1]
    row = pl.BlockSpec((tm, D), lambda i, k: (i, 0))
    vec = pl.BlockSpec((1, D), lambda i, k: (0, 0))
    return pl.pallas_call(
        _ffn_down_kernel,
        grid=(M // tm, Fd // tk),
        in_specs=[pl.BlockSpec((tm, tk), lambda i, k: (i, k)),
                  pl.BlockSpec((tk, D), lambda i, k: (k, 0)), row, vec, vec],
        out_specs=[row, row],
        out_shape=[jax.ShapeDtypeStruct((M, D), F32), jax.ShapeDtypeStruct((M, D), BF16)],
        scratch_shapes=[pltpu.VMEM((tm, D), F32)],
        compiler_params=_params(("parallel", "arbitrary")),
        name="ffn_down",
    )(act, w, x, gpost, gnext)


def _largest_tile(n, cap, quantum):
    best = quantum
    for t in range(quantum, cap + 1, quantum):
        if n % t == 0:
            best = t
    return best


def _layer(x, h, l, lw, *, seqs, T, pos0, prev_pool, prev_S, prev_conv, paged):
    M, D = x.shape
    B = seqs
    tm = min(M, 1024)
    tm2 = min(M, 256)
    WA = lw['w_xa'].shape[1]
    WH = lw['hgrn_norm'].shape[1]
    WC = lw['attn_subln'].shape[1]
    lam_init = 0.8 - 0.6 * math.exp(-0.3 * l)
    lams = (lw['lam_q1'], lw['lam_k1'], lw['lam_q2'], lw['lam_k2'])

    xa = matmul(h, lw['w_xa'], tm, WA, "proj_pool").reshape(B, T, WA)
    main = matmul(h, lw['w_main'], tm, WH, "proj_main").reshape(B, T, -1)

    a_out, pool_new = pool_mixer(xa, prev_pool, lw['pool_w'], lw['pool_scale'], pos0)

    if paged is None:
        o_b, S_new = hgrn2(main, prev_S, lw['hgrn_lb'], lw['hgrn_norm'], min(HG_CHUNK, T), None)
        qs, k_rot, kb = rope(main, 4, 5, WC, lw['rope_tabs'], pos0)
        o_c = prompt_attention(qs, kb, main, 6 * (WC // HEAD), lams, lw['attn_subln'], lam_init)
        v = main[:, :, 6 * WC:]
    else:
        pad = ((0, 0), (0, SUBLANES - T), (0, 0))
        main8 = jnp.pad(main, pad)
        o_b8, S_new = hgrn2(main8, prev_S, lw['hgrn_lb'], lw['hgrn_norm'], SUBLANES, T)
        o_b = o_b8[:, :T]
        qs8, k_rot8, _ = rope(main8, 4, 5, WC, lw['rope_tabs'], pos0)
        v = main[:, :, 6 * WC:]
        cache_k, cache_v, page_table = paged
        o_c8 = sample_attention(qs8.astype(F32), k_rot8, jnp.pad(v, pad), cache_k, cache_v, l,
                                page_table, lams, lw['attn_subln'], lam_init, T)
        o_c = o_c8[:, :T]
        k_rot = k_rot8[:, :T]

    x, h2 = out_proj(a_out.reshape(M, WA), o_b.reshape(M, WH), o_c.reshape(M, WC), lw['w_out'],
                     x, lw['norm_mix_post'], lw['norm_ffn_pre'], tm2)

    Fd = lw['w_gate'].shape[1]
    tn = _largest_tile(Fd, 512, LANES)
    if paged is None:
        act, conv_new = ffn_up_seq(h2.reshape(B, T, D), lw['w_gate'], lw['w_up'], lw['conv_w'],
                                   lw['conv_b'], prev_conv, tn)
        act = act.reshape(M, Fd)
    else:
        z = jnp.zeros((B, T, Fd), F32)
        p1 = z.at[:, 0].set(prev_conv[:, 1]).reshape(M, Fd)
        p2 = z.at[:, 0].set(prev_conv[:, 0]).at[:, 1].set(prev_conv[:, 1]).reshape(M, Fd)
        act, u = ffn_up_short(h2, lw['w_gate'], lw['w_up'], lw['conv_w'], lw['conv_b'], p1, p2, T, tn)
        conv_new = u.reshape(B, T, Fd)[:, T - (CONV_W - 1):]
    tk = _largest_tile(Fd, 1536, LANES)
    x, h_next = ffn_down(act, lw['w_down'], x, lw['norm_ffn_post'], lw['norm_next'], min(M, 512), tk)
    return x, h_next, (k_rot, v, pool_new, S_new, conv_new)


def kernel(x_prompt, x_sample, cache_k, cache_v, state_pool, state_hgrn, state_conv, page_table, w_in, w_out, norm_mix_pre, norm_mix_post, norm_ffn_pre, norm_ffn_post, pool_w, pool_scale, hgrn_lower_bounds, hgrn_norm, attn_lam_q1, attn_lam_k1, attn_lam_q2, attn_lam_k2, attn_subln, ffn_w_gate, ffn_w_up, ffn_conv_w, ffn_conv_b, ffn_w_down):
    depth = w_in.shape[0]
    B, T, D = x_prompt.shape
    DB, TS, _ = x_sample.shape
    n_pool, page = cache_k.shape[1], cache_k.shape[2]
    P = page_table.shape[1] * page
    WA = pool_scale.shape[1]
    WH = hgrn_norm.shape[1]
    WC = attn_subln.shape[1]
    HH = WH // HEAD
    Fd = ffn_w_gate.shape[2]
    assert TS >= CONV_W - 1 and TS <= SUBLANES and (TS & (TS - 1)) == 0

    sm = jax.nn.softmax(hgrn_lower_bounds.astype(F32), axis=0)
    lbs = jnp.cumsum(sm, axis=0) - sm[0]

    half = ROT_DIM // 2
    freqs = jnp.power(ROPE_THETA, -jnp.arange(0, ROT_DIM, 2, dtype=F32) / ROT_DIM)
    lane = np.arange(LANES)
    in_rot = (lane % QK_HALF) < ROT_DIM
    fl = jnp.where(jnp.asarray(in_rot), jnp.tile(freqs, LANES // half), 0.0).reshape(1, LANES)
    s1 = jnp.asarray(np.where(in_rot & ((lane % QK_HALF) < half), -1.0, 0.0), F32).reshape(1, LANES)
    s2 = jnp.asarray(np.where(in_rot & ((lane % QK_HALF) >= half), 1.0, 0.0), F32).reshape(1, LANES)

    ck = cache_k.reshape(depth, n_pool, page, WC)
    cv = cache_v.reshape(depth, n_pool, page, WC)
    ones = jnp.ones((1, D), F32)

    def layer_weights(l):
        wi = w_in[l].astype(BF16)
        return {
            'w_xa': wi[:, :WA],
            'w_main': wi[:, WA:],
            'w_out': w_out[l].astype(BF16),
            'w_gate': ffn_w_gate[l].astype(BF16),
            'w_up': ffn_w_up[l].astype(BF16),
            'w_down': ffn_w_down[l].astype(BF16),
            'pool_w': pool_w[l],
            'pool_scale': pool_scale[l].reshape(1, WA),
            'hgrn_lb': lbs[l].reshape(1, WH),
            'hgrn_norm': hgrn_norm[l].reshape(1, WH),
            'attn_subln': attn_subln[l].reshape(1, WC),
            'lam_q1': attn_lam_q1[l].reshape(1, QK_HALF),
            'lam_k1': attn_lam_k1[l].reshape(1, QK_HALF),
            'lam_q2': attn_lam_q2[l].reshape(1, QK_HALF),
            'lam_k2': attn_lam_k2[l].reshape(1, QK_HALF),
            'norm_mix_post': norm_mix_post[l].reshape(1, D),
            'norm_ffn_pre': norm_ffn_pre[l].reshape(1, D),
            'norm_ffn_post': norm_ffn_post[l].reshape(1, D),
            'norm_next': norm_mix_pre[l + 1].reshape(1, D) if l + 1 < depth else ones,
            'conv_w': ffn_conv_w[l],
            'conv_b': ffn_conv_b[l].reshape(1, Fd),
            'rope_tabs': (fl, s1, s2),
        }

    xp = x_prompt.reshape(B * T, D)
    xs = x_sample.reshape(DB * TS, D)
    g0 = norm_mix_pre[0].reshape(1, D)
    hp = rmsnorm_bf16(xp, g0, min(B * T, 512))
    hs = rmsnorm_bf16(xs, g0, DB * TS)
    zp_pool = jnp.zeros((B, POOL_STATE, WA), F32)
    zp_S = jnp.zeros((B, HH, HEAD, HEAD), F32)
    zp_conv = jnp.zeros((B, CONV_W - 1, Fd), F32)

    st_p, st_s = [], []
    for l in range(depth):
        lw = layer_weights(l)
        xp, hp, sp = _layer(xp, hp, l, lw, seqs=B, T=T, pos0=0, prev_pool=zp_pool, prev_S=zp_S,
                            prev_conv=zp_conv, paged=None)
        xs, hs, ss = _layer(xs, hs, l, lw, seqs=DB, T=TS, pos0=P, prev_pool=state_pool[l],
                            prev_S=state_hgrn[l], prev_conv=state_conv[l], paged=(ck, cv, page_table))
        st_p.append(sp)
        st_s.append(ss)

    def stack(sts, i, shape):
        return jnp.stack([s[i] for s in sts]).reshape(shape)

    HC = WC // HEAD
    return (xp.reshape(B, T, D), xs.reshape(DB, TS, D),
            stack(st_p, 0, (depth, B, T, HC, HEAD)), stack(st_p, 1, (depth, B, T, HC, HEAD)),
            stack(st_p, 2, (depth, B, POOL_STATE, WA)), stack(st_p, 3, (depth, B, HH, HEAD, HEAD)),
            stack(st_p, 4, (depth, B, CONV_W - 1, Fd)),
            stack(st_s, 0, (depth, DB, TS, HC, HEAD)), stack(st_s, 1, (depth, DB, TS, HC, HEAD)),
            stack(st_s, 2, (depth, DB, POOL_STATE, WA)), stack(st_s, 3, (depth, DB, HH, HEAD, HEAD)),
            stack(st_s, 4, (depth, DB, CONV_W - 1, Fd)))
```

```python
import functools
import math

import numpy as np
import jax
import jax.numpy as jnp
from jax import lax
from jax.experimental import pallas as pl
from jax.experimental.pallas import tpu as pltpu

F32 = jnp.float32
BF16 = jnp.bfloat16

LANES = 128
SUBLANES = 8
VMEM_LIMIT = 48 * 1024 * 1024

POOL_WINDOWS = (2, 4, 8, 16)
POOL_STATE = max(POOL_WINDOWS) - 1
HEAD = 128
QK_HALF = 64
QK_SCALE_LOG2E = QK_HALF ** -0.5 * math.log2(math.e)
ROT_DIM = 16
ROPE_THETA = 500000.0
LB_FLOOR = 1e-30
MASK_VALUE = -1e30
CONV_W = 3
EPS = 1e-6
HG_CHUNK = 64
HG_DIAG = 8
PAGES_PER_STEP = 8


def _params(sem):
    return pltpu.CompilerParams(dimension_semantics=sem, vmem_limit_bytes=VMEM_LIMIT)


def _rms(x, g):
    return x * lax.rsqrt(jnp.mean(x * x, axis=-1, keepdims=True) + EPS) * g


def _sigmoid_pair(z):
    e = jnp.exp(-jnp.abs(z))
    r = 1.0 / (1.0 + e)
    er = e * r
    pos = z >= 0
    return jnp.where(pos, r, er), jnp.where(pos, er, r)


def _norm_kernel(x_ref, g_ref, h_ref):
    h_ref[...] = _rms(x_ref[...], g_ref[...]).astype(BF16)


def rmsnorm_bf16(x, g, tm):
    M, D = x.shape
    return pl.pallas_call(
        _norm_kernel,
        grid=(M // tm,),
        in_specs=[pl.BlockSpec((tm, D), lambda i: (i, 0)),
                  pl.BlockSpec((1, D), lambda i: (0, 0))],
        out_specs=pl.BlockSpec((tm, D), lambda i: (i, 0)),
        out_shape=jax.ShapeDtypeStruct((M, D), BF16),
        compiler_params=_params(("parallel",)),
        name="rmsnorm",
    )(x, g)


def _mm_kernel(x_ref, w_ref, o_ref):
    o_ref[...] = jnp.dot(x_ref[...], w_ref[...], preferred_element_type=F32)


def matmul(x, w, tm, tn, name):
    M, K = x.shape
    N = w.shape[1]
    return pl.pallas_call(
        _mm_kernel,
        grid=(M // tm, N // tn),
        in_specs=[pl.BlockSpec((tm, K), lambda i, j: (i, 0)),
                  pl.BlockSpec((K, tn), lambda i, j: (0, j))],
        out_specs=pl.BlockSpec((tm, tn), lambda i, j: (i, j)),
        out_shape=jax.ShapeDtypeStruct((M, N), F32),
        compiler_params=_params(("parallel", "arbitrary")),
        name=name,
    )(x, w)


def _pool_kernel(xa_ref, prev_ref, wp_ref, sc_ref, y_ref, st_ref, buf, *, tt, pos0):
    t = pl.program_id(1)
    P = POOL_STATE + 1

    @pl.when(t == 0)
    def _():
        buf[0:1, :] = jnp.zeros((1, buf.shape[1]), F32)
        buf[1:P, :] = prev_ref[...]

    buf[P:P + tt, :] = xa_ref[...]
    pos = pos0 + t * tt + lax.broadcasted_iota(jnp.int32, (tt, LANES), 0)
    for g, w in enumerate(POOL_WINDOWS):
        cols = slice(g * LANES, (g + 1) * LANES)
        x = buf[P:P + tt, cols]
        acc = x
        for j in range(1, w):
            acc = acc + buf[P - j:P - j + tt, cols]
        cnt = jnp.minimum(w, pos + 1).astype(F32)
        d = acc / cnt - x
        y = jnp.dot(d.astype(BF16), wp_ref[g].astype(BF16), preferred_element_type=F32)
        y_ref[:, cols] = (y * sc_ref[:, cols]).astype(BF16)
    tail = buf[tt:tt + P, :]
    st_ref[...] = tail[1:, :]
    buf[0:P, :] = tail


def pool_mixer(xa, prev, wp, scale, pos0):
    B, T, W = xa.shape
    tt = min(T, 512)
    return pl.pallas_call(
        functools.partial(_pool_kernel, tt=tt, pos0=pos0),
        grid=(B, T // tt),
        in_specs=[pl.BlockSpec((None, tt, W), lambda b, t: (b, t, 0)),
                  pl.BlockSpec((None, POOL_STATE, W), lambda b, t: (b, 0, 0)),
                  pl.BlockSpec((len(POOL_WINDOWS), LANES, LANES), lambda b, t: (0, 0, 0)),
                  pl.BlockSpec((1, W), lambda b, t: (0, 0))],
        out_specs=[pl.BlockSpec((None, tt, W), lambda b, t: (b, t, 0)),
                   pl.BlockSpec((None, POOL_STATE, W), lambda b, t: (b, 0, 0))],
        out_shape=[jax.ShapeDtypeStruct((B, T, W), BF16),
                   jax.ShapeDtypeStruct((B, POOL_STATE, W), F32)],
        scratch_shapes=[pltpu.VMEM((POOL_STATE + 1 + tt, W), F32)],
        compiler_params=_params(("parallel", "arbitrary")),
        name="pool_mixer",
    )(xa, prev, wp, scale)


def _rope_kernel(q_ref, k_ref, fl_ref, s1_ref, s2_ref, qs_ref, kr_ref, kb_ref, *, tt, pos0, heads):
    t = pl.program_id(1)
    pos = (pos0 + t * tt + lax.broadcasted_iota(jnp.int32, (tt, LANES), 0)).astype(F32)
    ang = pos * fl_ref[...]
    c = jnp.cos(ang)
    s = jnp.sin(ang)
    sa = s * s1_ref[...]
    sb = s * s2_ref[...]
    half = ROT_DIM // 2
    for h in range(heads):
        cols = slice(h * HEAD, (h + 1) * HEAD)
        xq = q_ref[:, cols]
        yq = xq * c + pltpu.roll(xq, LANES - half, axis=1) * sa + pltpu.roll(xq, half, axis=1) * sb
        qs_ref[:, cols] = (yq * (QK_HALF ** -0.5)).astype(BF16)
        xk = k_ref[:, cols]
        yk = xk * c + pltpu.roll(xk, LANES - half, axis=1) * sa + pltpu.roll(xk, half, axis=1) * sb
        kr_ref[:, cols] = yk
        kb_ref[:, cols] = yk.astype(BF16)


def rope(main, qblk, kblk, width, tabs, pos0):
    B, T, _ = main.shape
    tt = min(T, 512)
    heads = width // HEAD
    tab = pl.BlockSpec((1, LANES), lambda b, t: (0, 0))
    blk = lambda c: pl.BlockSpec((None, tt, width), lambda b, t: (b, t, c))
    return pl.pallas_call(
        functools.partial(_rope_kernel, tt=tt, pos0=pos0, heads=heads),
        grid=(B, T // tt),
        in_specs=[blk(qblk), blk(kblk), tab, tab, tab],
        out_specs=[blk(0), blk(0), blk(0)],
        out_shape=[jax.ShapeDtypeStruct((B, T, width), BF16),
                   jax.ShapeDtypeStruct((B, T, width), F32),
                   jax.ShapeDtypeStruct((B, T, width), BF16)],
        compiler_params=_params(("parallel", "parallel")),
        name="rope",
    )(main, main, *tabs)


def _rope_heads_kernel(q_ref, k_ref, v_ref, fl_ref, s1_ref, s2_ref, kin_ref, vin_ref,
                       qs_ref, kb_ref, vb_ref, ko_ref, vo_ref, *, tt, pos0, heads):
    del kin_ref, vin_ref
    t = pl.program_id(1)
    pos = (pos0 + t * tt + lax.broadcasted_iota(jnp.int32, (tt, LANES), 0)).astype(F32)
    ang = pos * fl_ref[...]
    c = jnp.cos(ang)
    s = jnp.sin(ang)
    sa = s * s1_ref[...]
    sb = s * s2_ref[...]
    half = ROT_DIM // 2
    for h in range(heads):
        cols = slice(h * HEAD, (h + 1) * HEAD)
        xq = q_ref[:, cols]
        yq = xq * c + pltpu.roll(xq, LANES - half, axis=1) * sa + pltpu.roll(xq, half, axis=1) * sb
        qs_ref[h] = (yq * QK_SCALE_LOG2E).astype(BF16)
        xk = k_ref[:, cols]
        yk = xk * c + pltpu.roll(xk, LANES - half, axis=1) * sa + pltpu.roll(xk, half, axis=1) * sb
        ko_ref[h] = yk
        kb_ref[h] = yk.astype(BF16)
        xv = v_ref[:, cols]
        vo_ref[h] = xv
        vb_ref[h, :, 0:HEAD] = xv.astype(BF16)
        vb_ref[h, :, HEAD:2 * HEAD] = jnp.ones((tt, HEAD), BF16)


def rope_heads(main, qblk, width, tabs, pos0, k_stack, v_stack, layer):
    B, T, _ = main.shape
    tt = min(T, 512)
    heads = width // HEAD
    tab = pl.BlockSpec((1, LANES), lambda b, t: (0, 0))
    blk = lambda c: pl.BlockSpec((None, tt, width), lambda b, t: (b, t, c))
    hm = pl.BlockSpec((None, heads, tt, HEAD), lambda b, t: (b, 0, t, 0))
    st = pl.BlockSpec((None, None, heads, tt, HEAD), lambda b, t: (layer, b, 0, t, 0))
    anyspec = pl.BlockSpec(memory_space=pl.ANY)
    hshape = jax.ShapeDtypeStruct((B, heads, T, HEAD), BF16)
    vshape = jax.ShapeDtypeStruct((B, heads, T, 2 * HEAD), BF16)
    vm = pl.BlockSpec((None, heads, tt, 2 * HEAD), lambda b, t: (b, 0, t, 0))
    return pl.pallas_call(
        functools.partial(_rope_heads_kernel, tt=tt, pos0=pos0, heads=heads),
        grid=(B, T // tt),
        in_specs=[blk(qblk), blk(qblk + 1), blk(qblk + 2), tab, tab, tab, anyspec, anyspec],
        out_specs=[hm, hm, vm, st, st],
        out_shape=[hshape, hshape, vshape,
                   jax.ShapeDtypeStruct(k_stack.shape, F32), jax.ShapeDtypeStruct(v_stack.shape, F32)],
        input_output_aliases={6: 3, 7: 4},
        compiler_params=_params(("parallel", "parallel")),
        name="rope_heads",
    )(main, main, main, *tabs, k_stack, v_stack)


def _hgrn_kernel(q_ref, f_ref, i_ref, g_ref, s0_ref, lb_ref, gn_ref, o_ref, so_ref, st_ref,
                 *, tt, chunk, heads, valid):
    t = pl.program_id(1)
    C = chunk
    levels = [n for n in (64, 32, 16) if n <= C]

    @pl.when(t == 0)
    def _():
        for h in range(heads):
            st_ref[h] = s0_ref[h].T

    row = lax.broadcasted_iota(jnp.int32, (C, LANES), 0)
    rr = lax.broadcasted_iota(jnp.int32, (C, C), 0)
    cc = lax.broadcasted_iota(jnp.int32, (C, C), 1)
    lvl_masks = []
    for n in levels:
        sh = int(math.log2(n))
        same = (rr >> sh) == (cc >> sh)
        up = (rr & (n - 1)) >= n // 2
        lo = (cc & (n - 1)) < n // 2
        lvl_masks.append(jnp.where(same, jnp.where(up, jnp.where(lo, 1.0, 0.0), 0.0), 0.0))

    def chunk_body(c, carry):
        r0 = pl.multiple_of(c * C, C)
        rows = pl.ds(r0, C)
        for h in range(heads):
            cols = slice(h * HEAD, (h + 1) * HEAD)
            q = q_ref[rows, cols]
            z = f_ref[rows, cols]
            v = i_ref[rows, cols]
            gate = g_ref[rows, cols]
            lb = lb_ref[:, cols]
            lbm = jnp.maximum(lb, LB_FLOOR)
            oml = 1.0 - lb
            sg, sgn = _sigmoid_pair(z)
            lf = jnp.log(lbm + oml * sg)
            kin = oml * sgn - (lbm - lb)
            if valid is not None:
                ok = (r0 + row) < valid
                lf = jnp.where(ok, lf, 0.0)
                kin = jnp.where(ok, kin, 0.0)
            b = lf
            s = 1
            while s < C:
                b = b + jnp.where(row >= s, pltpu.roll(b, s, axis=0), 0.0)
                s *= 2
            St = st_ref[h]
            qd = q * jnp.exp(b)
            o = lax.dot_general(qd.astype(BF16), St.astype(BF16), (((1,), (1,)), ((), ())),
                                preferred_element_type=F32)
            if levels:
                att = jnp.zeros((C, C), F32)
                for n, msk in zip(levels, lvl_masks):
                    pieces = []
                    for m in range(C // n):
                        rb = m * n + n // 2 - 1
                        pieces.append(jnp.broadcast_to(b[rb:rb + 1, :], (n, LANES)))
                    R = pieces[0] if len(pieces) == 1 else jnp.concatenate(pieces, axis=0)
                    qn = q * jnp.exp(jnp.minimum(b - R, 0.0))
                    kn = kin * jnp.exp(jnp.minimum(R - b, 0.0))
                    a = lax.dot_general(qn.astype(BF16), kn.astype(BF16), (((1,), (1,)), ((), ())),
                                        preferred_element_type=F32)
                    att = att + a * msk
                o = o + jnp.dot(att.astype(BF16), v.astype(BF16), preferred_element_type=F32)
            rin = row & (HG_DIAG - 1)
            for j in range(HG_DIAG):
                if j == 0:
                    cj = jnp.sum(q * kin, axis=-1, keepdims=True)
                    o = o + cj * v
                else:
                    okj = rin >= j
                    kj = pltpu.roll(kin, j, axis=0)
                    bj = pltpu.roll(b, j, axis=0)
                    vj = pltpu.roll(v, j, axis=0)
                    e = jnp.exp(jnp.where(okj, b - bj, 0.0))
                    cj = jnp.sum(jnp.where(okj, q * kj * e, 0.0), axis=-1, keepdims=True)
                    o = o + cj * vj
            bl = b[C - 1:C, :]
            kd = kin * jnp.exp(bl - b)
            upd = jnp.dot(v.T.astype(BF16), kd.astype(BF16), preferred_element_type=F32)
            st_ref[h] = St * jnp.exp(bl) + upd
            sgg, _ = _sigmoid_pair(gate)
            o_ref[rows, cols] = (_rms(o, gn_ref[:, cols]) * (gate * sgg)).astype(BF16)
        return carry

    lax.fori_loop(0, tt // C, chunk_body, 0)

    @pl.when(t == pl.num_programs(1) - 1)
    def _():
        for h in range(heads):
            so_ref[h] = st_ref[h].T


def hgrn2(main, s0, lb, gn, chunk, valid):
    B, T, _ = main.shape
    heads = s0.shape[1]
    W = heads * HEAD
    tt = min(T, 256)
    blk = lambda c: pl.BlockSpec((None, tt, W), lambda b, t: (b, t, c))
    vec = pl.BlockSpec((1, W), lambda b, t: (0, 0))
    sblk = pl.BlockSpec((None, heads, HEAD, HEAD), lambda b, t: (b, 0, 0, 0))
    return pl.pallas_call(
        functools.partial(_hgrn_kernel, tt=tt, chunk=chunk, heads=heads, valid=valid),
        grid=(B, T // tt),
        in_specs=[blk(0), blk(1), blk(2), blk(3), sblk, vec, vec],
        out_specs=[blk(0), sblk],
        out_shape=[jax.ShapeDtypeStruct((B, T, W), BF16),
                   jax.ShapeDtypeStruct(s0.shape, F32)],
        scratch_shapes=[pltpu.VMEM((heads, HEAD, HEAD), F32)],
        compiler_params=_params(("parallel", "arbitrary")),
        name="hgrn2",
    )(main, main, main, main, s0, lb, gn)


def _lambda(lq1, lk1, lq2, lk2, lam_init):
    a = jnp.exp(jnp.sum(lq1 * lk1, axis=-1, keepdims=True))
    b = jnp.exp(jnp.sum(lq2 * lk2, axis=-1, keepdims=True))
    return a - b + lam_init


def _pattn_kernel(qi_ref, kj_ref, q_ref, kc_ref, kn_ref, v_ref, lq1, lk1, lq2, lk2, sub_ref, o_ref,
                  qst, m_ref, acc_ref, s_even, s_odd, *, tq, rs, lam_init):
    n = pl.program_id(2)
    qi = qi_ref[n]
    kj = kj_ref[n]

    def scores(k_ref):
        return lax.dot_general(qst[...], k_ref[...], (((1,), (1,)), ((), ())), preferred_element_type=F32)

    @pl.when(kj == 0)
    def _():
        q = q_ref[...]
        lane = lax.broadcasted_iota(jnp.int32, q.shape, 1)
        zero = jnp.zeros_like(q)
        qst[0:tq, :] = jnp.where(lane < QK_HALF, q, zero)
        qst[tq:2 * tq, :] = jnp.where(lane >= QK_HALF, q, zero)
        m_ref[...] = jnp.full(m_ref.shape, MASK_VALUE, F32)
        acc_ref[...] = jnp.zeros(acc_ref.shape, F32)
        s_even[...] = scores(kc_ref)

    def consume(s_ref, diag):
        for c in range(2 * tq // rs):
            rows = slice(c * rs, (c + 1) * rs)
            q0 = (c * rs) % tq
            kw = q0 + rs if diag else tq
            s = s_ref[rows, 0:kw]
            if diag:
                rq = lax.broadcasted_iota(jnp.int32, s.shape, 0) + q0
                ck = lax.broadcasted_iota(jnp.int32, s.shape, 1)
                s = jnp.where(ck <= rq, s, MASK_VALUE)
            m_prev = m_ref[rows, :]
            m_next = jnp.maximum(m_prev, jnp.max(s, axis=1, keepdims=True))
            p = jnp.exp2(s - m_next[:, 0:1])
            alpha = jnp.exp2(m_prev - m_next)
            pv = jnp.dot(p.astype(BF16), v_ref[0:kw, :], preferred_element_type=F32)
            acc_ref[rows, 0:HEAD] = alpha * acc_ref[rows, 0:HEAD] + pv[:, 0:HEAD]
            acc_ref[rows, HEAD:2 * HEAD] = alpha * acc_ref[rows, HEAD:2 * HEAD] + pv[:, HEAD:2 * HEAD]
            m_ref[rows, :] = m_next

    for parity, cur, nxt in ((0, s_even, s_odd), (1, s_odd, s_even)):
        mine = (kj & 1) == parity

        @pl.when(jnp.logical_and(kj < qi, mine))
        def _():
            nxt[...] = scores(kn_ref)
            consume(cur, False)

        @pl.when(jnp.logical_and(kj == qi, mine))
        def _():
            consume(cur, True)
            lam = _lambda(lq1[...], lk1[...], lq2[...], lk2[...], lam_init)
            o1 = acc_ref[0:tq, 0:HEAD] / acc_ref[0:tq, HEAD:2 * HEAD]
            o2 = acc_ref[tq:2 * tq, 0:HEAD] / acc_ref[tq:2 * tq, HEAD:2 * HEAD]
            o = o1 - lam * o2
            o_ref[...] = (_rms(o, sub_ref[...]) * (1.0 - lam_init)).astype(BF16)


def prompt_attention(qs, kb, vb, lams, subln, lam_init):
    B, heads, T, _ = qs.shape
    tq = min(T, 1024)
    rs = min(tq, 256)
    nq = T // tq
    pairs = [(i, j) for i in range(nq) for j in range(i + 1)]
    qi = jnp.asarray(np.array([p[0] for p in pairs], np.int32))
    kj = jnp.asarray(np.array([p[1] for p in pairs], np.int32))
    lam_spec = pl.BlockSpec((1, QK_HALF), lambda b, h, n, qi, kj: (0, 0))
    npairs = len(pairs)
    kc_spec = pl.BlockSpec((None, None, tq, HEAD), lambda b, h, n, qi, kj: (b, h, kj[n], 0))
    kn_spec = pl.BlockSpec((None, None, tq, HEAD),
                           lambda b, h, n, qi, kj: (b, h, kj[jnp.minimum(n + 1, npairs - 1)], 0))
    grid_spec = pltpu.PrefetchScalarGridSpec(
        num_scalar_prefetch=2,
        grid=(B, heads, len(pairs)),
        in_specs=[pl.BlockSpec((None, None, tq, HEAD), lambda b, h, n, qi, kj: (b, h, qi[n], 0)),
                  kc_spec, kn_spec,
                  pl.BlockSpec((None, None, tq, 2 * HEAD), lambda b, h, n, qi, kj: (b, h, kj[n], 0)),
                  lam_spec, lam_spec, lam_spec, lam_spec,
                  pl.BlockSpec((1, HEAD), lambda b, h, n, qi, kj: (0, h))],
        out_specs=pl.BlockSpec((None, tq, HEAD), lambda b, h, n, qi, kj: (b, qi[n], h)),
        scratch_shapes=[pltpu.VMEM((2 * tq, HEAD), BF16),
                        pltpu.VMEM((2 * tq, HEAD), F32),
                        pltpu.VMEM((2 * tq, 2 * HEAD), F32),
                        pltpu.VMEM((2 * tq, tq), F32),
                        pltpu.VMEM((2 * tq, tq), F32)],
    )
    return pl.pallas_call(
        functools.partial(_pattn_kernel, tq=tq, rs=rs, lam_init=lam_init),
        grid_spec=grid_spec,
        out_shape=jax.ShapeDtypeStruct((B, T, heads * HEAD), BF16),
        compiler_params=_params(("parallel", "parallel", "arbitrary")),
        name="prompt_attention",
    )(qi, kj, qs, kb, kb, vb, *lams, subln)


def _sattn_kernel(pt_ref, q_ref, kn_ref, vn_ref, lq1, lk1, lq2, lk2, sub_ref, *rest,
                  heads, tnew, lam_init, pps):
    k_refs = rest[:pps]
    v_refs = rest[pps:2 * pps]
    o_ref = rest[2 * pps]
    qbd, m_ref, l_ref, acc_ref = rest[2 * pps + 1:]
    j = pl.program_id(1)
    R = heads * 2 * SUBLANES
    W = heads * HEAD

    @pl.when(j == 0)
    def _():
        q = q_ref[...]
        qrep = jnp.concatenate([q] * (heads * 2), axis=0)
        lane = lax.broadcasted_iota(jnp.int32, (R, W), 1)
        rowi = lax.broadcasted_iota(jnp.int32, (R, W), 0)
        qbd[...] = jnp.where((lane >> 6) == (rowi >> 3), qrep, 0.0)
        m_ref[...] = jnp.full(m_ref.shape, MASK_VALUE, F32)
        l_ref[...] = jnp.zeros(l_ref.shape, F32)
        acc_ref[...] = jnp.zeros(acc_ref.shape, F32)

    def page_rows(ref):
        return jnp.concatenate([ref[h] for h in range(heads)], axis=1).astype(BF16)

    qb = qbd[...].astype(BF16)
    ss = []
    for p in range(pps):
        ss.append(lax.dot_general(qb, page_rows(k_refs[p]), (((1,), (1,)), ((), ())),
                                  preferred_element_type=F32))
    m_prev = m_ref[...]
    m_cur = ss[0].max(axis=1, keepdims=True)
    for p in range(1, pps):
        m_cur = jnp.maximum(m_cur, ss[p].max(axis=1, keepdims=True))
    m_next = jnp.maximum(m_prev, m_cur)
    alpha = jnp.exp(m_prev - m_next)
    l_new = alpha * l_ref[...]
    acc = alpha[:, 0:1] * acc_ref[...]
    for p in range(pps):
        pr = jnp.exp(ss[p] - m_next[:, 0:1])
        l_new = l_new + jnp.sum(pr, axis=1, keepdims=True)
        acc = acc + jnp.dot(pr.astype(BF16), page_rows(v_refs[p]), preferred_element_type=F32)
    m_ref[...] = m_next
    l_ref[...] = l_new
    acc_ref[...] = acc

    @pl.when(j == pl.num_programs(1) - 1)
    def _():
        qf = qbd[...]
        tok = lax.broadcasted_iota(jnp.int32, (R, LANES), 0) & (SUBLANES - 1)
        m_p = m_ref[...]
        sn = []
        for jn in range(tnew):
            sj = jnp.sum(qf * kn_ref[jn:jn + 1, :], axis=1, keepdims=True)
            sn.append(jnp.where(tok >= jn, sj, MASK_VALUE))
        m_n = m_p
        for sj in sn:
            m_n = jnp.maximum(m_n, sj)
        al = jnp.exp(m_p - m_n)
        l_f = al * l_ref[...]
        acc_f = al[:, 0:1] * acc_ref[...]
        for jn in range(tnew):
            pj = jnp.exp(sn[jn] - m_n)
            l_f = l_f + pj
            acc_f = acc_f + pj[:, 0:1] * vn_ref[jn:jn + 1, :]
        lam = _lambda(lq1[...], lk1[...], lq2[...], lk2[...], lam_init)
        for h in range(heads):
            cols = slice(h * HEAD, (h + 1) * HEAD)
            r1 = slice(h * 2 * SUBLANES, h * 2 * SUBLANES + SUBLANES)
            r2 = slice(h * 2 * SUBLANES + SUBLANES, (h + 1) * 2 * SUBLANES)
            o1 = acc_f[r1, cols] / l_f[r1, :]
            o2 = acc_f[r2, cols] / l_f[r2, :]
            o = o1 - lam * o2
            o_ref[:, cols] = (_rms(o, sub_ref[:, cols]) * (1.0 - lam_init)).astype(BF16)


def sample_attention(qs8, kn8, vn8, cache_k, cache_v, layer, page_table, lams, subln, lam_init, tnew):
    DB, _, W = qs8.shape
    heads = W // HEAD
    page = cache_k.shape[3]
    n_pages = page_table.shape[1]
    pps = math.gcd(PAGES_PER_STEP, n_pages)
    R = heads * 2 * SUBLANES
    tok_spec = pl.BlockSpec((None, SUBLANES, W), lambda b, j, pt: (b, 0, 0))
    lam_spec = pl.BlockSpec((1, QK_HALF), lambda b, j, pt: (0, 0))

    def page_spec(p):
        return pl.BlockSpec((None, None, heads, page, HEAD),
                            lambda b, j, pt: (layer, pt[b, j * pps + p], 0, 0, 0))

    grid_spec = pltpu.PrefetchScalarGridSpec(
        num_scalar_prefetch=1,
        grid=(DB, n_pages // pps),
        in_specs=[tok_spec, tok_spec, tok_spec, lam_spec, lam_spec, lam_spec, lam_spec,
                  pl.BlockSpec((1, W), lambda b, j, pt: (0, 0))]
                 + [page_spec(p) for p in range(pps)] + [page_spec(p) for p in range(pps)],
        out_specs=tok_spec,
        scratch_shapes=[pltpu.VMEM((R, W), F32),
                        pltpu.VMEM((R, LANES), F32),
                        pltpu.VMEM((R, LANES), F32),
                        pltpu.VMEM((R, W), F32)],
    )
    return pl.pallas_call(
        functools.partial(_sattn_kernel, heads=heads, tnew=tnew, lam_init=lam_init, pps=pps),
        grid_spec=grid_spec,
        out_shape=jax.ShapeDtypeStruct((DB, SUBLANES, W), BF16),
        compiler_params=_params(("parallel", "arbitrary")),
        name="sample_attention",
    )(page_table, qs8, kn8, vn8, *lams, subln, *([cache_k] * pps), *([cache_v] * pps))


def _oproj_kernel(a_ref, b_ref, c_ref, w_ref, x_ref, gpost_ref, gnext_ref, xo_ref, h_ref, *, wa, wb):
    y = jnp.dot(a_ref[...], w_ref[0:wa, :], preferred_element_type=F32)
    y = y + jnp.dot(b_ref[...], w_ref[wa:wa + wb, :], preferred_element_type=F32)
    y = y + jnp.dot(c_ref[...], w_ref[wa + wb:, :], preferred_element_type=F32)
    x = x_ref[...] + _rms(y, gpost_ref[...])
    xo_ref[...] = x
    h_ref[...] = _rms(x, gnext_ref[...]).astype(BF16)


def out_proj(a, b, c, w, x, gpost, gnext, tm):
    M, D = x.shape
    wa, wb, wc = a.shape[1], b.shape[1], c.shape[1]
    row = lambda width: pl.BlockSpec((tm, width), lambda i: (i, 0))
    vec = pl.BlockSpec((1, D), lambda i: (0, 0))
    return pl.pallas_call(
        functools.partial(_oproj_kernel, wa=wa, wb=wb),
        grid=(M // tm,),
        in_specs=[row(wa), row(wb), row(wc), pl.BlockSpec((D, D), lambda i: (0, 0)), row(D), vec, vec],
        out_specs=[row(D), row(D)],
        out_shape=[jax.ShapeDtypeStruct((M, D), F32), jax.ShapeDtypeStruct((M, D), BF16)],
        compiler_params=_params(("parallel",)),
        name="out_proj",
    )(a, b, c, w, x, gpost, gnext)


def _ffn_up_seq_kernel(h_ref, wg_ref, wu_ref, cw_ref, cb_ref, prev_ref, act_ref, st_ref, buf, *, tm):
    t = pl.program_id(2)
    h = h_ref[...]
    u = jnp.dot(h, wg_ref[...], preferred_element_type=F32)
    up = jnp.dot(h, wu_ref[...], preferred_element_type=F32)

    @pl.when(t == 0)
    def _():
        buf[0:SUBLANES - 2, :] = jnp.zeros((SUBLANES - 2, buf.shape[1]), F32)
        buf[SUBLANES - 2:SUBLANES, :] = prev_ref[...]

    buf[SUBLANES:SUBLANES + tm, :] = u
    c = cb_ref[...] + cw_ref[2:3, :] * u
    c = c + cw_ref[1:2, :] * buf[SUBLANES - 1:SUBLANES - 1 + tm, :]
    c = c + cw_ref[0:1, :] * buf[SUBLANES - 2:SUBLANES - 2 + tm, :]
    sg, _ = _sigmoid_pair(c)
    act_ref[...] = (c * sg * up).astype(BF16)
    tail = buf[tm:tm + SUBLANES, :]
    st_ref[...] = tail[SUBLANES - 2:, :]
    buf[0:SUBLANES, :] = tail


def ffn_up_seq(h3, wg, wu, cw, cb, prev, tn):
    B, T, D = h3.shape
    Fd = wg.shape[1]
    tm = min(T, 512)
    wspec = pl.BlockSpec((D, tn), lambda j, b, t: (0, j))
    return pl.pallas_call(
        functools.partial(_ffn_up_seq_kernel, tm=tm),
        grid=(Fd // tn, B, T // tm),
        in_specs=[pl.BlockSpec((None, tm, D), lambda j, b, t: (b, t, 0)), wspec, wspec,
                  pl.BlockSpec((CONV_W, tn), lambda j, b, t: (0, j)),
                  pl.BlockSpec((1, tn), lambda j, b, t: (0, j)),
                  pl.BlockSpec((None, CONV_W - 1, tn), lambda j, b, t: (b, 0, j))],
        out_specs=[pl.BlockSpec((None, tm, tn), lambda j, b, t: (b, t, j)),
                   pl.BlockSpec((None, CONV_W - 1, tn), lambda j, b, t: (b, 0, j))],
        out_shape=[jax.ShapeDtypeStruct((B, T, Fd), BF16),
                   jax.ShapeDtypeStruct((B, CONV_W - 1, Fd), F32)],
        scratch_shapes=[pltpu.VMEM((SUBLANES + tm, tn), F32)],
        compiler_params=_params(("parallel", "parallel", "arbitrary")),
        name="ffn_up_prompt",
    )(h3, wg, wu, cw, cb, prev)


def _ffn_up_short_kernel(h_ref, wg_ref, wu_ref, cw_ref, cb_ref, p1_ref, p2_ref, act_ref, u_ref, *, T):
    h = h_ref[...]
    u = jnp.dot(h, wg_ref[...], preferred_element_type=F32)
    up = jnp.dot(h, wu_ref[...], preferred_element_type=F32)
    u_ref[...] = u
    r = lax.broadcasted_iota(jnp.int32, u.shape, 0) & (T - 1)
    um1 = jnp.where(r >= 1, pltpu.roll(u, 1, axis=0), p1_ref[...])
    um2 = jnp.where(r >= 2, pltpu.roll(u, 2, axis=0), p2_ref[...])
    c = cb_ref[...] + cw_ref[2:3, :] * u + cw_ref[1:2, :] * um1 + cw_ref[0:1, :] * um2
    sg, _ = _sigmoid_pair(c)
    act_ref[...] = (c * sg * up).astype(BF16)


def ffn_up_short(h, wg, wu, cw, cb, p1, p2, T, tn):
    M, D = h.shape
    Fd = wg.shape[1]
    wspec = pl.BlockSpec((D, tn), lambda j: (0, j))
    col = pl.BlockSpec((M, tn), lambda j: (0, j))
    return pl.pallas_call(
        functools.partial(_ffn_up_short_kernel, T=T),
        grid=(Fd // tn,),
        in_specs=[pl.BlockSpec((M, D), lambda j: (0, 0)), wspec, wspec,
                  pl.BlockSpec((CONV_W, tn), lambda j: (0, j)),
                  pl.BlockSpec((1, tn), lambda j: (0, j)), col, col],
        out_specs=[col, col],
        out_shape=[jax.ShapeDtypeStruct((M, Fd), BF16), jax.ShapeDtypeStruct((M, Fd), F32)],
        compiler_params=_params(("parallel",)),
        name="ffn_up_sample",
    )(h, wg, wu, cw, cb, p1, p2)


def _ffn_down_kernel(a_ref, w_ref, x_ref, gpost_ref, gnext_ref, xo_ref, h_ref, acc_ref):
    k = pl.program_id(1)

    @pl.when(k == 0)
    def _():
        acc_ref[...] = jnp.zeros(acc_ref.shape, F32)

    acc_ref[...] += jnp.dot(a_ref[...], w_ref[...], preferred_element_type=F32)

    @pl.when(k == pl.num_programs(1) - 1)
    def _():
        x = x_ref[...] + _rms(acc_ref[...], gpost_ref[...])
        xo_ref[...] = x
        h_ref[...] = _rms(x, gnext_ref[...]).astype(BF16)


def ffn_down(act, w, x, gpost, gnext, tm, tk):
    M, D = x.shape
    Fd = act.shape[1]
    row = pl.BlockSpec((tm, D), lambda i, k: (i, 0))
    vec = pl.BlockSpec((1, D), lambda i, k: (0, 0))
    return pl.pallas_call(
        _ffn_down_kernel,
        grid=(M // tm, Fd // tk),
        in_specs=[pl.BlockSpec((tm, tk), lambda i, k: (i, k)),
                  pl.BlockSpec((tk, D), lambda i, k: (k, 0)), row, vec, vec],
        out_specs=[row, row],
        out_shape=[jax.ShapeDtypeStruct((M, D), F32), jax.ShapeDtypeStruct((M, D), BF16)],
        scratch_shapes=[pltpu.VMEM((tm, D), F32)],
        compiler_params=_params(("parallel", "arbitrary")),
        name="ffn_down",
    )(act, w, x, gpost, gnext)


def _largest_tile(n, cap, quantum):
    best = quantum
    for t in range(quantum, cap + 1, quantum):
        if n % t == 0:
            best = t
    return best


def _layer(x, h, l, lw, *, seqs, T, pos0, prev_pool, prev_S, prev_conv, paged, kv_stacks=None):
    M, D = x.shape
    B = seqs
    tm = min(M, 1024)
    tm2 = min(M, 256)
    WA = lw['w_xa'].shape[1]
    WH = lw['hgrn_norm'].shape[1]
    WC = lw['attn_subln'].shape[1]
    lam_init = 0.8 - 0.6 * math.exp(-0.3 * l)
    lams = (lw['lam_q1'], lw['lam_k1'], lw['lam_q2'], lw['lam_k2'])

    xa = matmul(h, lw['w_xa'], tm, WA, "proj_pool").reshape(B, T, WA)
    main = matmul(h, lw['w_main'], tm, WH, "proj_main").reshape(B, T, -1)

    a_out, pool_new = pool_mixer(xa, prev_pool, lw['pool_w'], lw['pool_scale'], pos0)

    if paged is None:
        o_b, S_new = hgrn2(main, prev_S, lw['hgrn_lb'], lw['hgrn_norm'], min(HG_CHUNK, T), None)
        qs, kb, vb, k_rot, v = rope_heads(main, 4, WC, lw['rope_tabs'], pos0, *kv_stacks, l)
        o_c = prompt_attention(qs, kb, vb, lams, lw['attn_subln'], lam_init)
    else:
        pad = ((0, 0), (0, SUBLANES - T), (0, 0))
        main8 = jnp.pad(main, pad)
        o_b8, S_new = hgrn2(main8, prev_S, lw['hgrn_lb'], lw['hgrn_norm'], SUBLANES, T)
        o_b = o_b8[:, :T]
        qs8, k_rot8, _ = rope(main8, 4, 5, WC, lw['rope_tabs'], pos0)
        v = main[:, :, 6 * WC:]
        cache_k, cache_v, page_table = paged
        o_c8 = sample_attention(qs8.astype(F32), k_rot8, jnp.pad(v, pad), cache_k, cache_v, l,
                                page_table, lams, lw['attn_subln'], lam_init, T)
        o_c = o_c8[:, :T]
        k_rot = k_rot8[:, :T]

    x, h2 = out_proj(a_out.reshape(M, WA), o_b.reshape(M, WH), o_c.reshape(M, WC), lw['w_out'],
                     x, lw['norm_mix_post'], lw['norm_ffn_pre'], tm2)

    Fd = lw['w_gate'].shape[1]
    tn = _largest_tile(Fd, 512, LANES)
    if paged is None:
        act, conv_new = ffn_up_seq(h2.reshape(B, T, D), lw['w_gate'], lw['w_up'], lw['conv_w'],
                                   lw['conv_b'], prev_conv, tn)
        act = act.reshape(M, Fd)
    else:
        z = jnp.zeros((B, T, Fd), F32)
        p1 = z.at[:, 0].set(prev_conv[:, 1]).reshape(M, Fd)
        p2 = z.at[:, 0].set(prev_conv[:, 0]).at[:, 1].set(prev_conv[:, 1]).reshape(M, Fd)
        act, u = ffn_up_short(h2, lw['w_gate'], lw['w_up'], lw['conv_w'], lw['conv_b'], p1, p2, T, tn)
        conv_new = u.reshape(B, T, Fd)[:, T - (CONV_W - 1):]
    tk = _largest_tile(Fd, 1536, LANES)
    x, h_next = ffn_down(act, lw['w_down'], x, lw['norm_ffn_post'], lw['norm_next'], min(M, 512), tk)
    return x, h_next, (k_rot, v, pool_new, S_new, conv_new)


def kernel(x_prompt, x_sample, cache_k, cache_v, state_pool, state_hgrn, state_conv, page_table, w_in, w_out, norm_mix_pre, norm_mix_post, norm_ffn_pre, norm_ffn_post, pool_w, pool_scale, hgrn_lower_bounds, hgrn_norm, attn_lam_q1, attn_lam_k1, attn_lam_q2, attn_lam_k2, attn_subln, ffn_w_gate, ffn_w_up, ffn_conv_w, ffn_conv_b, ffn_w_down):
    depth = w_in.shape[0]
    B, T, D = x_prompt.shape
    DB, TS, _ = x_sample.shape
    page = cache_k.shape[2]
    P = page_table.shape[1] * page
    WA = pool_scale.shape[1]
    WH = hgrn_norm.shape[1]
    WC = attn_subln.shape[1]
    HH = WH // HEAD
    Fd = ffn_w_gate.shape[2]
    assert TS >= CONV_W - 1 and TS <= SUBLANES and (TS & (TS - 1)) == 0

    sm = jax.nn.softmax(hgrn_lower_bounds.astype(F32), axis=0)
    lbs = jnp.cumsum(sm, axis=0) - sm[0]

    half = ROT_DIM // 2
    freqs = jnp.power(ROPE_THETA, -jnp.arange(0, ROT_DIM, 2, dtype=F32) / ROT_DIM)
    lane = np.arange(LANES)
    in_rot = (lane % QK_HALF) < ROT_DIM
    fl = jnp.where(jnp.asarray(in_rot), jnp.tile(freqs, LANES // half), 0.0).reshape(1, LANES)
    s1 = jnp.asarray(np.where(in_rot & ((lane % QK_HALF) < half), -1.0, 0.0), F32).reshape(1, LANES)
    s2 = jnp.asarray(np.where(in_rot & ((lane % QK_HALF) >= half), 1.0, 0.0), F32).reshape(1, LANES)

    ones = jnp.ones((1, D), F32)
    HC = WC // HEAD
    ck = jnp.swapaxes(cache_k, 2, 3)
    cv = jnp.swapaxes(cache_v, 2, 3)
    kv = (jnp.zeros((depth, B, HC, T, HEAD), F32), jnp.zeros((depth, B, HC, T, HEAD), F32))

    def layer_weights(l):
        wi = w_in[l].astype(BF16)
        return {
            'w_xa': wi[:, :WA],
            'w_main': wi[:, WA:],
            'w_out': w_out[l].astype(BF16),
            'w_gate': ffn_w_gate[l].astype(BF16),
            'w_up': ffn_w_up[l].astype(BF16),
            'w_down': ffn_w_down[l].astype(BF16),
            'pool_w': pool_w[l],
            'pool_scale': pool_scale[l].reshape(1, WA),
            'hgrn_lb': lbs[l].reshape(1, WH),
            'hgrn_norm': hgrn_norm[l].reshape(1, WH),
            'attn_subln': attn_subln[l].reshape(1, WC),
            'lam_q1': attn_lam_q1[l].reshape(1, QK_HALF),
            'lam_k1': attn_lam_k1[l].reshape(1, QK_HALF),
            'lam_q2': attn_lam_q2[l].reshape(1, QK_HALF),
            'lam_k2': attn_lam_k2[l].reshape(1, QK_HALF),
            'norm_mix_post': norm_mix_post[l].reshape(1, D),
            'norm_ffn_pre': norm_ffn_pre[l].reshape(1, D),
            'norm_ffn_post': norm_ffn_post[l].reshape(1, D),
            'norm_next': norm_mix_pre[l + 1].reshape(1, D) if l + 1 < depth else ones,
            'conv_w': ffn_conv_w[l],
            'conv_b': ffn_conv_b[l].reshape(1, Fd),
            'rope_tabs': (fl, s1, s2),
        }

    xp = x_prompt.reshape(B * T, D)
    xs = x_sample.reshape(DB * TS, D)
    g0 = norm_mix_pre[0].reshape(1, D)
    hp = rmsnorm_bf16(xp, g0, min(B * T, 512))
    hs = rmsnorm_bf16(xs, g0, DB * TS)
    zp_pool = jnp.zeros((B, POOL_STATE, WA), F32)
    zp_S = jnp.zeros((B, HH, HEAD, HEAD), F32)
    zp_conv = jnp.zeros((B, CONV_W - 1, Fd), F32)

    st_p, st_s = [], []
    for l in range(depth):
        lw = layer_weights(l)
        xp, hp, sp = _layer(xp, hp, l, lw, seqs=B, T=T, pos0=0, prev_pool=zp_pool, prev_S=zp_S,
                            prev_conv=zp_conv, paged=None, kv_stacks=kv)
        kv = sp[:2]
        xs, hs, ss = _layer(xs, hs, l, lw, seqs=DB, T=TS, pos0=P, prev_pool=state_pool[l],
                            prev_S=state_hgrn[l], prev_conv=state_conv[l],
                            paged=(ck, cv, page_table))
        st_p.append(sp)
        st_s.append(ss)

    def stack(sts, i, shape):
        return jnp.stack([s[i] for s in sts]).reshape(shape)

    return (xp.reshape(B, T, D), xs.reshape(DB, TS, D),
            jnp.swapaxes(kv[0], 2, 3), jnp.swapaxes(kv[1], 2, 3),
            stack(st_p, 2, (depth, B, POOL_STATE, WA)), stack(st_p, 3, (depth, B, HH, HEAD, HEAD)),
            stack(st_p, 4, (depth, B, CONV_W - 1, Fd)),
            stack(st_s, 0, (depth, DB, TS, HC, HEAD)), stack(st_s, 1, (depth, DB, TS, HC, HEAD)),
            stack(st_s, 2, (depth, DB, POOL_STATE, WA)), stack(st_s, 3, (depth, DB, HH, HEAD, HEAD)),
            stack(st_s, 4, (depth, DB, CONV_W - 1, Fd)))
```

```python
import functools
import math

import numpy as np
import jax
import jax.numpy as jnp
from jax import lax
from jax.experimental import pallas as pl
from jax.experimental.pallas import tpu as pltpu

F32 = jnp.float32
BF16 = jnp.bfloat16

LANES = 128
SUBLANES = 8
MXU_COLS = 256
VMEM_LIMIT = 48 * 1024 * 1024

POOL_WINDOWS = (2, 4, 8, 16)
POOL_STATE = max(POOL_WINDOWS) - 1
HEAD = 128
QK_HALF = 64
QK_SCALE_LOG2E = QK_HALF ** -0.5 * math.log2(math.e)
ROT_DIM = 16
ROPE_THETA = 500000.0
LB_FLOOR = 1e-30
MASK_VALUE = -1e30
CONV_W = 3
EPS = 1e-6
HG_CHUNK = 64
HG_DIAG = 8
HG_HEADS_PER_STEP = 6
PAGES_PER_STEP = 16
POOL_TN = 256
OPROJ_ROWS = 256


def _params(sem):
    return pltpu.CompilerParams(dimension_semantics=sem, vmem_limit_bytes=VMEM_LIMIT)


def _rms(x, g):
    return x * lax.rsqrt(jnp.mean(x * x, axis=-1, keepdims=True) + EPS) * g


def _sigmoid_pair(z):
    e = jnp.exp(-jnp.abs(z))
    r = 1.0 / (1.0 + e)
    er = e * r
    pos = z >= 0
    return jnp.where(pos, r, er), jnp.where(pos, er, r)


def _norm_kernel(x_ref, g_ref, h_ref):
    h_ref[...] = _rms(x_ref[...], g_ref[...]).astype(BF16)


def rmsnorm_bf16(x, g, tm):
    M, D = x.shape
    return pl.pallas_call(
        _norm_kernel,
        grid=(M // tm,),
        in_specs=[pl.BlockSpec((tm, D), lambda i: (i, 0)),
                  pl.BlockSpec((1, D), lambda i: (0, 0))],
        out_specs=pl.BlockSpec((tm, D), lambda i: (i, 0)),
        out_shape=jax.ShapeDtypeStruct((M, D), BF16),
        compiler_params=_params(("parallel",)),
        name="rmsnorm",
    )(x, g)


def _mm_kernel(x_ref, w_ref, o_ref):
    o_ref[...] = jnp.dot(x_ref[...], w_ref[...], preferred_element_type=F32)


def matmul(x, w, layer, col0, N, tm, tn, name):
    M, K = x.shape
    return pl.pallas_call(
        _mm_kernel,
        grid=(M // tm, N // tn),
        in_specs=[pl.BlockSpec((tm, K), lambda i, j: (i, 0)),
                  pl.BlockSpec((None, K, tn), lambda i, j: (layer, 0, col0 + j))],
        out_specs=pl.BlockSpec((tm, tn), lambda i, j: (i, j)),
        out_shape=jax.ShapeDtypeStruct((M, N), F32),
        compiler_params=_params(("parallel", "arbitrary")),
        name=name,
    )(x, w)


def _pool_kernel(xa_ref, prev_ref, wp_ref, sc_ref, y_ref, st_ref, buf, *, tt, pos0):
    t = pl.program_id(1)
    P = POOL_STATE + 1

    @pl.when(t == 0)
    def _():
        buf[0:1, :] = jnp.zeros((1, buf.shape[1]), F32)
        buf[1:P, :] = prev_ref[...]

    buf[P:P + tt, :] = xa_ref[...]
    pos = pos0 + t * tt + lax.broadcasted_iota(jnp.int32, (tt, LANES), 0)
    for g, w in enumerate(POOL_WINDOWS):
        cols = slice(g * LANES, (g + 1) * LANES)
        x = buf[P:P + tt, cols]
        acc = x
        for j in range(1, w):
            acc = acc + buf[P - j:P - j + tt, cols]
        cnt = jnp.minimum(w, pos + 1).astype(F32)
        d = acc / cnt - x
        y = jnp.dot(d.astype(BF16), wp_ref[g].astype(BF16), preferred_element_type=F32)
        y_ref[:, cols] = (y * sc_ref[:, cols]).astype(BF16)
    tail = buf[tt:tt + P, :]
    st_ref[...] = tail[1:, :]
    buf[0:P, :] = tail


def pool_mixer(xa, prev, wp, scale, pos0):
    B, T, W = xa.shape
    tt = min(T, 512)
    return pl.pallas_call(
        functools.partial(_pool_kernel, tt=tt, pos0=pos0),
        grid=(B, T // tt),
        in_specs=[pl.BlockSpec((None, tt, W), lambda b, t: (b, t, 0)),
                  pl.BlockSpec((None, POOL_STATE, W), lambda b, t: (b, 0, 0)),
                  pl.BlockSpec((len(POOL_WINDOWS), LANES, LANES), lambda b, t: (0, 0, 0)),
                  pl.BlockSpec((1, W), lambda b, t: (0, 0))],
        out_specs=[pl.BlockSpec((None, tt, W), lambda b, t: (b, t, 0)),
                   pl.BlockSpec((None, POOL_STATE, W), lambda b, t: (b, 0, 0))],
        out_shape=[jax.ShapeDtypeStruct((B, T, W), BF16),
                   jax.ShapeDtypeStruct((B, POOL_STATE, W), F32)],
        scratch_shapes=[pltpu.VMEM((POOL_STATE + 1 + tt, W), F32)],
        compiler_params=_params(("parallel", "arbitrary")),
        name="pool_mixer",
    )(xa, prev, wp, scale)


def _rope_kernel(q_ref, k_ref, fl_ref, s1_ref, s2_ref, qs_ref, kr_ref, kb_ref, *, tt, pos0, heads):
    t = pl.program_id(1)
    pos = (pos0 + t * tt + lax.broadcasted_iota(jnp.int32, (tt, LANES), 0)).astype(F32)
    ang = pos * fl_ref[...]
    c = jnp.cos(ang)
    s = jnp.sin(ang)
    sa = s * s1_ref[...]
    sb = s * s2_ref[...]
    half = ROT_DIM // 2
    for h in range(heads):
        cols = slice(h * HEAD, (h + 1) * HEAD)
        xq = q_ref[:, cols]
        yq = xq * c + pltpu.roll(xq, LANES - half, axis=1) * sa + pltpu.roll(xq, half, axis=1) * sb
        qs_ref[:, cols] = (yq * (QK_HALF ** -0.5)).astype(BF16)
        xk = k_ref[:, cols]
        yk = xk * c + pltpu.roll(xk, LANES - half, axis=1) * sa + pltpu.roll(xk, half, axis=1) * sb
        kr_ref[:, cols] = yk
        kb_ref[:, cols] = yk.astype(BF16)


def rope(main, qblk, kblk, width, tabs, pos0):
    B, T, _ = main.shape
    tt = min(T, 512)
    heads = width // HEAD
    tab = pl.BlockSpec((1, LANES), lambda b, t: (0, 0))
    blk = lambda c: pl.BlockSpec((None, tt, width), lambda b, t: (b, t, c))
    return pl.pallas_call(
        functools.partial(_rope_kernel, tt=tt, pos0=pos0, heads=heads),
        grid=(B, T // tt),
        in_specs=[blk(qblk), blk(kblk), tab, tab, tab],
        out_specs=[blk(0), blk(0), blk(0)],
        out_shape=[jax.ShapeDtypeStruct((B, T, width), BF16),
                   jax.ShapeDtypeStruct((B, T, width), F32),
                   jax.ShapeDtypeStruct((B, T, width), BF16)],
        compiler_params=_params(("parallel", "parallel")),
        name="rope",
    )(main, main, *tabs)


def _rope_heads_kernel(q_ref, k_ref, v_ref, fl_ref, s1_ref, s2_ref, kin_ref, vin_ref,
                       qs_ref, kb_ref, vb_ref, ko_ref, vo_ref, *, tt, pos0, heads):
    del kin_ref, vin_ref
    t = pl.program_id(1)
    pos = (pos0 + t * tt + lax.broadcasted_iota(jnp.int32, (tt, LANES), 0)).astype(F32)
    ang = pos * fl_ref[...]
    c = jnp.cos(ang)
    s = jnp.sin(ang)
    sa = s * s1_ref[...]
    sb = s * s2_ref[...]
    half = ROT_DIM // 2
    for h in range(heads):
        cols = slice(h * HEAD, (h + 1) * HEAD)
        xq = q_ref[:, cols]
        yq = xq * c + pltpu.roll(xq, LANES - half, axis=1) * sa + pltpu.roll(xq, half, axis=1) * sb
        qs_ref[h] = (yq * QK_SCALE_LOG2E).astype(BF16)
        xk = k_ref[:, cols]
        yk = xk * c + pltpu.roll(xk, LANES - half, axis=1) * sa + pltpu.roll(xk, half, axis=1) * sb
        ko_ref[h] = yk
        kb_ref[h] = yk.astype(BF16)
        xv = v_ref[:, cols]
        vo_ref[h] = xv
        vb_ref[h, :, 0:HEAD] = xv.astype(BF16)
        vb_ref[h, :, HEAD:2 * HEAD] = jnp.ones((tt, HEAD), BF16)


def rope_heads(main, qblk, width, tabs, pos0, k_stack, v_stack, layer):
    B, T, _ = main.shape
    tt = min(T, 512)
    heads = width // HEAD
    tab = pl.BlockSpec((1, LANES), lambda b, t: (0, 0))
    blk = lambda c: pl.BlockSpec((None, tt, width), lambda b, t: (b, t, c))
    hm = pl.BlockSpec((None, heads, tt, HEAD), lambda b, t: (b, 0, t, 0))
    st = pl.BlockSpec((None, None, heads, tt, HEAD), lambda b, t: (layer, b, 0, t, 0))
    anyspec = pl.BlockSpec(memory_space=pl.ANY)
    hshape = jax.ShapeDtypeStruct((B, heads, T, HEAD), BF16)
    vshape = jax.ShapeDtypeStruct((B, heads, T, 2 * HEAD), BF16)
    vm = pl.BlockSpec((None, heads, tt, 2 * HEAD), lambda b, t: (b, 0, t, 0))
    return pl.pallas_call(
        functools.partial(_rope_heads_kernel, tt=tt, pos0=pos0, heads=heads),
        grid=(B, T // tt),
        in_specs=[blk(qblk), blk(qblk + 1), blk(qblk + 2), tab, tab, tab, anyspec, anyspec],
        out_specs=[hm, hm, vm, st, st],
        out_shape=[hshape, hshape, vshape,
                   jax.ShapeDtypeStruct(k_stack.shape, F32), jax.ShapeDtypeStruct(v_stack.shape, F32)],
        input_output_aliases={6: 3, 7: 4},
        compiler_params=_params(("parallel", "parallel")),
        name="rope_heads",
    )(main, main, main, *tabs, k_stack, v_stack)


def _hgrn_kernel(q_ref, f_ref, i_ref, g_ref, s0_ref, lb_ref, gn_ref, o_ref, so_ref, st_ref,
                 *, tt, chunk, heads, valid):
    t = pl.program_id(2)
    C = chunk
    levels = [n for n in (64, 32, 16) if n <= C]

    @pl.when(t == 0)
    def _():
        for h in range(heads):
            st_ref[h] = s0_ref[h].T

    row = lax.broadcasted_iota(jnp.int32, (C, LANES), 0)
    rr = lax.broadcasted_iota(jnp.int32, (C, C), 0)
    cc = lax.broadcasted_iota(jnp.int32, (C, C), 1)
    lvl_masks = []
    for n in levels:
        sh = int(math.log2(n))
        same = (rr >> sh) == (cc >> sh)
        up = (rr & (n - 1)) >= n // 2
        lo = (cc & (n - 1)) < n // 2
        lvl_masks.append(jnp.where(same, jnp.where(up, jnp.where(lo, 1.0, 0.0), 0.0), 0.0))

    def chunk_body(c, carry):
        r0 = pl.multiple_of(c * C, C)
        rows = pl.ds(r0, C)
        for h in range(heads):
            cols = slice(h * HEAD, (h + 1) * HEAD)
            q = q_ref[rows, cols]
            z = f_ref[rows, cols]
            v = i_ref[rows, cols]
            gate = g_ref[rows, cols]
            lb = lb_ref[:, cols]
            lbm = jnp.maximum(lb, LB_FLOOR)
            oml = 1.0 - lb
            sg, sgn = _sigmoid_pair(z)
            f = lbm + oml * sg
            kin = oml * sgn - (lbm - lb)
            if valid is not None:
                ok = (r0 + row) < valid
                f = jnp.where(ok, f, 1.0)
                kin = jnp.where(ok, kin, 0.0)
            lf = jnp.log(f)
            b = lf
            s = 1
            while s < C:
                b = b + jnp.where(row >= s, pltpu.roll(b, s, axis=0), 0.0)
                s *= 2
            St = st_ref[h]
            qd = q * jnp.exp(b)
            o = lax.dot_general(qd.astype(BF16), St.astype(BF16), (((1,), (1,)), ((), ())),
                                preferred_element_type=F32)
            if levels:
                att = jnp.zeros((C, C), F32)
                for n, msk in zip(levels, lvl_masks):
                    pieces = []
                    for m in range(C // n):
                        rb = m * n + n // 2 - 1
                        pieces.append(jnp.broadcast_to(b[rb:rb + 1, :], (n, LANES)))
                    R = pieces[0] if len(pieces) == 1 else jnp.concatenate(pieces, axis=0)
                    qn = q * jnp.exp(jnp.minimum(b - R, 0.0))
                    kn = kin * jnp.exp(jnp.minimum(R - b, 0.0))
                    a = lax.dot_general(qn.astype(BF16), kn.astype(BF16), (((1,), (1,)), ((), ())),
                                        preferred_element_type=F32)
                    att = att + a * msk
                o = o + jnp.dot(att.astype(BF16), v.astype(BF16), preferred_element_type=F32)
            def back(x, j):
                return pltpu.roll(x.reshape(C // HG_DIAG, HG_DIAG, LANES), j, axis=1).reshape(C, LANES)

            rin = row & (HG_DIAG - 1)
            o = o + jnp.sum(q * kin, axis=-1, keepdims=True) * v
            e = f
            for j in range(1, HG_DIAG):
                if j > 1:
                    e = e * back(f, j - 1)
                kj = jnp.where(rin >= j, back(kin, j), 0.0)
                cj = jnp.sum(q * kj * e, axis=-1, keepdims=True)
                o = o + cj * back(v, j)
            bl = b[C - 1:C, :]
            kd = kin * jnp.exp(bl - b)
            upd = jnp.dot(v.T.astype(BF16), kd.astype(BF16), preferred_element_type=F32)
            st_ref[h] = St * jnp.exp(bl) + upd
            sgg, _ = _sigmoid_pair(gate)
            o_ref[rows, cols] = (_rms(o, gn_ref[:, cols]) * (gate * sgg)).astype(BF16)
        return carry

    lax.fori_loop(0, tt // C, chunk_body, 0)

    @pl.when(t == pl.num_programs(2) - 1)
    def _():
        for h in range(heads):
            so_ref[h] = st_ref[h].T


def hgrn2(main, s0, lb, gn, chunk, valid):
    B, T, _ = main.shape
    heads = s0.shape[1]
    hp = math.gcd(HG_HEADS_PER_STEP, heads)
    G = heads // hp
    W = heads * HEAD
    tt = min(T, 256)
    blk = lambda c: pl.BlockSpec((None, tt, hp * HEAD), lambda b, g, t: (b, t, c * G + g))
    vec = pl.BlockSpec((1, hp * HEAD), lambda b, g, t: (0, g))
    sblk = pl.BlockSpec((None, hp, HEAD, HEAD), lambda b, g, t: (b, g, 0, 0))
    return pl.pallas_call(
        functools.partial(_hgrn_kernel, tt=tt, chunk=chunk, heads=hp, valid=valid),
        grid=(B, G, T // tt),
        in_specs=[blk(0), blk(1), blk(2), blk(3), sblk, vec, vec],
        out_specs=[blk(0), sblk],
        out_shape=[jax.ShapeDtypeStruct((B, T, W), BF16),
                   jax.ShapeDtypeStruct(s0.shape, F32)],
        scratch_shapes=[pltpu.VMEM((hp, HEAD, HEAD), F32)],
        compiler_params=_params(("parallel", "parallel", "arbitrary")),
        name="hgrn2",
    )(main, main, main, main, s0, lb, gn)


def _lambda(lq1, lk1, lq2, lk2, lam_init):
    a = jnp.exp(jnp.sum(lq1 * lk1, axis=-1, keepdims=True))
    b = jnp.exp(jnp.sum(lq2 * lk2, axis=-1, keepdims=True))
    return a - b + lam_init


def _pattn_kernel(qi_ref, kj_ref, q_ref, kc_ref, kn_ref, v_ref, lq1, lk1, lq2, lk2, sub_ref, o_ref,
                  qst, m_ref, acc_ref, s_even, s_odd, *, tq, rs, lam_init):
    n = pl.program_id(2)
    qi = qi_ref[n]
    kj = kj_ref[n]

    def scores(k_ref):
        return lax.dot_general(qst[...], k_ref[...], (((1,), (1,)), ((), ())), preferred_element_type=F32)

    @pl.when(kj == 0)
    def _():
        q = q_ref[...]
        lane = lax.broadcasted_iota(jnp.int32, q.shape, 1)
        zero = jnp.zeros_like(q)
        qst[0:tq, :] = jnp.where(lane < QK_HALF, q, zero)
        qst[tq:2 * tq, :] = jnp.where(lane >= QK_HALF, q, zero)
        m_ref[...] = jnp.full(m_ref.shape, MASK_VALUE, F32)
        acc_ref[...] = jnp.zeros(acc_ref.shape, F32)
        s_even[...] = scores(kc_ref)

    def consume(s_ref, diag):
        for c in range(2 * tq // rs):
            rows = slice(c * rs, (c + 1) * rs)
            q0 = (c * rs) % tq
            kw = q0 + rs if diag else tq
            s = s_ref[rows, 0:kw]
            if diag:
                rq = lax.broadcasted_iota(jnp.int32, s.shape, 0) + q0
                ck = lax.broadcasted_iota(jnp.int32, s.shape, 1)
                s = jnp.where(ck <= rq, s, MASK_VALUE)
            m_prev = m_ref[rows, :]
            m_next = jnp.maximum(m_prev, jnp.max(s, axis=1, keepdims=True))
            p = jnp.exp2(s - m_next[:, 0:1])
            alpha = jnp.exp2(m_prev - m_next)
            pv = jnp.dot(p.astype(BF16), v_ref[0:kw, :], preferred_element_type=F32)
            acc_ref[rows, 0:HEAD] = alpha * acc_ref[rows, 0:HEAD] + pv[:, 0:HEAD]
            acc_ref[rows, HEAD:2 * HEAD] = alpha * acc_ref[rows, HEAD:2 * HEAD] + pv[:, HEAD:2 * HEAD]
            m_ref[rows, :] = m_next

    for parity, cur, nxt in ((0, s_even, s_odd), (1, s_odd, s_even)):
        mine = (kj & 1) == parity

        @pl.when(jnp.logical_and(kj < qi, mine))
        def _():
            nxt[...] = scores(kn_ref)
            consume(cur, False)

        @pl.when(jnp.logical_and(kj == qi, mine))
        def _():
            consume(cur, True)
            lam = _lambda(lq1[...], lk1[...], lq2[...], lk2[...], lam_init)
            o1 = acc_ref[0:tq, 0:HEAD] / acc_ref[0:tq, HEAD:2 * HEAD]
            o2 = acc_ref[tq:2 * tq, 0:HEAD] / acc_ref[tq:2 * tq, HEAD:2 * HEAD]
            o = o1 - lam * o2
            o_ref[...] = (_rms(o, sub_ref[...]) * (1.0 - lam_init)).astype(BF16)


def prompt_attention(qs, kb, vb, lams, subln, lam_init):
    B, heads, T, _ = qs.shape
    tq = min(T, 1024)
    rs = min(tq, 256)
    nq = T // tq
    pairs = [(i, j) for i in range(nq) for j in range(i + 1)]
    qi = jnp.asarray(np.array([p[0] for p in pairs], np.int32))
    kj = jnp.asarray(np.array([p[1] for p in pairs], np.int32))
    lam_spec = pl.BlockSpec((1, QK_HALF), lambda b, h, n, qi, kj: (0, 0))
    npairs = len(pairs)
    kc_spec = pl.BlockSpec((None, None, tq, HEAD), lambda b, h, n, qi, kj: (b, h, kj[n], 0))
    kn_spec = pl.BlockSpec((None, None, tq, HEAD),
                           lambda b, h, n, qi, kj: (b, h, kj[jnp.minimum(n + 1, npairs - 1)], 0))
    grid_spec = pltpu.PrefetchScalarGridSpec(
        num_scalar_prefetch=2,
        grid=(B, heads, len(pairs)),
        in_specs=[pl.BlockSpec((None, None, tq, HEAD), lambda b, h, n, qi, kj: (b, h, qi[n], 0)),
                  kc_spec, kn_spec,
                  pl.BlockSpec((None, None, tq, 2 * HEAD), lambda b, h, n, qi, kj: (b, h, kj[n], 0)),
                  lam_spec, lam_spec, lam_spec, lam_spec,
                  pl.BlockSpec((1, HEAD), lambda b, h, n, qi, kj: (0, h))],
        out_specs=pl.BlockSpec((None, tq, HEAD), lambda b, h, n, qi, kj: (b, qi[n], h)),
        scratch_shapes=[pltpu.VMEM((2 * tq, HEAD), BF16),
                        pltpu.VMEM((2 * tq, HEAD), F32),
                        pltpu.VMEM((2 * tq, 2 * HEAD), F32),
                        pltpu.VMEM((2 * tq, tq), F32),
                        pltpu.VMEM((2 * tq, tq), F32)],
    )
    return pl.pallas_call(
        functools.partial(_pattn_kernel, tq=tq, rs=rs, lam_init=lam_init),
        grid_spec=grid_spec,
        out_shape=jax.ShapeDtypeStruct((B, T, heads * HEAD), BF16),
        compiler_params=_params(("parallel", "parallel", "arbitrary")),
        name="prompt_attention",
    )(qi, kj, qs, kb, kb, vb, *lams, subln)


def _sattn_kernel(pt_ref, q_ref, kn_ref, vn_ref, lq1, lk1, lq2, lk2, sub_ref, *rest,
                  heads, tnew, lam_init, pps):
    k_refs = rest[:pps]
    v_refs = rest[pps:2 * pps]
    o_ref = rest[2 * pps]
    qbd, m_ref, l_ref, acc_ref = rest[2 * pps + 1:]
    j = pl.program_id(1)
    R = heads * 2 * SUBLANES
    W = heads * HEAD

    @pl.when(j == 0)
    def _():
        q = q_ref[...]
        qrep = jnp.concatenate([q] * (heads * 2), axis=0)
        lane = lax.broadcasted_iota(jnp.int32, (R, W), 1)
        rowi = lax.broadcasted_iota(jnp.int32, (R, W), 0)
        qbd[...] = jnp.where((lane >> 6) == (rowi >> 3), qrep, 0.0)
        m_ref[...] = jnp.full(m_ref.shape, MASK_VALUE, F32)
        l_ref[...] = jnp.zeros(l_ref.shape, F32)
        acc_ref[...] = jnp.zeros(acc_ref.shape, F32)

    def page_rows(ref):
        return jnp.concatenate([ref[h] for h in range(heads)], axis=1).astype(BF16)

    qb = qbd[...].astype(BF16)
    ss = []
    for p in range(pps):
        ss.append(lax.dot_general(qb, page_rows(k_refs[p]), (((1,), (1,)), ((), ())),
                                  preferred_element_type=F32))
    m_prev = m_ref[...]
    m_cur = ss[0].max(axis=1, keepdims=True)
    for p in range(1, pps):
        m_cur = jnp.maximum(m_cur, ss[p].max(axis=1, keepdims=True))
    m_next = jnp.maximum(m_prev, m_cur)
    alpha = jnp.exp(m_prev - m_next)
    l_new = alpha * l_ref[...]
    acc = alpha[:, 0:1] * acc_ref[...]
    for p in range(pps):
        pr = jnp.exp(ss[p] - m_next[:, 0:1])
        l_new = l_new + jnp.sum(pr, axis=1, keepdims=True)
        acc = acc + jnp.dot(pr.astype(BF16), page_rows(v_refs[p]), preferred_element_type=F32)
    m_ref[...] = m_next
    l_ref[...] = l_new
    acc_ref[...] = acc

    @pl.when(j == pl.num_programs(1) - 1)
    def _():
        qf = qbd[...]
        tok = lax.broadcasted_iota(jnp.int32, (R, LANES), 0) & (SUBLANES - 1)
        m_p = m_ref[...]
        sn = []
        for jn in range(tnew):
            sj = jnp.sum(qf * kn_ref[jn:jn + 1, :], axis=1, keepdims=True)
            sn.append(jnp.where(tok >= jn, sj, MASK_VALUE))
        m_n = m_p
        for sj in sn:
            m_n = jnp.maximum(m_n, sj)
        al = jnp.exp(m_p - m_n)
        l_f = al * l_ref[...]
        acc_f = al[:, 0:1] * acc_ref[...]
        for jn in range(tnew):
            pj = jnp.exp(sn[jn] - m_n)
            l_f = l_f + pj
            acc_f = acc_f + pj[:, 0:1] * vn_ref[jn:jn + 1, :]
        lam = _lambda(lq1[...], lk1[...], lq2[...], lk2[...], lam_init)
        for h in range(heads):
            cols = slice(h * HEAD, (h + 1) * HEAD)
            r1 = slice(h * 2 * SUBLANES, h * 2 * SUBLANES + SUBLANES)
            r2 = slice(h * 2 * SUBLANES + SUBLANES, (h + 1) * 2 * SUBLANES)
            o1 = acc_f[r1, cols] / l_f[r1, :]
            o2 = acc_f[r2, cols] / l_f[r2, :]
            o = o1 - lam * o2
            o_ref[:, cols] = (_rms(o, sub_ref[:, cols]) * (1.0 - lam_init)).astype(BF16)


def sample_attention(qs8, kn8, vn8, cache_k, cache_v, layer, page_table, lams, subln, lam_init, tnew):
    DB, _, W = qs8.shape
    heads = W // HEAD
    page = cache_k.shape[3]
    n_pages = page_table.shape[1]
    pps = math.gcd(PAGES_PER_STEP, n_pages)
    R = heads * 2 * SUBLANES
    tok_spec = pl.BlockSpec((None, SUBLANES, W), lambda b, j, pt: (b, 0, 0))
    lam_spec = pl.BlockSpec((1, QK_HALF), lambda b, j, pt: (0, 0))

    def page_spec(p):
        return pl.BlockSpec((None, None, heads, page, HEAD),
                            lambda b, j, pt: (layer, pt[b, j * pps + p], 0, 0, 0))

    grid_spec = pltpu.PrefetchScalarGridSpec(
        num_scalar_prefetch=1,
        grid=(DB, n_pages // pps),
        in_specs=[tok_spec, tok_spec, tok_spec, lam_spec, lam_spec, lam_spec, lam_spec,
                  pl.BlockSpec((1, W), lambda b, j, pt: (0, 0))]
                 + [page_spec(p) for p in range(pps)] + [page_spec(p) for p in range(pps)],
        out_specs=tok_spec,
        scratch_shapes=[pltpu.VMEM((R, W), F32),
                        pltpu.VMEM((R, LANES), F32),
                        pltpu.VMEM((R, LANES), F32),
                        pltpu.VMEM((R, W), F32)],
    )
    return pl.pallas_call(
        functools.partial(_sattn_kernel, heads=heads, tnew=tnew, lam_init=lam_init, pps=pps),
        grid_spec=grid_spec,
        out_shape=jax.ShapeDtypeStruct((DB, SUBLANES, W), BF16),
        compiler_params=_params(("parallel", "arbitrary")),
        name="sample_attention",
    )(page_table, qs8, kn8, vn8, *lams, subln, *([cache_k] * pps), *([cache_v] * pps))


def _oproj_kernel(a_ref, b_ref, c_ref, w_ref, x_ref, gpost_ref, gnext_ref, xo_ref, h_ref, *, wa, wb):
    tm = x_ref.shape[0]
    rg = min(tm, OPROJ_ROWS)
    ys = []
    for r in range(tm // rg):
        rows = slice(r * rg, (r + 1) * rg)
        y = jnp.dot(a_ref[rows, :], w_ref[0:wa, :], preferred_element_type=F32)
        y = y + jnp.dot(b_ref[rows, :], w_ref[wa:wa + wb, :], preferred_element_type=F32)
        ys.append(y + jnp.dot(c_ref[rows, :], w_ref[wa + wb:, :], preferred_element_type=F32))
    for r in range(tm // rg):
        rows = slice(r * rg, (r + 1) * rg)
        x = x_ref[rows, :] + _rms(ys[r], gpost_ref[...])
        xo_ref[rows, :] = x
        h_ref[rows, :] = _rms(x, gnext_ref[...]).astype(BF16)


def out_proj(a, b, c, w, layer, x, gpost, gnext, tm):
    M, D = x.shape
    wa, wb, wc = a.shape[1], b.shape[1], c.shape[1]
    row = lambda width: pl.BlockSpec((tm, width), lambda i: (i, 0))
    vec = pl.BlockSpec((1, D), lambda i: (0, 0))
    return pl.pallas_call(
        functools.partial(_oproj_kernel, wa=wa, wb=wb),
        grid=(M // tm,),
        in_specs=[row(wa), row(wb), row(wc), pl.BlockSpec((None, D, D), lambda i: (layer, 0, 0)), row(D), vec, vec],
        out_specs=[row(D), row(D)],
        out_shape=[jax.ShapeDtypeStruct((M, D), F32), jax.ShapeDtypeStruct((M, D), BF16)],
        compiler_params=_params(("parallel",)),
        name="out_proj",
    )(a, b, c, w, x, gpost, gnext)


def _ffn_up_seq_kernel(h_ref, wg_ref, wu_ref, cw_ref, cb_ref, prev_ref, act_ref, st_ref,
                       wgb, wub, buf, *, tm, groups):
    b = pl.program_id(1)
    t = pl.program_id(2)
    H = SUBLANES
    tn = buf.shape[1]
    gw = tn // groups

    @pl.when(t == 0)
    def _():
        @pl.when(b == 0)
        def _():
            wgb[...] = wg_ref[...].astype(BF16)
            wub[...] = wu_ref[...].astype(BF16)

        buf[0:H - 2, :] = jnp.zeros((H - 2, tn), F32)
        buf[H - 2:H, :] = prev_ref[...]

    h = h_ref[...]
    prods = []
    for g in range(groups):
        cols = slice(g * gw, (g + 1) * gw)
        prods.append((jnp.dot(h, wgb[:, cols], preferred_element_type=F32),
                      jnp.dot(h, wub[:, cols], preferred_element_type=F32)))
    for g in range(groups):
        cols = slice(g * gw, (g + 1) * gw)
        u, up = prods[g]
        buf[H:H + tm, cols] = u
        c = cb_ref[:, cols] + cw_ref[2:3, cols] * u
        c = c + cw_ref[1:2, cols] * buf[H - 1:H - 1 + tm, cols]
        c = c + cw_ref[0:1, cols] * buf[H - 2:H - 2 + tm, cols]
        act_ref[:, cols] = (c * (1.0 / (1.0 + jnp.exp(-c))) * up).astype(BF16)
    tail = buf[tm:tm + H, :]
    st_ref[...] = tail[H - 2:, :]
    buf[0:H, :] = tail


def ffn_up_seq(h3, wg, wu, layer, cw, cb, prev, tn):
    B, T, D = h3.shape
    Fd = wg.shape[2]
    tm = min(T, 512)
    groups = max(1, tn // MXU_COLS)
    wspec = pl.BlockSpec((None, D, tn), lambda j, b, t: (layer, 0, j))
    return pl.pallas_call(
        functools.partial(_ffn_up_seq_kernel, tm=tm, groups=groups),
        grid=(Fd // tn, B, T // tm),
        in_specs=[pl.BlockSpec((None, tm, D), lambda j, b, t: (b, t, 0)), wspec, wspec,
                  pl.BlockSpec((CONV_W, tn), lambda j, b, t: (0, j)),
                  pl.BlockSpec((1, tn), lambda j, b, t: (0, j)),
                  pl.BlockSpec((None, CONV_W - 1, tn), lambda j, b, t: (b, 0, j))],
        out_specs=[pl.BlockSpec((None, tm, tn), lambda j, b, t: (b, t, j)),
                   pl.BlockSpec((None, CONV_W - 1, tn), lambda j, b, t: (b, 0, j))],
        out_shape=[jax.ShapeDtypeStruct((B, T, Fd), BF16),
                   jax.ShapeDtypeStruct((B, CONV_W - 1, Fd), F32)],
        scratch_shapes=[pltpu.VMEM((D, tn), BF16), pltpu.VMEM((D, tn), BF16),
                        pltpu.VMEM((SUBLANES + tm, tn), F32)],
        compiler_params=_params(("arbitrary", "arbitrary", "arbitrary")),
        name="ffn_up_prompt",
    )(h3, wg, wu, cw, cb, prev)


def _ffn_up_short_kernel(h_ref, wg_ref, wu_ref, cw_ref, cb_ref, p1_ref, p2_ref, act_ref, u_ref, *, T):
    h = h_ref[...]
    u = jnp.dot(h, wg_ref[...].astype(BF16), preferred_element_type=F32)
    up = jnp.dot(h, wu_ref[...].astype(BF16), preferred_element_type=F32)
    u_ref[...] = u
    r = lax.broadcasted_iota(jnp.int32, u.shape, 0) & (T - 1)
    um1 = jnp.where(r >= 1, pltpu.roll(u, 1, axis=0), p1_ref[...])
    um2 = jnp.where(r >= 2, pltpu.roll(u, 2, axis=0), p2_ref[...])
    c = cb_ref[...] + cw_ref[2:3, :] * u + cw_ref[1:2, :] * um1 + cw_ref[0:1, :] * um2
    sg, _ = _sigmoid_pair(c)
    act_ref[...] = (c * sg * up).astype(BF16)


def ffn_up_short(h, wg, wu, layer, cw, cb, p1, p2, T, tn):
    M, D = h.shape
    Fd = wg.shape[2]
    wspec = pl.BlockSpec((None, D, tn), lambda j: (layer, 0, j))
    col = pl.BlockSpec((M, tn), lambda j: (0, j))
    return pl.pallas_call(
        functools.partial(_ffn_up_short_kernel, T=T),
        grid=(Fd // tn,),
        in_specs=[pl.BlockSpec((M, D), lambda j: (0, 0)), wspec, wspec,
                  pl.BlockSpec((CONV_W, tn), lambda j: (0, j)),
                  pl.BlockSpec((1, tn), lambda j: (0, j)), col, col],
        out_specs=[col, col],
        out_shape=[jax.ShapeDtypeStruct((M, Fd), BF16), jax.ShapeDtypeStruct((M, Fd), F32)],
        compiler_params=_params(("parallel",)),
        name="ffn_up_sample",
    )(h, wg, wu, cw, cb, p1, p2)


def _ffn_down_kernel(a_ref, w_ref, x_ref, gpost_ref, gnext_ref, xo_ref, h_ref, acc_ref):
    k = pl.program_id(1)

    @pl.when(k == 0)
    def _():
        acc_ref[...] = jnp.zeros(acc_ref.shape, F32)

    acc_ref[...] += jnp.dot(a_ref[...], w_ref[...], preferred_element_type=F32)

    @pl.when(k == pl.num_programs(1) - 1)
    def _():
        x = x_ref[...] + _rms(acc_ref[...], gpost_ref[...])
        xo_ref[...] = x
        h_ref[...] = _rms(x, gnext_ref[...]).astype(BF16)


def ffn_down(act, w, layer, x, gpost, gnext, tm, tk):
    M, D = x.shape
    Fd = act.shape[1]
    row = pl.BlockSpec((tm, D), lambda i, k: (i, 0))
    vec = pl.BlockSpec((1, D), lambda i, k: (0, 0))
    return pl.pallas_call(
        _ffn_down_kernel,
        grid=(M // tm, Fd // tk),
        in_specs=[pl.BlockSpec((tm, tk), lambda i, k: (i, k)),
                  pl.BlockSpec((None, tk, D), lambda i, k: (layer, k, 0)), row, vec, vec],
        out_specs=[row, row],
        out_shape=[jax.ShapeDtypeStruct((M, D), F32), jax.ShapeDtypeStruct((M, D), BF16)],
        scratch_shapes=[pltpu.VMEM((tm, D), F32)],
        compiler_params=_params(("parallel", "arbitrary")),
        name="ffn_down",
    )(act, w, x, gpost, gnext)


def _largest_tile(n, cap, quantum):
    best = quantum
    for t in range(quantum, cap + 1, quantum):
        if n % t == 0:
            best = t
    return best


def _layer(x, h, l, lw, *, seqs, T, pos0, prev_pool, prev_S, prev_conv, paged, kv_stacks=None):
    M, D = x.shape
    B = seqs
    tm = min(M, 1024)
    tm2 = min(M, 2 * OPROJ_ROWS)
    WA = lw['pool_scale'].shape[1]
    WH = lw['hgrn_norm'].shape[1]
    WC = lw['attn_subln'].shape[1]
    lam_init = 0.8 - 0.6 * math.exp(-0.3 * l)
    lams = (lw['lam_q1'], lw['lam_k1'], lw['lam_q2'], lw['lam_k2'])

    WM = lw['w_in'].shape[2] - WA
    xa = matmul(h, lw['w_in'], l, WM // POOL_TN, WA, tm, POOL_TN, "proj_pool").reshape(B, T, WA)
    main = matmul(h, lw['w_in'], l, 0, WM, tm, WH, "proj_main").reshape(B, T, WM)

    a_out, pool_new = pool_mixer(xa, prev_pool, lw['pool_w'], lw['pool_scale'], pos0)

    if paged is None:
        o_b, S_new = hgrn2(main, prev_S, lw['hgrn_lb'], lw['hgrn_norm'], min(HG_CHUNK, T), None)
        qs, kb, vb, k_rot, v = rope_heads(main, 4, WC, lw['rope_tabs'], pos0, *kv_stacks, l)
        o_c = prompt_attention(qs, kb, vb, lams, lw['attn_subln'], lam_init)
    else:
        pad = ((0, 0), (0, SUBLANES - T), (0, 0))
        main8 = jnp.pad(main, pad)
        o_b8, S_new = hgrn2(main8, prev_S, lw['hgrn_lb'], lw['hgrn_norm'], SUBLANES, T)
        o_b = o_b8[:, :T]
        qs8, k_rot8, _ = rope(main8, 4, 5, WC, lw['rope_tabs'], pos0)
        v = main[:, :, 6 * WC:]
        cache_k, cache_v, page_table = paged
        o_c8 = sample_attention(qs8.astype(F32), k_rot8, jnp.pad(v, pad), cache_k, cache_v, l,
                                page_table, lams, lw['attn_subln'], lam_init, T)
        o_c = o_c8[:, :T]
        k_rot = k_rot8[:, :T]

    x, h2 = out_proj(a_out.reshape(M, WA), o_b.reshape(M, WH), o_c.reshape(M, WC), lw['w_out'], l,
                     x, lw['norm_mix_post'], lw['norm_ffn_pre'], tm2)

    Fd = lw['w_gate'].shape[2]
    tn = _largest_tile(Fd, 512, LANES)
    if paged is None:
        act, conv_new = ffn_up_seq(h2.reshape(B, T, D), lw['w_gate'], lw['w_up'], l, lw['conv_w'],
                                   lw['conv_b'], prev_conv, tn)
        act = act.reshape(M, Fd)
    else:
        z = jnp.zeros((B, T, Fd), F32)
        p1 = z.at[:, 0].set(prev_conv[:, 1]).reshape(M, Fd)
        p2 = z.at[:, 0].set(prev_conv[:, 0]).at[:, 1].set(prev_conv[:, 1]).reshape(M, Fd)
        act, u = ffn_up_short(h2, lw['w_gate'], lw['w_up'], l, lw['conv_w'], lw['conv_b'], p1, p2, T, tn)
        conv_new = u.reshape(B, T, Fd)[:, T - (CONV_W - 1):]
    tk = _largest_tile(Fd, 1536, LANES)
    x, h_next = ffn_down(act, lw['w_down'], l, x, lw['norm_ffn_post'], lw['norm_next'], min(M, 512), tk)
    return x, h_next, (k_rot, v, pool_new, S_new, conv_new)


def kernel(x_prompt, x_sample, cache_k, cache_v, state_pool, state_hgrn, state_conv, page_table, w_in, w_out, norm_mix_pre, norm_mix_post, norm_ffn_pre, norm_ffn_post, pool_w, pool_scale, hgrn_lower_bounds, hgrn_norm, attn_lam_q1, attn_lam_k1, attn_lam_q2, attn_lam_k2, attn_subln, ffn_w_gate, ffn_w_up, ffn_conv_w, ffn_conv_b, ffn_w_down):
    depth = w_in.shape[0]
    B, T, D = x_prompt.shape
    DB, TS, _ = x_sample.shape
    page = cache_k.shape[2]
    P = page_table.shape[1] * page
    WA = pool_scale.shape[1]
    WH = hgrn_norm.shape[1]
    WC = attn_subln.shape[1]
    HH = WH // HEAD
    Fd = ffn_w_gate.shape[2]
    assert TS >= CONV_W - 1 and TS <= SUBLANES and (TS & (TS - 1)) == 0

    sm = jax.nn.softmax(hgrn_lower_bounds.astype(F32), axis=0)
    lbs = jnp.cumsum(sm, axis=0) - sm[0]

    half = ROT_DIM // 2
    freqs = jnp.power(ROPE_THETA, -jnp.arange(0, ROT_DIM, 2, dtype=F32) / ROT_DIM)
    lane = np.arange(LANES)
    in_rot = (lane % QK_HALF) < ROT_DIM
    fl = jnp.where(jnp.asarray(in_rot), jnp.tile(freqs, LANES // half), 0.0).reshape(1, LANES)
    s1 = jnp.asarray(np.where(in_rot & ((lane % QK_HALF) < half), -1.0, 0.0), F32).reshape(1, LANES)
    s2 = jnp.asarray(np.where(in_rot & ((lane % QK_HALF) >= half), 1.0, 0.0), F32).reshape(1, LANES)

    ones = jnp.ones((1, D), F32)
    HC = WC // HEAD
    ck = jnp.swapaxes(cache_k, 2, 3)
    cv = jnp.swapaxes(cache_v, 2, 3)
    kv = (jnp.zeros((depth, B, HC, T, HEAD), F32), jnp.zeros((depth, B, HC, T, HEAD), F32))

    w_in_b = jnp.concatenate([w_in[:, :, WA:], w_in[:, :, :WA]], axis=2).astype(BF16)
    w_out_b = w_out.astype(BF16)
    w_down_b = ffn_w_down.astype(BF16)

    def layer_weights(l):
        return {
            'w_in': w_in_b,
            'w_out': w_out_b,
            'w_gate': ffn_w_gate,
            'w_up': ffn_w_up,
            'w_down': w_down_b,
            'pool_w': pool_w[l],
            'pool_scale': pool_scale[l].reshape(1, WA),
            'hgrn_lb': lbs[l].reshape(1, WH),
            'hgrn_norm': hgrn_norm[l].reshape(1, WH),
            'attn_subln': attn_subln[l].reshape(1, WC),
            'lam_q1': attn_lam_q1[l].reshape(1, QK_HALF),
            'lam_k1': attn_lam_k1[l].reshape(1, QK_HALF),
            'lam_q2': attn_lam_q2[l].reshape(1, QK_HALF),
            'lam_k2': attn_lam_k2[l].reshape(1, QK_HALF),
            'norm_mix_post': norm_mix_post[l].reshape(1, D),
            'norm_ffn_pre': norm_ffn_pre[l].reshape(1, D),
            'norm_ffn_post': norm_ffn_post[l].reshape(1, D),
            'norm_next': norm_mix_pre[l + 1].reshape(1, D) if l + 1 < depth else ones,
            'conv_w': ffn_conv_w[l],
            'conv_b': ffn_conv_b[l].reshape(1, Fd),
            'rope_tabs': (fl, s1, s2),
        }

    xp = x_prompt.reshape(B * T, D)
    xs = x_sample.reshape(DB * TS, D)
    g0 = norm_mix_pre[0].reshape(1, D)
    hp = rmsnorm_bf16(xp, g0, min(B * T, 512))
    hs = rmsnorm_bf16(xs, g0, DB * TS)
    zp_pool = jnp.zeros((B, POOL_STATE, WA), F32)
    zp_S = jnp.zeros((B, HH, HEAD, HEAD), F32)
    zp_conv = jnp.zeros((B, CONV_W - 1, Fd), F32)

    st_p, st_s = [], []
    for l in range(depth):
        lw = layer_weights(l)
        xp, hp, sp = _layer(xp, hp, l, lw, seqs=B, T=T, pos0=0, prev_pool=zp_pool, prev_S=zp_S,
                            prev_conv=zp_conv, paged=None, kv_stacks=kv)
        kv = sp[:2]
        xs, hs, ss = _layer(xs, hs, l, lw, seqs=DB, T=TS, pos0=P, prev_pool=state_pool[l],
                            prev_S=state_hgrn[l], prev_conv=state_conv[l],
                            paged=(ck, cv, page_table))
        st_p.append(sp)
        st_s.append(ss)

    def stack(sts, i, shape):
        return jnp.stack([s[i] for s in sts]).reshape(shape)

    return (xp.reshape(B, T, D), xs.reshape(DB, TS, D),
            jnp.swapaxes(kv[0], 2, 3), jnp.swapaxes(kv[1], 2, 3),
            stack(st_p, 2, (depth, B, POOL_STATE, WA)), stack(st_p, 3, (depth, B, HH, HEAD, HEAD)),
            stack(st_p, 4, (depth, B, CONV_W - 1, Fd)),
            stack(st_s, 0, (depth, DB, TS, HC, HEAD)), stack(st_s, 1, (depth, DB, TS, HC, HEAD)),
            stack(st_s, 2, (depth, DB, POOL_STATE, WA)), stack(st_s, 3, (depth, DB, HH, HEAD, HEAD)),
            stack(st_s, 4, (depth, DB, CONV_W - 1, Fd)))
```

```python
import functools
import math

import numpy as np
import jax
import jax.numpy as jnp
from jax import lax
from jax.experimental import pallas as pl
from jax.experimental.pallas import tpu as pltpu

F32 = jnp.float32
BF16 = jnp.bfloat16

LANES = 128
SUBLANES = 8
MXU_COLS = 256
VMEM_LIMIT = 48 * 1024 * 1024

POOL_WINDOWS = (2, 4, 8, 16)
POOL_STATE = max(POOL_WINDOWS) - 1
HEAD = 128
QK_HALF = 64
QK_SCALE_LOG2E = QK_HALF ** -0.5 * math.log2(math.e)
ROT_DIM = 16
ROPE_THETA = 500000.0
LB_FLOOR = 1e-30
MASK_VALUE = -1e30
CONV_W = 3
EPS = 1e-6
HG_CHUNK = 64
HG_DIAG = 8
HG_HEADS_PER_STEP = 6
PAGES_PER_STEP = 16
OPROJ_ROWS = 256
FFN_UP_ROWS = 512
FFN_DOWN_ROWS = 256


def _params(sem):
    return pltpu.CompilerParams(dimension_semantics=sem, vmem_limit_bytes=VMEM_LIMIT)


def _rms(x, g):
    return x * lax.rsqrt(jnp.mean(x * x, axis=-1, keepdims=True) + EPS) * g


def _sigmoid_pair(z):
    e = jnp.exp(-jnp.abs(z))
    r = 1.0 / (1.0 + e)
    er = e * r
    pos = z >= 0
    return jnp.where(pos, r, er), jnp.where(pos, er, r)


def _norm_kernel(x_ref, g_ref, h_ref):
    h_ref[...] = _rms(x_ref[...], g_ref[...]).astype(BF16)


def rmsnorm_bf16(x, g, tm):
    M, D = x.shape
    return pl.pallas_call(
        _norm_kernel,
        grid=(M // tm,),
        in_specs=[pl.BlockSpec((tm, D), lambda i: (i, 0)),
                  pl.BlockSpec((1, D), lambda i: (0, 0))],
        out_specs=pl.BlockSpec((tm, D), lambda i: (i, 0)),
        out_shape=jax.ShapeDtypeStruct((M, D), BF16),
        compiler_params=_params(("parallel",)),
        name="rmsnorm",
    )(x, g)


def _mm_kernel(x_ref, w_ref, o_ref):
    o_ref[...] = jnp.dot(x_ref[...], w_ref[...], preferred_element_type=F32)


def matmul(x, w, layer, col0, N, tm, tn, name):
    M, K = x.shape
    return pl.pallas_call(
        _mm_kernel,
        grid=(M // tm, N // tn),
        in_specs=[pl.BlockSpec((tm, K), lambda i, j: (i, 0)),
                  pl.BlockSpec((None, K, tn), lambda i, j: (layer, 0, col0 + j))],
        out_specs=pl.BlockSpec((tm, tn), lambda i, j: (i, j)),
        out_shape=jax.ShapeDtypeStruct((M, N), F32),
        compiler_params=_params(("parallel", "arbitrary")),
        name=name,
    )(x, w)


def _pool_kernel(xa_ref, prev_ref, wp_ref, sc_ref, y_ref, st_ref, buf, *, tt, pos0):
    t = pl.program_id(1)
    P = POOL_STATE + 1

    @pl.when(t == 0)
    def _():
        buf[0:1, :] = jnp.zeros((1, buf.shape[1]), F32)
        buf[1:P, :] = prev_ref[...]

    buf[P:P + tt, :] = xa_ref[...]
    pos = pos0 + t * tt + lax.broadcasted_iota(jnp.int32, (tt, LANES), 0)
    for g, w in enumerate(POOL_WINDOWS):
        cols = slice(g * LANES, (g + 1) * LANES)
        x = buf[P:P + tt, cols]
        acc = x
        for j in range(1, w):
            acc = acc + buf[P - j:P - j + tt, cols]
        cnt = jnp.minimum(w, pos + 1).astype(F32)
        d = acc / cnt - x
        y = jnp.dot(d.astype(BF16), wp_ref[g].astype(BF16), preferred_element_type=F32)
        y_ref[:, cols] = (y * sc_ref[:, cols]).astype(BF16)
    tail = buf[tt:tt + P, :]
    st_ref[...] = tail[1:, :]
    buf[0:P, :] = tail


def pool_mixer(xa, prev, wp, scale, pos0):
    B, T, W = xa.shape
    tt = min(T, 512)
    return pl.pallas_call(
        functools.partial(_pool_kernel, tt=tt, pos0=pos0),
        grid=(B, T // tt),
        in_specs=[pl.BlockSpec((None, tt, W), lambda b, t: (b, t, 0)),
                  pl.BlockSpec((None, POOL_STATE, W), lambda b, t: (b, 0, 0)),
                  pl.BlockSpec((len(POOL_WINDOWS), LANES, LANES), lambda b, t: (0, 0, 0)),
                  pl.BlockSpec((1, W), lambda b, t: (0, 0))],
        out_specs=[pl.BlockSpec((None, tt, W), lambda b, t: (b, t, 0)),
                   pl.BlockSpec((None, POOL_STATE, W), lambda b, t: (b, 0, 0))],
        out_shape=[jax.ShapeDtypeStruct((B, T, W), BF16),
                   jax.ShapeDtypeStruct((B, POOL_STATE, W), F32)],
        scratch_shapes=[pltpu.VMEM((POOL_STATE + 1 + tt, W), F32)],
        compiler_params=_params(("parallel", "arbitrary")),
        name="pool_mixer",
    )(xa, prev, wp, scale)


def _rope_kernel(q_ref, k_ref, fl_ref, s1_ref, s2_ref, qs_ref, kr_ref, kb_ref, *, tt, pos0, heads):
    t = pl.program_id(1)
    pos = (pos0 + t * tt + lax.broadcasted_iota(jnp.int32, (tt, LANES), 0)).astype(F32)
    ang = pos * fl_ref[...]
    c = jnp.cos(ang)
    s = jnp.sin(ang)
    sa = s * s1_ref[...]
    sb = s * s2_ref[...]
    half = ROT_DIM // 2
    for h in range(heads):
        cols = slice(h * HEAD, (h + 1) * HEAD)
        xq = q_ref[:, cols]
        yq = xq * c + pltpu.roll(xq, LANES - half, axis=1) * sa + pltpu.roll(xq, half, axis=1) * sb
        qs_ref[:, cols] = (yq * (QK_HALF ** -0.5)).astype(BF16)
        xk = k_ref[:, cols]
        yk = xk * c + pltpu.roll(xk, LANES - half, axis=1) * sa + pltpu.roll(xk, half, axis=1) * sb
        kr_ref[:, cols] = yk
        kb_ref[:, cols] = yk.astype(BF16)


def rope(main, qblk, kblk, width, tabs, pos0):
    B, T, _ = main.shape
    tt = min(T, 512)
    heads = width // HEAD
    tab = pl.BlockSpec((1, LANES), lambda b, t: (0, 0))
    blk = lambda c: pl.BlockSpec((None, tt, width), lambda b, t: (b, t, c))
    return pl.pallas_call(
        functools.partial(_rope_kernel, tt=tt, pos0=pos0, heads=heads),
        grid=(B, T // tt),
        in_specs=[blk(qblk), blk(kblk), tab, tab, tab],
        out_specs=[blk(0), blk(0), blk(0)],
        out_shape=[jax.ShapeDtypeStruct((B, T, width), BF16),
                   jax.ShapeDtypeStruct((B, T, width), F32),
                   jax.ShapeDtypeStruct((B, T, width), BF16)],
        compiler_params=_params(("parallel", "parallel")),
        name="rope",
    )(main, main, *tabs)


def _rope_heads_kernel(q_ref, k_ref, v_ref, fl_ref, s1_ref, s2_ref, kin_ref, vin_ref,
                       qs_ref, kb_ref, vb_ref, ko_ref, vo_ref, *, tt, pos0, heads):
    del kin_ref, vin_ref
    t = pl.program_id(1)
    pos = (pos0 + t * tt + lax.broadcasted_iota(jnp.int32, (tt, LANES), 0)).astype(F32)
    ang = pos * fl_ref[...]
    c = jnp.cos(ang)
    s = jnp.sin(ang)
    sa = s * s1_ref[...]
    sb = s * s2_ref[...]
    half = ROT_DIM // 2
    for h in range(heads):
        cols = slice(h * HEAD, (h + 1) * HEAD)
        xq = q_ref[:, cols]
        yq = xq * c + pltpu.roll(xq, LANES - half, axis=1) * sa + pltpu.roll(xq, half, axis=1) * sb
        qs_ref[h] = (yq * QK_SCALE_LOG2E).astype(BF16)
        xk = k_ref[:, cols]
        yk = xk * c + pltpu.roll(xk, LANES - half, axis=1) * sa + pltpu.roll(xk, half, axis=1) * sb
        ko_ref[h] = yk
        kb_ref[h] = yk.astype(BF16)
        xv = v_ref[:, cols]
        vo_ref[h] = xv
        vb_ref[h, :, 0:HEAD] = xv.astype(BF16)
        vb_ref[h, :, HEAD:2 * HEAD] = jnp.ones((tt, HEAD), BF16)


def rope_heads(main, qblk, width, tabs, pos0, k_stack, v_stack, layer):
    B, T, _ = main.shape
    tt = min(T, 512)
    heads = width // HEAD
    tab = pl.BlockSpec((1, LANES), lambda b, t: (0, 0))
    blk = lambda c: pl.BlockSpec((None, tt, width), lambda b, t: (b, t, c))
    hm = pl.BlockSpec((None, heads, tt, HEAD), lambda b, t: (b, 0, t, 0))
    st = pl.BlockSpec((None, None, heads, tt, HEAD), lambda b, t: (layer, b, 0, t, 0))
    anyspec = pl.BlockSpec(memory_space=pl.ANY)
    hshape = jax.ShapeDtypeStruct((B, heads, T, HEAD), BF16)
    vshape = jax.ShapeDtypeStruct((B, heads, T, 2 * HEAD), BF16)
    vm = pl.BlockSpec((None, heads, tt, 2 * HEAD), lambda b, t: (b, 0, t, 0))
    return pl.pallas_call(
        functools.partial(_rope_heads_kernel, tt=tt, pos0=pos0, heads=heads),
        grid=(B, T // tt),
        in_specs=[blk(qblk), blk(qblk + 1), blk(qblk + 2), tab, tab, tab, anyspec, anyspec],
        out_specs=[hm, hm, vm, st, st],
        out_shape=[hshape, hshape, vshape,
                   jax.ShapeDtypeStruct(k_stack.shape, F32), jax.ShapeDtypeStruct(v_stack.shape, F32)],
        input_output_aliases={6: 3, 7: 4},
        compiler_params=_params(("parallel", "parallel")),
        name="rope_heads",
    )(main, main, main, *tabs, k_stack, v_stack)


def _hgrn_kernel(q_ref, f_ref, i_ref, g_ref, s0_ref, lb_ref, gn_ref, o_ref, so_ref, st_ref,
                 *, tt, chunk, heads, valid):
    t = pl.program_id(2)
    C = chunk
    levels = [n for n in (64, 32, 16) if n <= C]

    @pl.when(t == 0)
    def _():
        for h in range(heads):
            st_ref[h] = s0_ref[h].T

    row = lax.broadcasted_iota(jnp.int32, (C, LANES), 0)
    rr = lax.broadcasted_iota(jnp.int32, (C, C), 0)
    cc = lax.broadcasted_iota(jnp.int32, (C, C), 1)
    lvl_masks = []
    for n in levels:
        sh = int(math.log2(n))
        same = (rr >> sh) == (cc >> sh)
        up = (rr & (n - 1)) >= n // 2
        lo = (cc & (n - 1)) < n // 2
        lvl_masks.append(jnp.where(same, jnp.where(up, jnp.where(lo, 1.0, 0.0), 0.0), 0.0))

    def chunk_body(c, carry):
        r0 = pl.multiple_of(c * C, C)
        rows = pl.ds(r0, C)
        for h in range(heads):
            cols = slice(h * HEAD, (h + 1) * HEAD)
            q = q_ref[rows, cols]
            z = f_ref[rows, cols]
            v = i_ref[rows, cols]
            gate = g_ref[rows, cols]
            lb = lb_ref[:, cols]
            lbm = jnp.maximum(lb, LB_FLOOR)
            oml = 1.0 - lb
            sg, sgn = _sigmoid_pair(z)
            f = lbm + oml * sg
            kin = oml * sgn - (lbm - lb)
            if valid is not None:
                ok = (r0 + row) < valid
                f = jnp.where(ok, f, 1.0)
                kin = jnp.where(ok, kin, 0.0)
            lf = jnp.log(f)
            b = lf
            s = 1
            while s < C:
                b = b + jnp.where(row >= s, pltpu.roll(b, s, axis=0), 0.0)
                s *= 2
            St = st_ref[h]
            qd = q * jnp.exp(b)
            o = lax.dot_general(qd.astype(BF16), St.astype(BF16), (((1,), (1,)), ((), ())),
                                preferred_element_type=F32)
            if levels:
                att = jnp.zeros((C, C), F32)
                for n, msk in zip(levels, lvl_masks):
                    pieces = []
                    for m in range(C // n):
                        rb = m * n + n // 2 - 1
                        pieces.append(jnp.broadcast_to(b[rb:rb + 1, :], (n, LANES)))
                    R = pieces[0] if len(pieces) == 1 else jnp.concatenate(pieces, axis=0)
                    qn = q * jnp.exp(jnp.minimum(b - R, 0.0))
                    kn = kin * jnp.exp(jnp.minimum(R - b, 0.0))
                    a = lax.dot_general(qn.astype(BF16), kn.astype(BF16), (((1,), (1,)), ((), ())),
                                        preferred_element_type=F32)
                    att = att + a * msk
                o = o + jnp.dot(att.astype(BF16), v.astype(BF16), preferred_element_type=F32)
            def back(x, j):
                return pltpu.roll(x.reshape(C // HG_DIAG, HG_DIAG, LANES), j, axis=1).reshape(C, LANES)

            rin = row & (HG_DIAG - 1)
            o = o + jnp.sum(q * kin, axis=-1, keepdims=True) * v
            e = f
            for j in range(1, HG_DIAG):
                if j > 1:
                    e = e * back(f, j - 1)
                kj = jnp.where(rin >= j, back(kin, j), 0.0)
                cj = jnp.sum(q * kj * e, axis=-1, keepdims=True)
                o = o + cj * back(v, j)
            bl = b[C - 1:C, :]
            kd = kin * jnp.exp(bl - b)
            upd = jnp.dot(v.T.astype(BF16), kd.astype(BF16), preferred_element_type=F32)
            st_ref[h] = St * jnp.exp(bl) + upd
            sgg, _ = _sigmoid_pair(gate)
            o_ref[rows, cols] = (_rms(o, gn_ref[:, cols]) * (gate * sgg)).astype(BF16)
        return carry

    lax.fori_loop(0, tt // C, chunk_body, 0)

    @pl.when(t == pl.num_programs(2) - 1)
    def _():
        for h in range(heads):
            so_ref[h] = st_ref[h].T


def hgrn2(main, s0, lb, gn, chunk, valid):
    B, T, _ = main.shape
    heads = s0.shape[1]
    hp = math.gcd(HG_HEADS_PER_STEP, heads)
    G = heads // hp
    W = heads * HEAD
    tt = min(T, 256)
    blk = lambda c: pl.BlockSpec((None, tt, hp * HEAD), lambda b, g, t: (b, t, c * G + g))
    vec = pl.BlockSpec((1, hp * HEAD), lambda b, g, t: (0, g))
    sblk = pl.BlockSpec((None, hp, HEAD, HEAD), lambda b, g, t: (b, g, 0, 0))
    return pl.pallas_call(
        functools.partial(_hgrn_kernel, tt=tt, chunk=chunk, heads=hp, valid=valid),
        grid=(B, G, T // tt),
        in_specs=[blk(0), blk(1), blk(2), blk(3), sblk, vec, vec],
        out_specs=[blk(0), sblk],
        out_shape=[jax.ShapeDtypeStruct((B, T, W), BF16),
                   jax.ShapeDtypeStruct(s0.shape, F32)],
        scratch_shapes=[pltpu.VMEM((hp, HEAD, HEAD), F32)],
        compiler_params=_params(("parallel", "parallel", "arbitrary")),
        name="hgrn2",
    )(main, main, main, main, s0, lb, gn)


def _lambda(lq1, lk1, lq2, lk2, lam_init):
    a = jnp.exp(jnp.sum(lq1 * lk1, axis=-1, keepdims=True))
    b = jnp.exp(jnp.sum(lq2 * lk2, axis=-1, keepdims=True))
    return a - b + lam_init


def _pattn_kernel(qi_ref, kj_ref, q_ref, kc_ref, kn_ref, v_ref, lq1, lk1, lq2, lk2, sub_ref, o_ref,
                  qst, m_ref, acc_ref, s_even, s_odd, *, tq, rs, lam_init):
    n = pl.program_id(2)
    qi = qi_ref[n]
    kj = kj_ref[n]

    def scores(k_ref):
        return lax.dot_general(qst[...], k_ref[...], (((1,), (1,)), ((), ())), preferred_element_type=F32)

    @pl.when(kj == 0)
    def _():
        q = q_ref[...]
        lane = lax.broadcasted_iota(jnp.int32, q.shape, 1)
        zero = jnp.zeros_like(q)
        qst[0:tq, :] = jnp.where(lane < QK_HALF, q, zero)
        qst[tq:2 * tq, :] = jnp.where(lane >= QK_HALF, q, zero)
        m_ref[...] = jnp.full(m_ref.shape, MASK_VALUE, F32)
        acc_ref[...] = jnp.zeros(acc_ref.shape, F32)
        s_even[...] = scores(kc_ref)

    def consume(s_ref, diag):
        for c in range(2 * tq // rs):
            rows = slice(c * rs, (c + 1) * rs)
            q0 = (c * rs) % tq
            kw = q0 + rs if diag else tq
            s = s_ref[rows, 0:kw]
            if diag:
                rq = lax.broadcasted_iota(jnp.int32, s.shape, 0) + q0
                ck = lax.broadcasted_iota(jnp.int32, s.shape, 1)
                s = jnp.where(ck <= rq, s, MASK_VALUE)
            m_prev = m_ref[rows, :]
            m_next = jnp.maximum(m_prev, jnp.max(s, axis=1, keepdims=True))
            p = jnp.exp2(s - m_next[:, 0:1])
            alpha = jnp.exp2(m_prev - m_next)
            pv = jnp.dot(p.astype(BF16), v_ref[0:kw, :], preferred_element_type=F32)
            acc_ref[rows, 0:HEAD] = alpha * acc_ref[rows, 0:HEAD] + pv[:, 0:HEAD]
            acc_ref[rows, HEAD:2 * HEAD] = alpha * acc_ref[rows, HEAD:2 * HEAD] + pv[:, HEAD:2 * HEAD]
            m_ref[rows, :] = m_next

    for parity, cur, nxt in ((0, s_even, s_odd), (1, s_odd, s_even)):
        mine = (kj & 1) == parity

        @pl.when(jnp.logical_and(kj < qi, mine))
        def _():
            nxt[...] = scores(kn_ref)
            consume(cur, False)

        @pl.when(jnp.logical_and(kj == qi, mine))
        def _():
            consume(cur, True)
            lam = _lambda(lq1[...], lk1[...], lq2[...], lk2[...], lam_init)
            o1 = acc_ref[0:tq, 0:HEAD] / acc_ref[0:tq, HEAD:2 * HEAD]
            o2 = acc_ref[tq:2 * tq, 0:HEAD] / acc_ref[tq:2 * tq, HEAD:2 * HEAD]
            o = o1 - lam * o2
            o_ref[...] = (_rms(o, sub_ref[...]) * (1.0 - lam_init)).astype(BF16)


def prompt_attention(qs, kb, vb, lams, subln, lam_init):
    B, heads, T, _ = qs.shape
    tq = min(T, 1024)
    rs = min(tq, 256)
    nq = T // tq
    pairs = [(i, j) for i in range(nq) for j in range(i + 1)]
    qi = jnp.asarray(np.array([p[0] for p in pairs], np.int32))
    kj = jnp.asarray(np.array([p[1] for p in pairs], np.int32))
    lam_spec = pl.BlockSpec((1, QK_HALF), lambda b, h, n, qi, kj: (0, 0))
    npairs = len(pairs)
    kc_spec = pl.BlockSpec((None, None, tq, HEAD), lambda b, h, n, qi, kj: (b, h, kj[n], 0))
    kn_spec = pl.BlockSpec((None, None, tq, HEAD),
                           lambda b, h, n, qi, kj: (b, h, kj[jnp.minimum(n + 1, npairs - 1)], 0))
    grid_spec = pltpu.PrefetchScalarGridSpec(
        num_scalar_prefetch=2,
        grid=(B, heads, len(pairs)),
        in_specs=[pl.BlockSpec((None, None, tq, HEAD), lambda b, h, n, qi, kj: (b, h, qi[n], 0)),
                  kc_spec, kn_spec,
                  pl.BlockSpec((None, None, tq, 2 * HEAD), lambda b, h, n, qi, kj: (b, h, kj[n], 0)),
                  lam_spec, lam_spec, lam_spec, lam_spec,
                  pl.BlockSpec((1, HEAD), lambda b, h, n, qi, kj: (0, h))],
        out_specs=pl.BlockSpec((None, tq, HEAD), lambda b, h, n, qi, kj: (b, qi[n], h)),
        scratch_shapes=[pltpu.VMEM((2 * tq, HEAD), BF16),
                        pltpu.VMEM((2 * tq, HEAD), F32),
                        pltpu.VMEM((2 * tq, 2 * HEAD), F32),
                        pltpu.VMEM((2 * tq, tq), F32),
                        pltpu.VMEM((2 * tq, tq), F32)],
    )
    return pl.pallas_call(
        functools.partial(_pattn_kernel, tq=tq, rs=rs, lam_init=lam_init),
        grid_spec=grid_spec,
        out_shape=jax.ShapeDtypeStruct((B, T, heads * HEAD), BF16),
        compiler_params=_params(("parallel", "parallel", "arbitrary")),
        name="prompt_attention",
    )(qi, kj, qs, kb, kb, vb, *lams, subln)


def _sattn_kernel(pt_ref, q_ref, kn_ref, vn_ref, lq1, lk1, lq2, lk2, sub_ref, *rest,
                  heads, tnew, lam_init, pps):
    k_refs = rest[:pps]
    v_refs = rest[pps:2 * pps]
    o_ref = rest[2 * pps]
    qbd, m_ref, l_ref, acc_ref = rest[2 * pps + 1:]
    j = pl.program_id(1)
    R = heads * 2 * SUBLANES
    W = heads * HEAD

    @pl.when(j == 0)
    def _():
        q = q_ref[...]
        qrep = jnp.concatenate([q] * (heads * 2), axis=0)
        lane = lax.broadcasted_iota(jnp.int32, (R, W), 1)
        rowi = lax.broadcasted_iota(jnp.int32, (R, W), 0)
        qbd[...] = jnp.where((lane >> 6) == (rowi >> 3), qrep, 0.0)
        m_ref[...] = jnp.full(m_ref.shape, MASK_VALUE, F32)
        l_ref[...] = jnp.zeros(l_ref.shape, F32)
        acc_ref[...] = jnp.zeros(acc_ref.shape, F32)

    def page_rows(ref):
        return jnp.concatenate([ref[h] for h in range(heads)], axis=1).astype(BF16)

    qb = qbd[...].astype(BF16)
    ss = []
    for p in range(pps):
        ss.append(lax.dot_general(qb, page_rows(k_refs[p]), (((1,), (1,)), ((), ())),
                                  preferred_element_type=F32))
    m_prev = m_ref[...]
    m_cur = ss[0].max(axis=1, keepdims=True)
    for p in range(1, pps):
        m_cur = jnp.maximum(m_cur, ss[p].max(axis=1, keepdims=True))
    m_next = jnp.maximum(m_prev, m_cur)
    alpha = jnp.exp(m_prev - m_next)
    l_new = alpha * l_ref[...]
    acc = alpha[:, 0:1] * acc_ref[...]
    for p in range(pps):
        pr = jnp.exp(ss[p] - m_next[:, 0:1])
        l_new = l_new + jnp.sum(pr, axis=1, keepdims=True)
        acc = acc + jnp.dot(pr.astype(BF16), page_rows(v_refs[p]), preferred_element_type=F32)
    m_ref[...] = m_next
    l_ref[...] = l_new
    acc_ref[...] = acc

    @pl.when(j == pl.num_programs(1) - 1)
    def _():
        qf = qbd[...]
        tok = lax.broadcasted_iota(jnp.int32, (R, LANES), 0) & (SUBLANES - 1)
        m_p = m_ref[...]
        sn = []
        for jn in range(tnew):
            sj = jnp.sum(qf * kn_ref[jn:jn + 1, :], axis=1, keepdims=True)
            sn.append(jnp.where(tok >= jn, sj, MASK_VALUE))
        m_n = m_p
        for sj in sn:
            m_n = jnp.maximum(m_n, sj)
        al = jnp.exp(m_p - m_n)
        l_f = al * l_ref[...]
        acc_f = al[:, 0:1] * acc_ref[...]
        for jn in range(tnew):
            pj = jnp.exp(sn[jn] - m_n)
            l_f = l_f + pj
            acc_f = acc_f + pj[:, 0:1] * vn_ref[jn:jn + 1, :]
        lam = _lambda(lq1[...], lk1[...], lq2[...], lk2[...], lam_init)
        for h in range(heads):
            cols = slice(h * HEAD, (h + 1) * HEAD)
            r1 = slice(h * 2 * SUBLANES, h * 2 * SUBLANES + SUBLANES)
            r2 = slice(h * 2 * SUBLANES + SUBLANES, (h + 1) * 2 * SUBLANES)
            o1 = acc_f[r1, cols] / l_f[r1, :]
            o2 = acc_f[r2, cols] / l_f[r2, :]
            o = o1 - lam * o2
            o_ref[:, cols] = (_rms(o, sub_ref[:, cols]) * (1.0 - lam_init)).astype(BF16)


def sample_attention(qs8, kn8, vn8, cache_k, cache_v, layer, page_table, lams, subln, lam_init, tnew):
    DB, _, W = qs8.shape
    heads = W // HEAD
    page = cache_k.shape[3]
    n_pages = page_table.shape[1]
    pps = math.gcd(PAGES_PER_STEP, n_pages)
    R = heads * 2 * SUBLANES
    tok_spec = pl.BlockSpec((None, SUBLANES, W), lambda b, j, pt: (b, 0, 0))
    lam_spec = pl.BlockSpec((1, QK_HALF), lambda b, j, pt: (0, 0))

    def page_spec(p):
        return pl.BlockSpec((None, None, heads, page, HEAD),
                            lambda b, j, pt: (layer, pt[b, j * pps + p], 0, 0, 0))

    grid_spec = pltpu.PrefetchScalarGridSpec(
        num_scalar_prefetch=1,
        grid=(DB, n_pages // pps),
        in_specs=[tok_spec, tok_spec, tok_spec, lam_spec, lam_spec, lam_spec, lam_spec,
                  pl.BlockSpec((1, W), lambda b, j, pt: (0, 0))]
                 + [page_spec(p) for p in range(pps)] + [page_spec(p) for p in range(pps)],
        out_specs=tok_spec,
        scratch_shapes=[pltpu.VMEM((R, W), F32),
                        pltpu.VMEM((R, LANES), F32),
                        pltpu.VMEM((R, LANES), F32),
                        pltpu.VMEM((R, W), F32)],
    )
    return pl.pallas_call(
        functools.partial(_sattn_kernel, heads=heads, tnew=tnew, lam_init=lam_init, pps=pps),
        grid_spec=grid_spec,
        out_shape=jax.ShapeDtypeStruct((DB, SUBLANES, W), BF16),
        compiler_params=_params(("parallel", "arbitrary")),
        name="sample_attention",
    )(page_table, qs8, kn8, vn8, *lams, subln, *([cache_k] * pps), *([cache_v] * pps))


def _oproj_kernel(a_ref, b_ref, c_ref, w_ref, x_ref, gpost_ref, gnext_ref, xo_ref, h_ref, *, wa, wb):
    tm = x_ref.shape[0]
    rg = min(tm, OPROJ_ROWS)
    ys = []
    for r in range(tm // rg):
        rows = slice(r * rg, (r + 1) * rg)
        y = jnp.dot(a_ref[rows, :], w_ref[0:wa, :], preferred_element_type=F32)
        y = y + jnp.dot(b_ref[rows, :], w_ref[wa:wa + wb, :], preferred_element_type=F32)
        ys.append(y + jnp.dot(c_ref[rows, :], w_ref[wa + wb:, :], preferred_element_type=F32))
    for r in range(tm // rg):
        rows = slice(r * rg, (r + 1) * rg)
        x = x_ref[rows, :] + _rms(ys[r], gpost_ref[...])
        xo_ref[rows, :] = x
        h_ref[rows, :] = _rms(x, gnext_ref[...]).astype(BF16)


def out_proj(a, b, c, w, layer, x, gpost, gnext, tm):
    M, D = x.shape
    wa, wb, wc = a.shape[1], b.shape[1], c.shape[1]
    row = lambda width: pl.BlockSpec((tm, width), lambda i: (i, 0))
    vec = pl.BlockSpec((1, D), lambda i: (0, 0))
    return pl.pallas_call(
        functools.partial(_oproj_kernel, wa=wa, wb=wb),
        grid=(M // tm,),
        in_specs=[row(wa), row(wb), row(wc), pl.BlockSpec((None, D, D), lambda i: (layer, 0, 0)), row(D), vec, vec],
        out_specs=[row(D), row(D)],
        out_shape=[jax.ShapeDtypeStruct((M, D), F32), jax.ShapeDtypeStruct((M, D), BF16)],
        compiler_params=_params(("parallel",)),
        name="out_proj",
    )(a, b, c, w, x, gpost, gnext)


def _ffn_up_seq_kernel(h_ref, wg_ref, wu_ref, cw_ref, cb_ref, prev_ref, act_ref, st_ref,
                       wgb, wub, buf, *, tm, groups):
    b = pl.program_id(1)
    t = pl.program_id(2)
    H = SUBLANES
    tn = buf.shape[1]
    gw = tn // groups

    @pl.when(t == 0)
    def _():
        @pl.when(b == 0)
        def _():
            wgb[...] = wg_ref[...].astype(BF16)
            wub[...] = wu_ref[...].astype(BF16)

        buf[0:H - 2, :] = jnp.zeros((H - 2, tn), F32)
        buf[H - 2:H, :] = prev_ref[...]

    rg = min(tm, FFN_UP_ROWS)
    blocks = [(r, g) for r in range(tm // rg) for g in range(groups)]
    prods = []
    for r, g in blocks:
        cols = slice(g * gw, (g + 1) * gw)
        h = h_ref[r * rg:(r + 1) * rg, :]
        prods.append((jnp.dot(h, wgb[:, cols], preferred_element_type=F32),
                      jnp.dot(h, wub[:, cols], preferred_element_type=F32)))
    for (r, g), (u, up) in zip(blocks, prods):
        cols = slice(g * gw, (g + 1) * gw)
        r0 = H + r * rg
        buf[r0:r0 + rg, cols] = u
        c = cb_ref[:, cols] + cw_ref[2:3, cols] * u
        c = c + cw_ref[1:2, cols] * buf[r0 - 1:r0 - 1 + rg, cols]
        c = c + cw_ref[0:1, cols] * buf[r0 - 2:r0 - 2 + rg, cols]
        act_ref[r * rg:(r + 1) * rg, cols] = (c * (1.0 / (1.0 + jnp.exp(-c))) * up).astype(BF16)
    tail = buf[tm:tm + H, :]
    st_ref[...] = tail[H - 2:, :]
    buf[0:H, :] = tail


def ffn_up_seq(h3, wg, wu, layer, cw, cb, prev, tn):
    B, T, D = h3.shape
    Fd = wg.shape[2]
    tm = min(T, 1024)
    groups = max(1, tn // MXU_COLS)
    wspec = pl.BlockSpec((None, D, tn), lambda j, b, t: (layer, 0, j))
    return pl.pallas_call(
        functools.partial(_ffn_up_seq_kernel, tm=tm, groups=groups),
        grid=(Fd // tn, B, T // tm),
        in_specs=[pl.BlockSpec((None, tm, D), lambda j, b, t: (b, t, 0)), wspec, wspec,
                  pl.BlockSpec((CONV_W, tn), lambda j, b, t: (0, j)),
                  pl.BlockSpec((1, tn), lambda j, b, t: (0, j)),
                  pl.BlockSpec((None, CONV_W - 1, tn), lambda j, b, t: (b, 0, j))],
        out_specs=[pl.BlockSpec((None, tm, tn), lambda j, b, t: (b, t, j)),
                   pl.BlockSpec((None, CONV_W - 1, tn), lambda j, b, t: (b, 0, j))],
        out_shape=[jax.ShapeDtypeStruct((B, T, Fd), BF16),
                   jax.ShapeDtypeStruct((B, CONV_W - 1, Fd), F32)],
        scratch_shapes=[pltpu.VMEM((D, tn), BF16), pltpu.VMEM((D, tn), BF16),
                        pltpu.VMEM((SUBLANES + tm, tn), F32)],
        compiler_params=_params(("arbitrary", "arbitrary", "arbitrary")),
        name="ffn_up_prompt",
    )(h3, wg, wu, cw, cb, prev)


def _ffn_up_short_kernel(h_ref, wg_ref, wu_ref, cw_ref, cb_ref, p1_ref, p2_ref, act_ref, u_ref, *, T):
    h = h_ref[...]
    u = jnp.dot(h, wg_ref[...].astype(BF16), preferred_element_type=F32)
    up = jnp.dot(h, wu_ref[...].astype(BF16), preferred_element_type=F32)
    u_ref[...] = u
    r = lax.broadcasted_iota(jnp.int32, u.shape, 0) & (T - 1)
    um1 = jnp.where(r >= 1, pltpu.roll(u, 1, axis=0), p1_ref[...])
    um2 = jnp.where(r >= 2, pltpu.roll(u, 2, axis=0), p2_ref[...])
    c = cb_ref[...] + cw_ref[2:3, :] * u + cw_ref[1:2, :] * um1 + cw_ref[0:1, :] * um2
    sg, _ = _sigmoid_pair(c)
    act_ref[...] = (c * sg * up).astype(BF16)


def ffn_up_short(h, wg, wu, layer, cw, cb, p1, p2, T, tn):
    M, D = h.shape
    Fd = wg.shape[2]
    wspec = pl.BlockSpec((None, D, tn), lambda j: (layer, 0, j))
    col = pl.BlockSpec((M, tn), lambda j: (0, j))
    return pl.pallas_call(
        functools.partial(_ffn_up_short_kernel, T=T),
        grid=(Fd // tn,),
        in_specs=[pl.BlockSpec((M, D), lambda j: (0, 0)), wspec, wspec,
                  pl.BlockSpec((CONV_W, tn), lambda j: (0, j)),
                  pl.BlockSpec((1, tn), lambda j: (0, j)), col, col],
        out_specs=[col, col],
        out_shape=[jax.ShapeDtypeStruct((M, Fd), BF16), jax.ShapeDtypeStruct((M, Fd), F32)],
        compiler_params=_params(("parallel",)),
        name="ffn_up_sample",
    )(h, wg, wu, cw, cb, p1, p2)


def _ffn_down_kernel(a_ref, w_ref, x_ref, gpost_ref, gnext_ref, xo_ref, h_ref):
    y = jnp.dot(a_ref[...], w_ref[...], preferred_element_type=F32)
    x = x_ref[...] + _rms(y, gpost_ref[...])
    xo_ref[...] = x
    h_ref[...] = _rms(x, gnext_ref[...]).astype(BF16)


def ffn_down(act, w, layer, x, gpost, gnext, tm):
    M, D = x.shape
    Fd = act.shape[1]
    row = pl.BlockSpec((tm, D), lambda i: (i, 0))
    vec = pl.BlockSpec((1, D), lambda i: (0, 0))
    return pl.pallas_call(
        _ffn_down_kernel,
        grid=(M // tm,),
        in_specs=[pl.BlockSpec((tm, Fd), lambda i: (i, 0)),
                  pl.BlockSpec((None, Fd, D), lambda i: (layer, 0, 0), pipeline_mode=pl.Buffered(1)),
                  row, vec, vec],
        out_specs=[row, row],
        out_shape=[jax.ShapeDtypeStruct((M, D), F32), jax.ShapeDtypeStruct((M, D), BF16)],
        compiler_params=_params(("arbitrary",)),
        name="ffn_down",
    )(act, w, x, gpost, gnext)


def _largest_tile(n, cap, quantum):
    best = quantum
    for t in range(quantum, cap + 1, quantum):
        if n % t == 0:
            best = t
    return best


def _layer(x, h, l, lw, *, seqs, T, pos0, prev_pool, prev_S, prev_conv, paged, kv_stacks=None):
    M, D = x.shape
    B = seqs
    tm = min(M, 1024)
    tm2 = min(M, 2 * OPROJ_ROWS)
    WA = lw['pool_scale'].shape[1]
    WH = lw['hgrn_norm'].shape[1]
    WC = lw['attn_subln'].shape[1]
    lam_init = 0.8 - 0.6 * math.exp(-0.3 * l)
    lams = (lw['lam_q1'], lw['lam_k1'], lw['lam_q2'], lw['lam_k2'])

    WM = lw['w_main'].shape[2]
    xa = matmul(h, lw['w_pool'], l, 0, WA, tm, WA, "proj_pool").reshape(B, T, WA)
    main = matmul(h, lw['w_main'], l, 0, WM, tm, WH, "proj_main").reshape(B, T, WM)

    a_out, pool_new = pool_mixer(xa, prev_pool, lw['pool_w'], lw['pool_scale'], pos0)

    if paged is None:
        o_b, S_new = hgrn2(main, prev_S, lw['hgrn_lb'], lw['hgrn_norm'], min(HG_CHUNK, T), None)
        qs, kb, vb, k_rot, v = rope_heads(main, 4, WC, lw['rope_tabs'], pos0, *kv_stacks, l)
        o_c = prompt_attention(qs, kb, vb, lams, lw['attn_subln'], lam_init)
    else:
        pad = ((0, 0), (0, SUBLANES - T), (0, 0))
        main8 = jnp.pad(main, pad)
        o_b8, S_new = hgrn2(main8, prev_S, lw['hgrn_lb'], lw['hgrn_norm'], SUBLANES, T)
        o_b = o_b8[:, :T]
        qs8, k_rot8, _ = rope(main8, 4, 5, WC, lw['rope_tabs'], pos0)
        v = main[:, :, 6 * WC:]
        cache_k, cache_v, page_table = paged
        o_c8 = sample_attention(qs8.astype(F32), k_rot8, jnp.pad(v, pad), cache_k, cache_v, l,
                                page_table, lams, lw['attn_subln'], lam_init, T)
        o_c = o_c8[:, :T]
        k_rot = k_rot8[:, :T]

    x, h2 = out_proj(a_out.reshape(M, WA), o_b.reshape(M, WH), o_c.reshape(M, WC), lw['w_out'], l,
                     x, lw['norm_mix_post'], lw['norm_ffn_pre'], tm2)

    Fd = lw['w_gate'].shape[2]
    tn = _largest_tile(Fd, 512, LANES)
    if paged is None:
        act, conv_new = ffn_up_seq(h2.reshape(B, T, D), lw['w_gate'], lw['w_up'], l, lw['conv_w'],
                                   lw['conv_b'], prev_conv, tn)
        act = act.reshape(M, Fd)
    else:
        z = jnp.zeros((B, T, Fd), F32)
        p1 = z.at[:, 0].set(prev_conv[:, 1]).reshape(M, Fd)
        p2 = z.at[:, 0].set(prev_conv[:, 0]).at[:, 1].set(prev_conv[:, 1]).reshape(M, Fd)
        act, u = ffn_up_short(h2, lw['w_gate'], lw['w_up'], l, lw['conv_w'], lw['conv_b'], p1, p2, T, tn)
        conv_new = u.reshape(B, T, Fd)[:, T - (CONV_W - 1):]
    x, h_next = ffn_down(act, lw['w_down'], l, x, lw['norm_ffn_post'], lw['norm_next'], min(M, FFN_DOWN_ROWS))
    return x, h_next, (k_rot, v, pool_new, S_new, conv_new)


def kernel(x_prompt, x_sample, cache_k, cache_v, state_pool, state_hgrn, state_conv, page_table, w_in, w_out, norm_mix_pre, norm_mix_post, norm_ffn_pre, norm_ffn_post, pool_w, pool_scale, hgrn_lower_bounds, hgrn_norm, attn_lam_q1, attn_lam_k1, attn_lam_q2, attn_lam_k2, attn_subln, ffn_w_gate, ffn_w_up, ffn_conv_w, ffn_conv_b, ffn_w_down):
    depth = w_in.shape[0]
    B, T, D = x_prompt.shape
    DB, TS, _ = x_sample.shape
    page = cache_k.shape[2]
    P = page_table.shape[1] * page
    WA = pool_scale.shape[1]
    WH = hgrn_norm.shape[1]
    WC = attn_subln.shape[1]
    HH = WH // HEAD
    Fd = ffn_w_gate.shape[2]
    assert TS >= CONV_W - 1 and TS <= SUBLANES and (TS & (TS - 1)) == 0

    sm = jax.nn.softmax(hgrn_lower_bounds.astype(F32), axis=0)
    lbs = jnp.cumsum(sm, axis=0) - sm[0]

    half = ROT_DIM // 2
    freqs = jnp.power(ROPE_THETA, -jnp.arange(0, ROT_DIM, 2, dtype=F32) / ROT_DIM)
    lane = np.arange(LANES)
    in_rot = (lane % QK_HALF) < ROT_DIM
    fl = jnp.where(jnp.asarray(in_rot), jnp.tile(freqs, LANES // half), 0.0).reshape(1, LANES)
    s1 = jnp.asarray(np.where(in_rot & ((lane % QK_HALF) < half), -1.0, 0.0), F32).reshape(1, LANES)
    s2 = jnp.asarray(np.where(in_rot & ((lane % QK_HALF) >= half), 1.0, 0.0), F32).reshape(1, LANES)

    ones = jnp.ones((1, D), F32)
    HC = WC // HEAD
    ck = jnp.swapaxes(cache_k, 2, 3)
    cv = jnp.swapaxes(cache_v, 2, 3)
    kv = (jnp.zeros((depth, B, HC, T, HEAD), F32), jnp.zeros((depth, B, HC, T, HEAD), F32))

    w_pool_b = w_in[:, :, :WA].astype(BF16)
    w_main_b = w_in[:, :, WA:].astype(BF16)
    w_out_b = w_out.astype(BF16)
    w_down_b = ffn_w_down.astype(BF16)

    def layer_weights(l):
        return {
            'w_pool': w_pool_b,
            'w_main': w_main_b,
            'w_out': w_out_b,
            'w_gate': ffn_w_gate,
            'w_up': ffn_w_up,
            'w_down': w_down_b,
            'pool_w': pool_w[l],
            'pool_scale': pool_scale[l].reshape(1, WA),
            'hgrn_lb': lbs[l].reshape(1, WH),
            'hgrn_norm': hgrn_norm[l].reshape(1, WH),
            'attn_subln': attn_subln[l].reshape(1, WC),
            'lam_q1': attn_lam_q1[l].reshape(1, QK_HALF),
            'lam_k1': attn_lam_k1[l].reshape(1, QK_HALF),
            'lam_q2': attn_lam_q2[l].reshape(1, QK_HALF),
            'lam_k2': attn_lam_k2[l].reshape(1, QK_HALF),
            'norm_mix_post': norm_mix_post[l].reshape(1, D),
            'norm_ffn_pre': norm_ffn_pre[l].reshape(1, D),
            'norm_ffn_post': norm_ffn_post[l].reshape(1, D),
            'norm_next': norm_mix_pre[l + 1].reshape(1, D) if l + 1 < depth else ones,
            'conv_w': ffn_conv_w[l],
            'conv_b': ffn_conv_b[l].reshape(1, Fd),
            'rope_tabs': (fl, s1, s2),
        }

    xp = x_prompt.reshape(B * T, D)
    xs = x_sample.reshape(DB * TS, D)
    g0 = norm_mix_pre[0].reshape(1, D)
    hp = rmsnorm_bf16(xp, g0, min(B * T, 512))
    hs = rmsnorm_bf16(xs, g0, DB * TS)
    zp_pool = jnp.zeros((B, POOL_STATE, WA), F32)
    zp_S = jnp.zeros((B, HH, HEAD, HEAD), F32)
    zp_conv = jnp.zeros((B, CONV_W - 1, Fd), F32)

    st_p, st_s = [], []
    for l in range(depth):
        lw = layer_weights(l)
        xp, hp, sp = _layer(xp, hp, l, lw, seqs=B, T=T, pos0=0, prev_pool=zp_pool, prev_S=zp_S,
                            prev_conv=zp_conv, paged=None, kv_stacks=kv)
        kv = sp[:2]
        xs, hs, ss = _layer(xs, hs, l, lw, seqs=DB, T=TS, pos0=P, prev_pool=state_pool[l],
                            prev_S=state_hgrn[l], prev_conv=state_conv[l],
                            paged=(ck, cv, page_table))
        st_p.append(sp)
        st_s.append(ss)

    def stack(sts, i, shape):
        return jnp.stack([s[i] for s in sts]).reshape(shape)

    return (xp.reshape(B, T, D), xs.reshape(DB, TS, D),
            jnp.swapaxes(kv[0], 2, 3), jnp.swapaxes(kv[1], 2, 3),
            stack(st_p, 2, (depth, B, POOL_STATE, WA)), stack(st_p, 3, (depth, B, HH, HEAD, HEAD)),
            stack(st_p, 4, (depth, B, CONV_W - 1, Fd)),
            stack(st_s, 0, (depth, DB, TS, HC, HEAD)), stack(st_s, 1, (depth, DB, TS, HC, HEAD)),
            stack(st_s, 2, (depth, DB, POOL_STATE, WA)), stack(st_s, 3, (depth, DB, HH, HEAD, HEAD)),
            stack(st_s, 4, (depth, DB, CONV_W - 1, Fd)))
```

```python
import functools
import math

import numpy as np
import jax
import jax.numpy as jnp
from jax import lax
from jax.experimental import pallas as pl
from jax.experimental.pallas import tpu as pltpu

F32 = jnp.float32
BF16 = jnp.bfloat16

LANES = 128
SUBLANES = 8
MXU_COLS = 256
VMEM_LIMIT = 48 * 1024 * 1024

POOL_WINDOWS = (2, 4, 8, 16)
POOL_STATE = max(POOL_WINDOWS) - 1
HEAD = 128
QK_HALF = 64
QK_SCALE_LOG2E = QK_HALF ** -0.5 * math.log2(math.e)
ROT_DIM = 16
ROPE_THETA = 500000.0
LB_FLOOR = 1e-30
MASK_VALUE = -1e30
CONV_W = 3
EPS = 1e-6
HG_CHUNK = 64
HG_DIAG = 8
HG_HEADS_PER_STEP = 6
PAGES_PER_STEP = 16
OPROJ_ROWS = 256
FFN_UP_ROWS = 512
FFN_DOWN_ROWS = 256


def _params(sem):
    return pltpu.CompilerParams(dimension_semantics=sem, vmem_limit_bytes=VMEM_LIMIT)


def _rms(x, g):
    return x * lax.rsqrt(jnp.mean(x * x, axis=-1, keepdims=True) + EPS) * g


def _sigmoid_pair(z):
    e = jnp.exp(-jnp.abs(z))
    r = 1.0 / (1.0 + e)
    er = e * r
    pos = z >= 0
    return jnp.where(pos, r, er), jnp.where(pos, er, r)


def _norm_kernel(x_ref, g_ref, h_ref):
    h_ref[...] = _rms(x_ref[...], g_ref[...]).astype(BF16)


def rmsnorm_bf16(x, g, tm):
    M, D = x.shape
    return pl.pallas_call(
        _norm_kernel,
        grid=(M // tm,),
        in_specs=[pl.BlockSpec((tm, D), lambda i: (i, 0)),
                  pl.BlockSpec((1, D), lambda i: (0, 0))],
        out_specs=pl.BlockSpec((tm, D), lambda i: (i, 0)),
        out_shape=jax.ShapeDtypeStruct((M, D), BF16),
        compiler_params=_params(("parallel",)),
        name="rmsnorm",
    )(x, g)


def _mm_kernel(x_ref, w_ref, o_ref):
    o_ref[...] = jnp.dot(x_ref[...], w_ref[...], preferred_element_type=F32)


def matmul(x, w, layer, col0, N, tm, tn, name):
    M, K = x.shape
    return pl.pallas_call(
        _mm_kernel,
        grid=(M // tm, N // tn),
        in_specs=[pl.BlockSpec((tm, K), lambda i, j: (i, 0)),
                  pl.BlockSpec((None, K, tn), lambda i, j: (layer, 0, col0 + j))],
        out_specs=pl.BlockSpec((tm, tn), lambda i, j: (i, j)),
        out_shape=jax.ShapeDtypeStruct((M, N), F32),
        compiler_params=_params(("parallel", "arbitrary")),
        name=name,
    )(x, w)


def _pool_kernel(xa_ref, prev_ref, wp_ref, sc_ref, y_ref, st_ref, buf, *, tt, pos0):
    t = pl.program_id(1)
    P = POOL_STATE + 1

    @pl.when(t == 0)
    def _():
        buf[0:1, :] = jnp.zeros((1, buf.shape[1]), F32)
        buf[1:P, :] = prev_ref[...]

    buf[P:P + tt, :] = xa_ref[...]
    pos = pos0 + t * tt + lax.broadcasted_iota(jnp.int32, (tt, LANES), 0)
    for g, w in enumerate(POOL_WINDOWS):
        cols = slice(g * LANES, (g + 1) * LANES)
        x = buf[P:P + tt, cols]
        acc = x
        for j in range(1, w):
            acc = acc + buf[P - j:P - j + tt, cols]
        cnt = jnp.minimum(w, pos + 1).astype(F32)
        d = acc / cnt - x
        y = jnp.dot(d.astype(BF16), wp_ref[g].astype(BF16), preferred_element_type=F32)
        y_ref[:, cols] = (y * sc_ref[:, cols]).astype(BF16)
    tail = buf[tt:tt + P, :]
    st_ref[...] = tail[1:, :]
    buf[0:P, :] = tail


def pool_mixer(xa, prev, wp, scale, pos0):
    B, T, W = xa.shape
    tt = min(T, 512)
    return pl.pallas_call(
        functools.partial(_pool_kernel, tt=tt, pos0=pos0),
        grid=(B, T // tt),
        in_specs=[pl.BlockSpec((None, tt, W), lambda b, t: (b, t, 0)),
                  pl.BlockSpec((None, POOL_STATE, W), lambda b, t: (b, 0, 0)),
                  pl.BlockSpec((len(POOL_WINDOWS), LANES, LANES), lambda b, t: (0, 0, 0)),
                  pl.BlockSpec((1, W), lambda b, t: (0, 0))],
        out_specs=[pl.BlockSpec((None, tt, W), lambda b, t: (b, t, 0)),
                   pl.BlockSpec((None, POOL_STATE, W), lambda b, t: (b, 0, 0))],
        out_shape=[jax.ShapeDtypeStruct((B, T, W), BF16),
                   jax.ShapeDtypeStruct((B, POOL_STATE, W), F32)],
        scratch_shapes=[pltpu.VMEM((POOL_STATE + 1 + tt, W), F32)],
        compiler_params=_params(("parallel", "arbitrary")),
        name="pool_mixer",
    )(xa, prev, wp, scale)


def _rope_kernel(q_ref, k_ref, fl_ref, s1_ref, s2_ref, qs_ref, kr_ref, kb_ref, *, tt, pos0, heads):
    t = pl.program_id(1)
    pos = (pos0 + t * tt + lax.broadcasted_iota(jnp.int32, (tt, LANES), 0)).astype(F32)
    ang = pos * fl_ref[...]
    c = jnp.cos(ang)
    s = jnp.sin(ang)
    sa = s * s1_ref[...]
    sb = s * s2_ref[...]
    half = ROT_DIM // 2
    for h in range(heads):
        cols = slice(h * HEAD, (h + 1) * HEAD)
        xq = q_ref[:, cols]
        yq = xq * c + pltpu.roll(xq, LANES - half, axis=1) * sa + pltpu.roll(xq, half, axis=1) * sb
        qs_ref[:, cols] = (yq * (QK_HALF ** -0.5)).astype(BF16)
        xk = k_ref[:, cols]
        yk = xk * c + pltpu.roll(xk, LANES - half, axis=1) * sa + pltpu.roll(xk, half, axis=1) * sb
        kr_ref[:, cols] = yk
        kb_ref[:, cols] = yk.astype(BF16)


def rope(main, qblk, kblk, width, tabs, pos0):
    B, T, _ = main.shape
    tt = min(T, 512)
    heads = width // HEAD
    tab = pl.BlockSpec((1, LANES), lambda b, t: (0, 0))
    blk = lambda c: pl.BlockSpec((None, tt, width), lambda b, t: (b, t, c))
    return pl.pallas_call(
        functools.partial(_rope_kernel, tt=tt, pos0=pos0, heads=heads),
        grid=(B, T // tt),
        in_specs=[blk(qblk), blk(kblk), tab, tab, tab],
        out_specs=[blk(0), blk(0), blk(0)],
        out_shape=[jax.ShapeDtypeStruct((B, T, width), BF16),
                   jax.ShapeDtypeStruct((B, T, width), F32),
                   jax.ShapeDtypeStruct((B, T, width), BF16)],
        compiler_params=_params(("parallel", "parallel")),
        name="rope",
    )(main, main, *tabs)


def _rope_heads_kernel(q_ref, k_ref, v_ref, fl_ref, s1_ref, s2_ref, kin_ref, vin_ref,
                       qs_ref, kb_ref, vb_ref, ko_ref, vo_ref, *, tt, pos0, heads):
    del kin_ref, vin_ref
    t = pl.program_id(1)
    pos = (pos0 + t * tt + lax.broadcasted_iota(jnp.int32, (tt, LANES), 0)).astype(F32)
    ang = pos * fl_ref[...]
    c = jnp.cos(ang)
    s = jnp.sin(ang)
    sa = s * s1_ref[...]
    sb = s * s2_ref[...]
    half = ROT_DIM // 2
    for h in range(heads):
        cols = slice(h * HEAD, (h + 1) * HEAD)
        xq = q_ref[:, cols]
        yq = xq * c + pltpu.roll(xq, LANES - half, axis=1) * sa + pltpu.roll(xq, half, axis=1) * sb
        qs_ref[h] = (yq * QK_SCALE_LOG2E).astype(BF16)
        xk = k_ref[:, cols]
        yk = xk * c + pltpu.roll(xk, LANES - half, axis=1) * sa + pltpu.roll(xk, half, axis=1) * sb
        ko_ref[h] = yk
        kb_ref[h] = yk.astype(BF16)
        xv = v_ref[:, cols]
        vo_ref[h] = xv
        vb_ref[h, :, 0:HEAD] = xv.astype(BF16)
        vb_ref[h, :, HEAD:2 * HEAD] = jnp.ones((tt, HEAD), BF16)


def rope_heads(main, qblk, width, tabs, pos0, k_stack, v_stack, layer):
    B, T, _ = main.shape
    tt = min(T, 512)
    heads = width // HEAD
    tab = pl.BlockSpec((1, LANES), lambda b, t: (0, 0))
    blk = lambda c: pl.BlockSpec((None, tt, width), lambda b, t: (b, t, c))
    hm = pl.BlockSpec((None, heads, tt, HEAD), lambda b, t: (b, 0, t, 0))
    st = pl.BlockSpec((None, None, heads, tt, HEAD), lambda b, t: (layer, b, 0, t, 0))
    anyspec = pl.BlockSpec(memory_space=pl.ANY)
    hshape = jax.ShapeDtypeStruct((B, heads, T, HEAD), BF16)
    vshape = jax.ShapeDtypeStruct((B, heads, T, 2 * HEAD), BF16)
    vm = pl.BlockSpec((None, heads, tt, 2 * HEAD), lambda b, t: (b, 0, t, 0))
    return pl.pallas_call(
        functools.partial(_rope_heads_kernel, tt=tt, pos0=pos0, heads=heads),
        grid=(B, T // tt),
        in_specs=[blk(qblk), blk(qblk + 1), blk(qblk + 2), tab, tab, tab, anyspec, anyspec],
        out_specs=[hm, hm, vm, st, st],
        out_shape=[hshape, hshape, vshape,
                   jax.ShapeDtypeStruct(k_stack.shape, F32), jax.ShapeDtypeStruct(v_stack.shape, F32)],
        input_output_aliases={6: 3, 7: 4},
        compiler_params=_params(("parallel", "parallel")),
        name="rope_heads",
    )(main, main, main, *tabs, k_stack, v_stack)


def _hgrn_kernel(q_ref, f_ref, i_ref, g_ref, s0_ref, lb_ref, gn_ref, o_ref, so_ref, st_ref,
                 *, tt, chunk, heads, valid, t=None, nt=None):
    if t is None:
        t, nt = pl.program_id(2), pl.num_programs(2)
    C = chunk
    levels = [n for n in (64, 32, 16) if n <= C]

    @pl.when(t == 0)
    def _():
        for h in range(heads):
            st_ref[h] = s0_ref[h].T

    row = lax.broadcasted_iota(jnp.int32, (C, LANES), 0)
    rr = lax.broadcasted_iota(jnp.int32, (C, C), 0)
    cc = lax.broadcasted_iota(jnp.int32, (C, C), 1)
    lvl_masks = []
    for n in levels:
        sh = int(math.log2(n))
        same = (rr >> sh) == (cc >> sh)
        up = (rr & (n - 1)) >= n // 2
        lo = (cc & (n - 1)) < n // 2
        lvl_masks.append(jnp.where(same, jnp.where(up, jnp.where(lo, 1.0, 0.0), 0.0), 0.0))

    def chunk_body(c, carry):
        r0 = pl.multiple_of(c * C, C)
        rows = pl.ds(r0, C)
        for h in range(heads):
            cols = slice(h * HEAD, (h + 1) * HEAD)
            q = q_ref[rows, cols]
            z = f_ref[rows, cols]
            v = i_ref[rows, cols]
            gate = g_ref[rows, cols]
            lb = lb_ref[:, cols]
            lbm = jnp.maximum(lb, LB_FLOOR)
            oml = 1.0 - lb
            sg, sgn = _sigmoid_pair(z)
            f = lbm + oml * sg
            kin = oml * sgn - (lbm - lb)
            if valid is not None:
                ok = (r0 + row) < valid
                f = jnp.where(ok, f, 1.0)
                kin = jnp.where(ok, kin, 0.0)
            lf = jnp.log(f)
            b = lf
            s = 1
            while s < C:
                b = b + jnp.where(row >= s, pltpu.roll(b, s, axis=0), 0.0)
                s *= 2
            St = st_ref[h]
            qd = q * jnp.exp(b)
            o = lax.dot_general(qd.astype(BF16), St.astype(BF16), (((1,), (1,)), ((), ())),
                                preferred_element_type=F32)
            if levels:
                att = jnp.zeros((C, C), F32)
                for n, msk in zip(levels, lvl_masks):
                    pieces = []
                    for m in range(C // n):
                        rb = m * n + n // 2 - 1
                        pieces.append(jnp.broadcast_to(b[rb:rb + 1, :], (n, LANES)))
                    R = pieces[0] if len(pieces) == 1 else jnp.concatenate(pieces, axis=0)
                    qn = q * jnp.exp(jnp.minimum(b - R, 0.0))
                    kn = kin * jnp.exp(jnp.minimum(R - b, 0.0))
                    a = lax.dot_general(qn.astype(BF16), kn.astype(BF16), (((1,), (1,)), ((), ())),
                                        preferred_element_type=F32)
                    att = att + a * msk
                o = o + jnp.dot(att.astype(BF16), v.astype(BF16), preferred_element_type=F32)
            def back(x, j):
                return pltpu.roll(x.reshape(C // HG_DIAG, HG_DIAG, LANES), j, axis=1).reshape(C, LANES)

            rin = row & (HG_DIAG - 1)
            o = o + jnp.sum(q * kin, axis=-1, keepdims=True) * v
            e = f
            for j in range(1, HG_DIAG):
                if j > 1:
                    e = e * back(f, j - 1)
                kj = jnp.where(rin >= j, back(kin, j), 0.0)
                cj = jnp.sum(q * kj * e, axis=-1, keepdims=True)
                o = o + cj * back(v, j)
            bl = b[C - 1:C, :]
            kd = kin * jnp.exp(bl - b)
            upd = jnp.dot(v.T.astype(BF16), kd.astype(BF16), preferred_element_type=F32)
            st_ref[h] = St * jnp.exp(bl) + upd
            sgg, _ = _sigmoid_pair(gate)
            o_ref[rows, cols] = (_rms(o, gn_ref[:, cols]) * (gate * sgg)).astype(BF16)
        return carry

    lax.fori_loop(0, tt // C, chunk_body, 0)

    @pl.when(t == nt - 1)
    def _():
        for h in range(heads):
            so_ref[h] = st_ref[h].T


def hgrn2(main, s0, lb, gn, chunk, valid):
    B, T, _ = main.shape
    heads = s0.shape[1]
    hp = math.gcd(HG_HEADS_PER_STEP, heads)
    G = heads // hp
    W = heads * HEAD
    tt = min(T, 256)
    blk = lambda c: pl.BlockSpec((None, tt, hp * HEAD), lambda b, g, t: (b, t, c * G + g))
    vec = pl.BlockSpec((1, hp * HEAD), lambda b, g, t: (0, g))
    sblk = pl.BlockSpec((None, hp, HEAD, HEAD), lambda b, g, t: (b, g, 0, 0))
    return pl.pallas_call(
        functools.partial(_hgrn_kernel, tt=tt, chunk=chunk, heads=hp, valid=valid),
        grid=(B, G, T // tt),
        in_specs=[blk(0), blk(1), blk(2), blk(3), sblk, vec, vec],
        out_specs=[blk(0), sblk],
        out_shape=[jax.ShapeDtypeStruct((B, T, W), BF16),
                   jax.ShapeDtypeStruct(s0.shape, F32)],
        scratch_shapes=[pltpu.VMEM((hp, HEAD, HEAD), F32)],
        compiler_params=_params(("parallel", "parallel", "arbitrary")),
        name="hgrn2",
    )(main, main, main, main, s0, lb, gn)


def _lambda(lq1, lk1, lq2, lk2, lam_init):
    a = jnp.exp(jnp.sum(lq1 * lk1, axis=-1, keepdims=True))
    b = jnp.exp(jnp.sum(lq2 * lk2, axis=-1, keepdims=True))
    return a - b + lam_init


def _pattn_kernel(qi_ref, kj_ref, q_ref, kc_ref, kn_ref, v_ref, lq1, lk1, lq2, lk2, sub_ref, o_ref,
                  qst, m_ref, acc_ref, s_even, s_odd, *, tq, rs, lam_init):
    n = pl.program_id(2)
    qi = qi_ref[n]
    kj = kj_ref[n]

    def scores(k_ref):
        return lax.dot_general(qst[...], k_ref[...], (((1,), (1,)), ((), ())), preferred_element_type=F32)

    @pl.when(kj == 0)
    def _():
        q = q_ref[...]
        lane = lax.broadcasted_iota(jnp.int32, q.shape, 1)
        zero = jnp.zeros_like(q)
        qst[0:tq, :] = jnp.where(lane < QK_HALF, q, zero)
        qst[tq:2 * tq, :] = jnp.where(lane >= QK_HALF, q, zero)
        m_ref[...] = jnp.full(m_ref.shape, MASK_VALUE, F32)
        acc_ref[...] = jnp.zeros(acc_ref.shape, F32)
        s_even[...] = scores(kc_ref)

    def consume(s_ref, diag):
        for c in range(2 * tq // rs):
            rows = slice(c * rs, (c + 1) * rs)
            q0 = (c * rs) % tq
            kw = q0 + rs if diag else tq
            s = s_ref[rows, 0:kw]
            if diag:
                rq = lax.broadcasted_iota(jnp.int32, s.shape, 0) + q0
                ck = lax.broadcasted_iota(jnp.int32, s.shape, 1)
                s = jnp.where(ck <= rq, s, MASK_VALUE)
            m_prev = m_ref[rows, :]
            m_next = jnp.maximum(m_prev, jnp.max(s, axis=1, keepdims=True))
            p = jnp.exp2(s - m_next[:, 0:1])
            alpha = jnp.exp2(m_prev - m_next)
            pv = jnp.dot(p.astype(BF16), v_ref[0:kw, :], preferred_element_type=F32)
            acc_ref[rows, 0:HEAD] = alpha * acc_ref[rows, 0:HEAD] + pv[:, 0:HEAD]
            acc_ref[rows, HEAD:2 * HEAD] = alpha * acc_ref[rows, HEAD:2 * HEAD] + pv[:, HEAD:2 * HEAD]
            m_ref[rows, :] = m_next

    for parity, cur, nxt in ((0, s_even, s_odd), (1, s_odd, s_even)):
        mine = (kj & 1) == parity

        @pl.when(jnp.logical_and(kj < qi, mine))
        def _():
            nxt[...] = scores(kn_ref)
            consume(cur, False)

        @pl.when(jnp.logical_and(kj == qi, mine))
        def _():
            consume(cur, True)
            lam = _lambda(lq1[...], lk1[...], lq2[...], lk2[...], lam_init)
            o1 = acc_ref[0:tq, 0:HEAD] / acc_ref[0:tq, HEAD:2 * HEAD]
            o2 = acc_ref[tq:2 * tq, 0:HEAD] / acc_ref[tq:2 * tq, HEAD:2 * HEAD]
            o = o1 - lam * o2
            o_ref[...] = (_rms(o, sub_ref[...]) * (1.0 - lam_init)).astype(BF16)


def prompt_attention(qs, kb, vb, lams, subln, lam_init):
    B, heads, T, _ = qs.shape
    tq = min(T, 1024)
    rs = min(tq, 256)
    nq = T // tq
    pairs = [(i, j) for i in range(nq) for j in range(i + 1)]
    qi = jnp.asarray(np.array([p[0] for p in pairs], np.int32))
    kj = jnp.asarray(np.array([p[1] for p in pairs], np.int32))
    lam_spec = pl.BlockSpec((1, QK_HALF), lambda b, h, n, qi, kj: (0, 0))
    npairs = len(pairs)
    kc_spec = pl.BlockSpec((None, None, tq, HEAD), lambda b, h, n, qi, kj: (b, h, kj[n], 0))
    kn_spec = pl.BlockSpec((None, None, tq, HEAD),
                           lambda b, h, n, qi, kj: (b, h, kj[jnp.minimum(n + 1, npairs - 1)], 0))
    grid_spec = pltpu.PrefetchScalarGridSpec(
        num_scalar_prefetch=2,
        grid=(B, heads, len(pairs)),
        in_specs=[pl.BlockSpec((None, None, tq, HEAD), lambda b, h, n, qi, kj: (b, h, qi[n], 0)),
                  kc_spec, kn_spec,
                  pl.BlockSpec((None, None, tq, 2 * HEAD), lambda b, h, n, qi, kj: (b, h, kj[n], 0)),
                  lam_spec, lam_spec, lam_spec, lam_spec,
                  pl.BlockSpec((1, HEAD), lambda b, h, n, qi, kj: (0, h))],
        out_specs=pl.BlockSpec((None, tq, HEAD), lambda b, h, n, qi, kj: (b, qi[n], h)),
        scratch_shapes=[pltpu.VMEM((2 * tq, HEAD), BF16),
                        pltpu.VMEM((2 * tq, HEAD), F32),
                        pltpu.VMEM((2 * tq, 2 * HEAD), F32),
                        pltpu.VMEM((2 * tq, tq), F32),
                        pltpu.VMEM((2 * tq, tq), F32)],
    )
    return pl.pallas_call(
        functools.partial(_pattn_kernel, tq=tq, rs=rs, lam_init=lam_init),
        grid_spec=grid_spec,
        out_shape=jax.ShapeDtypeStruct((B, T, heads * HEAD), BF16),
        compiler_params=_params(("parallel", "parallel", "arbitrary")),
        name="prompt_attention",
    )(qi, kj, qs, kb, kb, vb, *lams, subln)


def _sattn_kernel(pt_ref, q_ref, kn_ref, vn_ref, lq1, lk1, lq2, lk2, sub_ref, *rest,
                  heads, tnew, lam_init, pps, j=None, nj=None):
    k_refs = rest[:pps]
    v_refs = rest[pps:2 * pps]
    o_ref = rest[2 * pps]
    qbd, m_ref, l_ref, acc_ref = rest[2 * pps + 1:]
    if j is None:
        j, nj = pl.program_id(1), pl.num_programs(1)
    R = heads * 2 * SUBLANES
    W = heads * HEAD

    @pl.when(j == 0)
    def _():
        q = q_ref[...]
        qrep = jnp.concatenate([q] * (heads * 2), axis=0)
        lane = lax.broadcasted_iota(jnp.int32, (R, W), 1)
        rowi = lax.broadcasted_iota(jnp.int32, (R, W), 0)
        qbd[...] = jnp.where((lane >> 6) == (rowi >> 3), qrep, 0.0)
        m_ref[...] = jnp.full(m_ref.shape, MASK_VALUE, F32)
        l_ref[...] = jnp.zeros(l_ref.shape, F32)
        acc_ref[...] = jnp.zeros(acc_ref.shape, F32)

    def page_rows(ref):
        return jnp.concatenate([ref[h] for h in range(heads)], axis=1).astype(BF16)

    qb = qbd[...].astype(BF16)
    k_all = jnp.concatenate([page_rows(r) for r in k_refs], axis=0)
    s = lax.dot_general(qb, k_all, (((1,), (1,)), ((), ())), preferred_element_type=F32)
    m_prev = m_ref[...]
    m_next = jnp.maximum(m_prev, s.max(axis=1, keepdims=True))
    alpha = jnp.exp(m_prev - m_next)
    pr = jnp.exp(s - m_next[:, 0:1])
    v_all = jnp.concatenate([page_rows(r) for r in v_refs], axis=0)
    m_ref[...] = m_next
    l_ref[...] = alpha * l_ref[...] + jnp.sum(pr, axis=1, keepdims=True)
    acc_ref[...] = alpha[:, 0:1] * acc_ref[...] + jnp.dot(pr.astype(BF16), v_all, preferred_element_type=F32)

    @pl.when(j == nj - 1)
    def _():
        qf = qbd[...]
        tok = lax.broadcasted_iota(jnp.int32, (R, LANES), 0) & (SUBLANES - 1)
        m_p = m_ref[...]
        sn = []
        for jn in range(tnew):
            sj = jnp.sum(qf * kn_ref[jn:jn + 1, :], axis=1, keepdims=True)
            sn.append(jnp.where(tok >= jn, sj, MASK_VALUE))
        m_n = m_p
        for sj in sn:
            m_n = jnp.maximum(m_n, sj)
        al = jnp.exp(m_p - m_n)
        l_f = al * l_ref[...]
        acc_f = al[:, 0:1] * acc_ref[...]
        for jn in range(tnew):
            pj = jnp.exp(sn[jn] - m_n)
            l_f = l_f + pj
            acc_f = acc_f + pj[:, 0:1] * vn_ref[jn:jn + 1, :]
        lam = _lambda(lq1[...], lk1[...], lq2[...], lk2[...], lam_init)
        for h in range(heads):
            cols = slice(h * HEAD, (h + 1) * HEAD)
            r1 = slice(h * 2 * SUBLANES, h * 2 * SUBLANES + SUBLANES)
            r2 = slice(h * 2 * SUBLANES + SUBLANES, (h + 1) * 2 * SUBLANES)
            o1 = acc_f[r1, cols] / l_f[r1, :]
            o2 = acc_f[r2, cols] / l_f[r2, :]
            o = o1 - lam * o2
            o_ref[:, cols] = (_rms(o, sub_ref[:, cols]) * (1.0 - lam_init)).astype(BF16)


def _hgrn_sattn_kernel(pt_ref, *refs, n_pages_in, hg_kw, sa_kw, NH, NA, nt_h, nj_a):
    h_in, refs = refs[:7], refs[7:]
    a_in, refs = refs[:8 + n_pages_in], refs[8 + n_pages_in:]
    h_out, a_out, h_scr, a_scr = refs[:2], refs[2:3], refs[3:4], refs[4:]
    s = pl.program_id(0)

    @pl.when(s < NH)
    def _():
        _hgrn_kernel(*h_in, *h_out, *h_scr, t=s % nt_h, nt=nt_h, **hg_kw)

    @pl.when(s < NA)
    def _():
        _sattn_kernel(pt_ref, *a_in, *a_out, *a_scr, j=s % nj_a, nj=nj_a, **sa_kw)


def hgrn2_with_sample_attention(main, s0, lb, gn, chunk, qs8, kn8, vn8, cache_k, cache_v, layer, page_table,
                                lams, subln, lam_init, tnew):
    B, T, _ = main.shape
    heads = s0.shape[1]
    W = heads * HEAD
    DB, _, WA = qs8.shape
    aheads = WA // HEAD
    page = cache_k.shape[3]
    n_pages = page_table.shape[1]
    pps = math.gcd(PAGES_PER_STEP, n_pages)
    nj_a = n_pages // pps
    NA = DB * nj_a
    cands = [c for c in (chunk, 2 * chunk, 4 * chunk) if T % c == 0]
    tt = min(cands, key=lambda c: abs(math.log((B * T // c) / NA)))
    nt_h = T // tt
    NH = B * nt_h
    R = aheads * 2 * SUBLANES

    def hb(s):
        return jnp.minimum(s, NH - 1) // nt_h

    def ht(s):
        return jnp.minimum(s, NH - 1) % nt_h

    def ab(s):
        return jnp.minimum(s, NA - 1) // nj_a

    def aj(s):
        return jnp.minimum(s, NA - 1) % nj_a

    blk = lambda c: pl.BlockSpec((None, tt, W), lambda s, pt: (hb(s), ht(s), c))
    vec = pl.BlockSpec((1, W), lambda s, pt: (0, 0))
    sblk = pl.BlockSpec((None, heads, HEAD, HEAD), lambda s, pt: (hb(s), 0, 0, 0))
    tok_spec = pl.BlockSpec((None, SUBLANES, WA), lambda s, pt: (ab(s), 0, 0))
    lam_spec = pl.BlockSpec((1, QK_HALF), lambda s, pt: (0, 0))

    def page_spec(p):
        return pl.BlockSpec((None, None, aheads, page, HEAD),
                            lambda s, pt: (layer, pt[ab(s), aj(s) * pps + p], 0, 0, 0))

    grid_spec = pltpu.PrefetchScalarGridSpec(
        num_scalar_prefetch=1,
        grid=(max(NH, NA),),
        in_specs=[blk(0), blk(1), blk(2), blk(3), sblk, vec, vec,
                  tok_spec, tok_spec, tok_spec, lam_spec, lam_spec, lam_spec, lam_spec,
                  pl.BlockSpec((1, WA), lambda s, pt: (0, 0))]
                 + [page_spec(p) for p in range(pps)] + [page_spec(p) for p in range(pps)],
        out_specs=[blk(0), sblk, tok_spec],
        scratch_shapes=[pltpu.VMEM((heads, HEAD, HEAD), F32),
                        pltpu.VMEM((R, WA), F32),
                        pltpu.VMEM((R, LANES), F32),
                        pltpu.VMEM((R, LANES), F32),
                        pltpu.VMEM((R, WA), F32)],
    )
    return pl.pallas_call(
        functools.partial(_hgrn_sattn_kernel, n_pages_in=2 * pps,
                          hg_kw=dict(tt=tt, chunk=chunk, heads=heads, valid=None),
                          sa_kw=dict(heads=aheads, tnew=tnew, lam_init=lam_init, pps=pps),
                          NH=NH, NA=NA, nt_h=nt_h, nj_a=nj_a),
        grid_spec=grid_spec,
        out_shape=[jax.ShapeDtypeStruct((B, T, W), BF16), jax.ShapeDtypeStruct(s0.shape, F32),
                   jax.ShapeDtypeStruct((DB, SUBLANES, WA), BF16)],
        compiler_params=_params(("arbitrary",)),
        name="hgrn2_sample_attention",
    )(page_table, main, main, main, main, s0, lb, gn, qs8, kn8, vn8, *lams, subln,
      *([cache_k] * pps), *([cache_v] * pps))


def _oproj_kernel(a_ref, b_ref, c_ref, w_ref, x_ref, gpost_ref, gnext_ref, xo_ref, h_ref, *, wa, wb):
    tm = x_ref.shape[0]
    rg = min(tm, OPROJ_ROWS)
    ys = []
    for r in range(tm // rg):
        rows = slice(r * rg, (r + 1) * rg)
        y = jnp.dot(a_ref[rows, :], w_ref[0:wa, :], preferred_element_type=F32)
        y = y + jnp.dot(b_ref[rows, :], w_ref[wa:wa + wb, :], preferred_element_type=F32)
        ys.append(y + jnp.dot(c_ref[rows, :], w_ref[wa + wb:, :], preferred_element_type=F32))
    for r in range(tm // rg):
        rows = slice(r * rg, (r + 1) * rg)
        x = x_ref[rows, :] + _rms(ys[r], gpost_ref[...])
        xo_ref[rows, :] = x
        h_ref[rows, :] = _rms(x, gnext_ref[...]).astype(BF16)


def out_proj(a, b, c, w, layer, x, gpost, gnext, tm):
    M, D = x.shape
    wa, wb, wc = a.shape[1], b.shape[1], c.shape[1]
    row = lambda width: pl.BlockSpec((tm, width), lambda i: (i, 0))
    vec = pl.BlockSpec((1, D), lambda i: (0, 0))
    return pl.pallas_call(
        functools.partial(_oproj_kernel, wa=wa, wb=wb),
        grid=(M // tm,),
        in_specs=[row(wa), row(wb), row(wc), pl.BlockSpec((None, D, D), lambda i: (layer, 0, 0)), row(D), vec, vec],
        out_specs=[row(D), row(D)],
        out_shape=[jax.ShapeDtypeStruct((M, D), F32), jax.ShapeDtypeStruct((M, D), BF16)],
        compiler_params=_params(("parallel",)),
        name="out_proj",
    )(a, b, c, w, x, gpost, gnext)


def _ffn_up_seq_kernel(h_ref, wg_ref, wu_ref, cw_ref, cb_ref, prev_ref, act_ref, st_ref,
                       wgb, wub, buf, *, tm, groups):
    b = pl.program_id(1)
    t = pl.program_id(2)
    H = SUBLANES
    tn = buf.shape[1]
    gw = tn // groups

    @pl.when(t == 0)
    def _():
        @pl.when(b == 0)
        def _():
            wgb[...] = wg_ref[...].astype(BF16)
            wub[...] = wu_ref[...].astype(BF16)

        buf[0:H - 2, :] = jnp.zeros((H - 2, tn), F32)
        buf[H - 2:H, :] = prev_ref[...]

    rg = min(tm, FFN_UP_ROWS)
    blocks = [(r, g) for r in range(tm // rg) for g in range(groups)]
    prods = []
    for r, g in blocks:
        cols = slice(g * gw, (g + 1) * gw)
        h = h_ref[r * rg:(r + 1) * rg, :]
        prods.append((jnp.dot(h, wgb[:, cols], preferred_element_type=F32),
                      jnp.dot(h, wub[:, cols], preferred_element_type=F32)))
    for (r, g), (u, up) in zip(blocks, prods):
        cols = slice(g * gw, (g + 1) * gw)
        r0 = H + r * rg
        buf[r0:r0 + rg, cols] = u
        c = cb_ref[:, cols] + cw_ref[2:3, cols] * u
        c = c + cw_ref[1:2, cols] * buf[r0 - 1:r0 - 1 + rg, cols]
        c = c + cw_ref[0:1, cols] * buf[r0 - 2:r0 - 2 + rg, cols]
        act_ref[r * rg:(r + 1) * rg, cols] = (c * (1.0 / (1.0 + jnp.exp(-c))) * up).astype(BF16)
    tail = buf[tm:tm + H, :]
    st_ref[...] = tail[H - 2:, :]
    buf[0:H, :] = tail


def ffn_up_seq(h3, wg, wu, layer, cw, cb, prev, tn):
    B, T, D = h3.shape
    Fd = wg.shape[2]
    tm = min(T, 1024)
    groups = max(1, tn // MXU_COLS)
    wspec = pl.BlockSpec((None, D, tn), lambda j, b, t: (layer, 0, j))
    return pl.pallas_call(
        functools.partial(_ffn_up_seq_kernel, tm=tm, groups=groups),
        grid=(Fd // tn, B, T // tm),
        in_specs=[pl.BlockSpec((None, tm, D), lambda j, b, t: (b, t, 0)), wspec, wspec,
                  pl.BlockSpec((CONV_W, tn), lambda j, b, t: (0, j)),
                  pl.BlockSpec((1, tn), lambda j, b, t: (0, j)),
                  pl.BlockSpec((None, CONV_W - 1, tn), lambda j, b, t: (b, 0, j))],
        out_specs=[pl.BlockSpec((None, tm, tn), lambda j, b, t: (b, t, j)),
                   pl.BlockSpec((None, CONV_W - 1, tn), lambda j, b, t: (b, 0, j))],
        out_shape=[jax.ShapeDtypeStruct((B, T, Fd), BF16),
                   jax.ShapeDtypeStruct((B, CONV_W - 1, Fd), F32)],
        scratch_shapes=[pltpu.VMEM((D, tn), BF16), pltpu.VMEM((D, tn), BF16),
                        pltpu.VMEM((SUBLANES + tm, tn), F32)],
        compiler_params=_params(("arbitrary", "arbitrary", "arbitrary")),
        name="ffn_up_prompt",
    )(h3, wg, wu, cw, cb, prev)


def _ffn_up_short_kernel(h_ref, wg_ref, wu_ref, cw_ref, cb_ref, p1_ref, p2_ref, act_ref, u_ref, *, T):
    h = h_ref[...]
    u = jnp.dot(h, wg_ref[...].astype(BF16), preferred_element_type=F32)
    up = jnp.dot(h, wu_ref[...].astype(BF16), preferred_element_type=F32)
    u_ref[...] = u
    r = lax.broadcasted_iota(jnp.int32, u.shape, 0) & (T - 1)
    um1 = jnp.where(r >= 1, pltpu.roll(u, 1, axis=0), p1_ref[...])
    um2 = jnp.where(r >= 2, pltpu.roll(u, 2, axis=0), p2_ref[...])
    c = cb_ref[...] + cw_ref[2:3, :] * u + cw_ref[1:2, :] * um1 + cw_ref[0:1, :] * um2
    sg, _ = _sigmoid_pair(c)
    act_ref[...] = (c * sg * up).astype(BF16)


def ffn_up_short(h, wg, wu, layer, cw, cb, p1, p2, T, tn):
    M, D = h.shape
    Fd = wg.shape[2]
    wspec = pl.BlockSpec((None, D, tn), lambda j: (layer, 0, j))
    col = pl.BlockSpec((M, tn), lambda j: (0, j))
    return pl.pallas_call(
        functools.partial(_ffn_up_short_kernel, T=T),
        grid=(Fd // tn,),
        in_specs=[pl.BlockSpec((M, D), lambda j: (0, 0)), wspec, wspec,
                  pl.BlockSpec((CONV_W, tn), lambda j: (0, j)),
                  pl.BlockSpec((1, tn), lambda j: (0, j)), col, col],
        out_specs=[col, col],
        out_shape=[jax.ShapeDtypeStruct((M, Fd), BF16), jax.ShapeDtypeStruct((M, Fd), F32)],
        compiler_params=_params(("parallel",)),
        name="ffn_up_sample",
    )(h, wg, wu, cw, cb, p1, p2)


def _ffn_down_kernel(a_ref, w_ref, x_ref, gpost_ref, gnext_ref, xo_ref, h_ref):
    y = jnp.dot(a_ref[...], w_ref[...], preferred_element_type=F32)
    x = x_ref[...] + _rms(y, gpost_ref[...])
    xo_ref[...] = x
    h_ref[...] = _rms(x, gnext_ref[...]).astype(BF16)


def ffn_down(act, w, layer, x, gpost, gnext, tm):
    M, D = x.shape
    Fd = act.shape[1]
    row = pl.BlockSpec((tm, D), lambda i: (i, 0))
    vec = pl.BlockSpec((1, D), lambda i: (0, 0))
    return pl.pallas_call(
        _ffn_down_kernel,
        grid=(M // tm,),
        in_specs=[pl.BlockSpec((tm, Fd), lambda i: (i, 0)),
                  pl.BlockSpec((None, Fd, D), lambda i: (layer, 0, 0), pipeline_mode=pl.Buffered(1)),
                  row, vec, vec],
        out_specs=[row, row],
        out_shape=[jax.ShapeDtypeStruct((M, D), F32), jax.ShapeDtypeStruct((M, D), BF16)],
        compiler_params=_params(("arbitrary",)),
        name="ffn_down",
    )(act, w, x, gpost, gnext)


def _largest_tile(n, cap, quantum):
    best = quantum
    for t in range(quantum, cap + 1, quantum):
        if n % t == 0:
            best = t
    return best


def _project(h, l, lw, B, T):
    M = B * T
    tm = min(M, 1024)
    WA = lw['pool_scale'].shape[1]
    WH = lw['hgrn_norm'].shape[1]
    WM = lw['w_main'].shape[2]
    xa = matmul(h, lw['w_pool'], l, 0, WA, tm, WA, "proj_pool").reshape(B, T, WA)
    main = matmul(h, lw['w_main'], l, 0, WM, tm, WH, "proj_main").reshape(B, T, WM)
    return xa, main


def _finish(x, a_out, o_b, o_c, l, lw, B, T, prev_conv, short):
    M, D = x.shape
    WA, WH, WC = a_out.shape[-1], o_b.shape[-1], o_c.shape[-1]
    x, h2 = out_proj(a_out.reshape(M, WA), o_b.reshape(M, WH), o_c.reshape(M, WC), lw['w_out'], l,
                     x, lw['norm_mix_post'], lw['norm_ffn_pre'], min(M, 2 * OPROJ_ROWS))
    Fd = lw['w_gate'].shape[2]
    tn = _largest_tile(Fd, 512, LANES)
    if not short:
        act, conv_new = ffn_up_seq(h2.reshape(B, T, D), lw['w_gate'], lw['w_up'], l, lw['conv_w'],
                                   lw['conv_b'], prev_conv, tn)
        act = act.reshape(M, Fd)
    else:
        z = jnp.zeros((B, T, Fd), F32)
        p1 = z.at[:, 0].set(prev_conv[:, 1]).reshape(M, Fd)
        p2 = z.at[:, 0].set(prev_conv[:, 0]).at[:, 1].set(prev_conv[:, 1]).reshape(M, Fd)
        act, u = ffn_up_short(h2, lw['w_gate'], lw['w_up'], l, lw['conv_w'], lw['conv_b'], p1, p2, T, tn)
        conv_new = u.reshape(B, T, Fd)[:, T - (CONV_W - 1):]
    x, h_next = ffn_down(act, lw['w_down'], l, x, lw['norm_ffn_post'], lw['norm_next'], min(M, FFN_DOWN_ROWS))
    return x, h_next, conv_new


def _layer(l, lw, xp, hp, xs, hs, *, B, T, DB, TS, P, zero_states, states, paged, kv_stacks):
    WC = lw['attn_subln'].shape[1]
    lam_init = 0.8 - 0.6 * math.exp(-0.3 * l)
    lams = (lw['lam_q1'], lw['lam_k1'], lw['lam_q2'], lw['lam_k2'])
    zp_pool, zp_S, zp_conv = zero_states
    sp_pool, sp_S, sp_conv = states
    cache_k, cache_v, page_table = paged

    xa_p, main_p = _project(hp, l, lw, B, T)
    xa_s, main_s = _project(hs, l, lw, DB, TS)
    a_p, pool_p = pool_mixer(xa_p, zp_pool, lw['pool_w'], lw['pool_scale'], 0)
    a_s, pool_s = pool_mixer(xa_s, sp_pool, lw['pool_w'], lw['pool_scale'], P)

    pad = ((0, 0), (0, SUBLANES - TS), (0, 0))
    main8 = jnp.pad(main_s, pad)
    ob8, S_s = hgrn2(main8, sp_S, lw['hgrn_lb'], lw['hgrn_norm'], SUBLANES, TS)
    qs8, k_rot8, _ = rope(main8, 4, 5, WC, lw['rope_tabs'], P)
    v_s = main_s[:, :, 6 * WC:]

    qs, kb, vb, k_stack, v_stack = rope_heads(main_p, 4, WC, lw['rope_tabs'], 0, *kv_stacks, l)
    ob_p, S_p, oc8 = hgrn2_with_sample_attention(
        main_p, zp_S, lw['hgrn_lb'], lw['hgrn_norm'], min(HG_CHUNK, T),
        qs8.astype(F32), k_rot8, jnp.pad(v_s, pad), cache_k, cache_v, l, page_table,
        lams, lw['attn_subln'], lam_init, TS)
    oc_p = prompt_attention(qs, kb, vb, lams, lw['attn_subln'], lam_init)

    xp, hp, conv_p = _finish(xp, a_p, ob_p, oc_p, l, lw, B, T, zp_conv, False)
    xs, hs, conv_s = _finish(xs, a_s, ob8[:, :TS], oc8[:, :TS], l, lw, DB, TS, sp_conv, True)
    return (xp, hp, xs, hs, (k_stack, v_stack), (pool_p, S_p, conv_p),
            (k_rot8[:, :TS], v_s, pool_s, S_s, conv_s))


def kernel(x_prompt, x_sample, cache_k, cache_v, state_pool, state_hgrn, state_conv, page_table, w_in, w_out, norm_mix_pre, norm_mix_post, norm_ffn_pre, norm_ffn_post, pool_w, pool_scale, hgrn_lower_bounds, hgrn_norm, attn_lam_q1, attn_lam_k1, attn_lam_q2, attn_lam_k2, attn_subln, ffn_w_gate, ffn_w_up, ffn_conv_w, ffn_conv_b, ffn_w_down):
    depth = w_in.shape[0]
    B, T, D = x_prompt.shape
    DB, TS, _ = x_sample.shape
    page = cache_k.shape[2]
    P = page_table.shape[1] * page
    WA = pool_scale.shape[1]
    WH = hgrn_norm.shape[1]
    WC = attn_subln.shape[1]
    HH = WH // HEAD
    Fd = ffn_w_gate.shape[2]
    assert TS >= CONV_W - 1 and TS <= SUBLANES and (TS & (TS - 1)) == 0

    sm = jax.nn.softmax(hgrn_lower_bounds.astype(F32), axis=0)
    lbs = jnp.cumsum(sm, axis=0) - sm[0]

    half = ROT_DIM // 2
    freqs = jnp.power(ROPE_THETA, -jnp.arange(0, ROT_DIM, 2, dtype=F32) / ROT_DIM)
    lane = np.arange(LANES)
    in_rot = (lane % QK_HALF) < ROT_DIM
    fl = jnp.where(jnp.asarray(in_rot), jnp.tile(freqs, LANES // half), 0.0).reshape(1, LANES)
    s1 = jnp.asarray(np.where(in_rot & ((lane % QK_HALF) < half), -1.0, 0.0), F32).reshape(1, LANES)
    s2 = jnp.asarray(np.where(in_rot & ((lane % QK_HALF) >= half), 1.0, 0.0), F32).reshape(1, LANES)

    ones = jnp.ones((1, D), F32)
    HC = WC // HEAD
    ck = jnp.swapaxes(cache_k, 2, 3)
    cv = jnp.swapaxes(cache_v, 2, 3)
    kv = (jnp.zeros((depth, B, HC, T, HEAD), F32), jnp.zeros((depth, B, HC, T, HEAD), F32))

    w_pool_b = w_in[:, :, :WA].astype(BF16)
    w_main_b = w_in[:, :, WA:].astype(BF16)
    w_out_b = w_out.astype(BF16)
    w_down_b = ffn_w_down.astype(BF16)

    def layer_weights(l):
        return {
            'w_pool': w_pool_b,
            'w_main': w_main_b,
            'w_out': w_out_b,
            'w_gate': ffn_w_gate,
            'w_up': ffn_w_up,
            'w_down': w_down_b,
            'pool_w': pool_w[l],
            'pool_scale': pool_scale[l].reshape(1, WA),
            'hgrn_lb': lbs[l].reshape(1, WH),
            'hgrn_norm': hgrn_norm[l].reshape(1, WH),
            'attn_subln': attn_subln[l].reshape(1, WC),
            'lam_q1': attn_lam_q1[l].reshape(1, QK_HALF),
            'lam_k1': attn_lam_k1[l].reshape(1, QK_HALF),
            'lam_q2': attn_lam_q2[l].reshape(1, QK_HALF),
            'lam_k2': attn_lam_k2[l].reshape(1, QK_HALF),
            'norm_mix_post': norm_mix_post[l].reshape(1, D),
            'norm_ffn_pre': norm_ffn_pre[l].reshape(1, D),
            'norm_ffn_post': norm_ffn_post[l].reshape(1, D),
            'norm_next': norm_mix_pre[l + 1].reshape(1, D) if l + 1 < depth else ones,
            'conv_w': ffn_conv_w[l],
            'conv_b': ffn_conv_b[l].reshape(1, Fd),
            'rope_tabs': (fl, s1, s2),
        }

    xp = x_prompt.reshape(B * T, D)
    xs = x_sample.reshape(DB * TS, D)
    g0 = norm_mix_pre[0].reshape(1, D)
    hp = rmsnorm_bf16(xp, g0, min(B * T, 512))
    hs = rmsnorm_bf16(xs, g0, DB * TS)
    zp_pool = jnp.zeros((B, POOL_STATE, WA), F32)
    zp_S = jnp.zeros((B, HH, HEAD, HEAD), F32)
    zp_conv = jnp.zeros((B, CONV_W - 1, Fd), F32)

    st_p, st_s = [], []
    for l in range(depth):
        xp, hp, xs, hs, kv, sp, ss = _layer(
            l, layer_weights(l), xp, hp, xs, hs, B=B, T=T, DB=DB, TS=TS, P=P,
            zero_states=(zp_pool, zp_S, zp_conv), states=(state_pool[l], state_hgrn[l], state_conv[l]),
            paged=(ck, cv, page_table), kv_stacks=kv)
        st_p.append(sp)
        st_s.append(ss)

    def stack(sts, i, shape):
        return jnp.stack([s[i] for s in sts]).reshape(shape)

    return (xp.reshape(B, T, D), xs.reshape(DB, TS, D),
            jnp.swapaxes(kv[0], 2, 3), jnp.swapaxes(kv[1], 2, 3),
            stack(st_p, 0, (depth, B, POOL_STATE, WA)), stack(st_p, 1, (depth, B, HH, HEAD, HEAD)),
            stack(st_p, 2, (depth, B, CONV_W - 1, Fd)),
            stack(st_s, 0, (depth, DB, TS, HC, HEAD)), stack(st_s, 1, (depth, DB, TS, HC, HEAD)),
            stack(st_s, 2, (depth, DB, POOL_STATE, WA)), stack(st_s, 3, (depth, DB, HH, HEAD, HEAD)),
            stack(st_s, 4, (depth, DB, CONV_W - 1, Fd)))
```

```python
import functools
import math

import numpy as np
import jax
import jax.numpy as jnp
from jax import lax
from jax.experimental import pallas as pl
from jax.experimental.pallas import tpu as pltpu

F32 = jnp.float32
BF16 = jnp.bfloat16

LANES = 128
SUBLANES = 8
MXU_COLS = 256
VMEM_LIMIT = 48 * 1024 * 1024

POOL_WINDOWS = (2, 4, 8, 16)
POOL_STATE = max(POOL_WINDOWS) - 1
HEAD = 128
QK_HALF = 64
QK_SCALE_LOG2E = QK_HALF ** -0.5 * math.log2(math.e)
ROT_DIM = 16
ROPE_THETA = 500000.0
LB_FLOOR = 1e-30
MASK_VALUE = -1e30
CONV_W = 3
EPS = 1e-6
HG_CHUNK = 64
HG_DIAG = 8
HG_HEADS_PER_STEP = 6
PAGES_PER_STEP = 16
OPROJ_ROWS = 256
FFN_UP_ROWS = 512
FFN_DOWN_ROWS = 256


def _params(sem):
    return pltpu.CompilerParams(dimension_semantics=sem, vmem_limit_bytes=VMEM_LIMIT)


def _rms(x, g):
    return x * lax.rsqrt(jnp.mean(x * x, axis=-1, keepdims=True) + EPS) * g


def _sigmoid_pair(z):
    e = jnp.exp(-jnp.abs(z))
    r = 1.0 / (1.0 + e)
    er = e * r
    pos = z >= 0
    return jnp.where(pos, r, er), jnp.where(pos, er, r)


def _norm_kernel(x_ref, g_ref, h_ref):
    h_ref[...] = _rms(x_ref[...], g_ref[...]).astype(BF16)


def rmsnorm_bf16(x, g, tm):
    M, D = x.shape
    return pl.pallas_call(
        _norm_kernel,
        grid=(M // tm,),
        in_specs=[pl.BlockSpec((tm, D), lambda i: (i, 0)),
                  pl.BlockSpec((1, D), lambda i: (0, 0))],
        out_specs=pl.BlockSpec((tm, D), lambda i: (i, 0)),
        out_shape=jax.ShapeDtypeStruct((M, D), BF16),
        compiler_params=_params(("parallel",)),
        name="rmsnorm",
    )(x, g)


def _mm_kernel(x_ref, *rest, nsub):
    w_refs, o_ref, wb = rest[:nsub], rest[nsub], rest[nsub + 1]

    @pl.when(pl.program_id(1) == 0)
    def _():
        for c in range(nsub):
            wb[:, c * MXU_COLS:(c + 1) * MXU_COLS] = w_refs[c][...].astype(BF16)

    o_ref[...] = jnp.dot(x_ref[...], wb[...], preferred_element_type=F32)


def matmul(x, w, layer, col0, N, tm, tn, name):
    M, K = x.shape
    nsub = tn // MXU_COLS
    assert col0 % MXU_COLS == 0 and tn % MXU_COLS == 0 and N % tn == 0
    wspec = lambda c: pl.BlockSpec((None, K, MXU_COLS),
                                   lambda j, i: (layer, 0, col0 // MXU_COLS + j * nsub + c))
    return pl.pallas_call(
        functools.partial(_mm_kernel, nsub=nsub),
        grid=(N // tn, M // tm),
        in_specs=[pl.BlockSpec((tm, K), lambda j, i: (i, 0))] + [wspec(c) for c in range(nsub)],
        out_specs=pl.BlockSpec((tm, tn), lambda j, i: (i, j)),
        out_shape=jax.ShapeDtypeStruct((M, N), F32),
        scratch_shapes=[pltpu.VMEM((K, tn), BF16)],
        compiler_params=_params(("arbitrary", "arbitrary")),
        name=name,
    )(x, *([w] * nsub))


def _pool_kernel(xa_ref, prev_ref, wp_ref, sc_ref, y_ref, st_ref, buf, *, tt, pos0):
    t = pl.program_id(1)
    P = POOL_STATE + 1

    @pl.when(t == 0)
    def _():
        buf[0:1, :] = jnp.zeros((1, buf.shape[1]), F32)
        buf[1:P, :] = prev_ref[...]

    buf[P:P + tt, :] = xa_ref[...]
    pos = pos0 + t * tt + lax.broadcasted_iota(jnp.int32, (tt, LANES), 0)
    for g, w in enumerate(POOL_WINDOWS):
        cols = slice(g * LANES, (g + 1) * LANES)
        x = buf[P:P + tt, cols]
        acc = x
        for j in range(1, w):
            acc = acc + buf[P - j:P - j + tt, cols]
        cnt = jnp.minimum(w, pos + 1).astype(F32)
        d = acc / cnt - x
        y = jnp.dot(d.astype(BF16), wp_ref[g].astype(BF16), preferred_element_type=F32)
        y_ref[:, cols] = (y * sc_ref[:, cols]).astype(BF16)
    tail = buf[tt:tt + P, :]
    st_ref[...] = tail[1:, :]
    buf[0:P, :] = tail


def pool_mixer(xa, prev, wp, scale, pos0):
    B, T, W = xa.shape
    tt = min(T, 512)
    return pl.pallas_call(
        functools.partial(_pool_kernel, tt=tt, pos0=pos0),
        grid=(B, T // tt),
        in_specs=[pl.BlockSpec((None, tt, W), lambda b, t: (b, t, 0)),
                  pl.BlockSpec((None, POOL_STATE, W), lambda b, t: (b, 0, 0)),
                  pl.BlockSpec((len(POOL_WINDOWS), LANES, LANES), lambda b, t: (0, 0, 0)),
                  pl.BlockSpec((1, W), lambda b, t: (0, 0))],
        out_specs=[pl.BlockSpec((None, tt, W), lambda b, t: (b, t, 0)),
                   pl.BlockSpec((None, POOL_STATE, W), lambda b, t: (b, 0, 0))],
        out_shape=[jax.ShapeDtypeStruct((B, T, W), BF16),
                   jax.ShapeDtypeStruct((B, POOL_STATE, W), F32)],
        scratch_shapes=[pltpu.VMEM((POOL_STATE + 1 + tt, W), F32)],
        compiler_params=_params(("parallel", "arbitrary")),
        name="pool_mixer",
    )(xa, prev, wp, scale)


def _rope_kernel(q_ref, k_ref, fl_ref, s1_ref, s2_ref, qs_ref, kr_ref, kb_ref, *, tt, pos0, heads):
    t = pl.program_id(1)
    pos = (pos0 + t * tt + lax.broadcasted_iota(jnp.int32, (tt, LANES), 0)).astype(F32)
    ang = pos * fl_ref[...]
    c = jnp.cos(ang)
    s = jnp.sin(ang)
    sa = s * s1_ref[...]
    sb = s * s2_ref[...]
    half = ROT_DIM // 2
    for h in range(heads):
        cols = slice(h * HEAD, (h + 1) * HEAD)
        xq = q_ref[:, cols]
        yq = xq * c + pltpu.roll(xq, LANES - half, axis=1) * sa + pltpu.roll(xq, half, axis=1) * sb
        qs_ref[:, cols] = (yq * (QK_HALF ** -0.5)).astype(BF16)
        xk = k_ref[:, cols]
        yk = xk * c + pltpu.roll(xk, LANES - half, axis=1) * sa + pltpu.roll(xk, half, axis=1) * sb
        kr_ref[:, cols] = yk
        kb_ref[:, cols] = yk.astype(BF16)


def rope(main, qblk, kblk, width, tabs, pos0):
    B, T, _ = main.shape
    tt = min(T, 512)
    heads = width // HEAD
    tab = pl.BlockSpec((1, LANES), lambda b, t: (0, 0))
    blk = lambda c: pl.BlockSpec((None, tt, width), lambda b, t: (b, t, c))
    return pl.pallas_call(
        functools.partial(_rope_kernel, tt=tt, pos0=pos0, heads=heads),
        grid=(B, T // tt),
        in_specs=[blk(qblk), blk(kblk), tab, tab, tab],
        out_specs=[blk(0), blk(0), blk(0)],
        out_shape=[jax.ShapeDtypeStruct((B, T, width), BF16),
                   jax.ShapeDtypeStruct((B, T, width), F32),
                   jax.ShapeDtypeStruct((B, T, width), BF16)],
        compiler_params=_params(("parallel", "parallel")),
        name="rope",
    )(main, main, *tabs)


def _rope_heads_kernel(q_ref, k_ref, v_ref, fl_ref, s1_ref, s2_ref, kin_ref, vin_ref,
                       qs_ref, kb_ref, vb_ref, ko_ref, vo_ref, *, tt, pos0, heads):
    del kin_ref, vin_ref
    t = pl.program_id(1)
    pos = (pos0 + t * tt + lax.broadcasted_iota(jnp.int32, (tt, LANES), 0)).astype(F32)
    ang = pos * fl_ref[...]
    c = jnp.cos(ang)
    s = jnp.sin(ang)
    sa = s * s1_ref[...]
    sb = s * s2_ref[...]
    half = ROT_DIM // 2
    for h in range(heads):
        cols = slice(h * HEAD, (h + 1) * HEAD)
        xq = q_ref[:, cols]
        yq = xq * c + pltpu.roll(xq, LANES - half, axis=1) * sa + pltpu.roll(xq, half, axis=1) * sb
        qs_ref[h] = (yq * QK_SCALE_LOG2E).astype(BF16)
        xk = k_ref[:, cols]
        yk = xk * c + pltpu.roll(xk, LANES - half, axis=1) * sa + pltpu.roll(xk, half, axis=1) * sb
        ko_ref[h] = yk
        kb_ref[h] = yk.astype(BF16)
        xv = v_ref[:, cols]
        vo_ref[h] = xv
        vb_ref[h, :, 0:HEAD] = xv.astype(BF16)
        vb_ref[h, :, HEAD:2 * HEAD] = jnp.ones((tt, HEAD), BF16)


def rope_heads(main, qblk, width, tabs, pos0, k_stack, v_stack, layer):
    B, T, _ = main.shape
    tt = min(T, 512)
    heads = width // HEAD
    tab = pl.BlockSpec((1, LANES), lambda b, t: (0, 0))
    blk = lambda c: pl.BlockSpec((None, tt, width), lambda b, t: (b, t, c))
    hm = pl.BlockSpec((None, heads, tt, HEAD), lambda b, t: (b, 0, t, 0))
    st = pl.BlockSpec((None, None, heads, tt, HEAD), lambda b, t: (layer, b, 0, t, 0))
    anyspec = pl.BlockSpec(memory_space=pl.ANY)
    hshape = jax.ShapeDtypeStruct((B, heads, T, HEAD), BF16)
    vshape = jax.ShapeDtypeStruct((B, heads, T, 2 * HEAD), BF16)
    vm = pl.BlockSpec((None, heads, tt, 2 * HEAD), lambda b, t: (b, 0, t, 0))
    return pl.pallas_call(
        functools.partial(_rope_heads_kernel, tt=tt, pos0=pos0, heads=heads),
        grid=(B, T // tt),
        in_specs=[blk(qblk), blk(qblk + 1), blk(qblk + 2), tab, tab, tab, anyspec, anyspec],
        out_specs=[hm, hm, vm, st, st],
        out_shape=[hshape, hshape, vshape,
                   jax.ShapeDtypeStruct(k_stack.shape, F32), jax.ShapeDtypeStruct(v_stack.shape, F32)],
        input_output_aliases={6: 3, 7: 4},
        compiler_params=_params(("parallel", "parallel")),
        name="rope_heads",
    )(main, main, main, *tabs, k_stack, v_stack)


def _hgrn_kernel(q_ref, f_ref, i_ref, g_ref, s0_ref, lb_ref, gn_ref, o_ref, so_ref, st_ref,
                 *, tt, chunk, heads, valid, t=None, nt=None):
    if t is None:
        t, nt = pl.program_id(2), pl.num_programs(2)
    C = chunk
    levels = [n for n in (64, 32, 16) if n <= C]

    @pl.when(t == 0)
    def _():
        for h in range(heads):
            st_ref[h] = s0_ref[h].T

    row = lax.broadcasted_iota(jnp.int32, (C, LANES), 0)
    rr = lax.broadcasted_iota(jnp.int32, (C, C), 0)
    cc = lax.broadcasted_iota(jnp.int32, (C, C), 1)
    lvl_masks = []
    for n in levels:
        sh = int(math.log2(n))
        same = (rr >> sh) == (cc >> sh)
        up = (rr & (n - 1)) >= n // 2
        lo = (cc & (n - 1)) < n // 2
        lvl_masks.append(jnp.where(same, jnp.where(up, jnp.where(lo, 1.0, 0.0), 0.0), 0.0))

    def chunk_body(c, carry):
        r0 = pl.multiple_of(c * C, C)
        rows = pl.ds(r0, C)
        for h in range(heads):
            cols = slice(h * HEAD, (h + 1) * HEAD)
            q = q_ref[rows, cols]
            z = f_ref[rows, cols]
            v = i_ref[rows, cols]
            gate = g_ref[rows, cols]
            lb = lb_ref[:, cols]
            lbm = jnp.maximum(lb, LB_FLOOR)
            oml = 1.0 - lb
            sg, sgn = _sigmoid_pair(z)
            f = lbm + oml * sg
            kin = oml * sgn - (lbm - lb)
            if valid is not None:
                ok = (r0 + row) < valid
                f = jnp.where(ok, f, 1.0)
                kin = jnp.where(ok, kin, 0.0)
            lf = jnp.log(f)
            b = lf
            s = 1
            while s < C:
                b = b + jnp.where(row >= s, pltpu.roll(b, s, axis=0), 0.0)
                s *= 2
            St = st_ref[h]
            qd = q * jnp.exp(b)
            o = lax.dot_general(qd.astype(BF16), St.astype(BF16), (((1,), (1,)), ((), ())),
                                preferred_element_type=F32)
            if levels:
                att = jnp.zeros((C, C), F32)
                for n, msk in zip(levels, lvl_masks):
                    pieces = []
                    for m in range(C // n):
                        rb = m * n + n // 2 - 1
                        pieces.append(jnp.broadcast_to(b[rb:rb + 1, :], (n, LANES)))
                    R = pieces[0] if len(pieces) == 1 else jnp.concatenate(pieces, axis=0)
                    qn = q * jnp.exp(jnp.minimum(b - R, 0.0))
                    kn = kin * jnp.exp(jnp.minimum(R - b, 0.0))
                    a = lax.dot_general(qn.astype(BF16), kn.astype(BF16), (((1,), (1,)), ((), ())),
                                        preferred_element_type=F32)
                    att = att + a * msk
                o = o + jnp.dot(att.astype(BF16), v.astype(BF16), preferred_element_type=F32)
            def back(x, j):
                return pltpu.roll(x.reshape(C // HG_DIAG, HG_DIAG, LANES), j, axis=1).reshape(C, LANES)

            rin = row & (HG_DIAG - 1)
            o = o + jnp.sum(q * kin, axis=-1, keepdims=True) * v
            e = f
            for j in range(1, HG_DIAG):
                if j > 1:
                    e = e * back(f, j - 1)
                kj = jnp.where(rin >= j, back(kin, j), 0.0)
                cj = jnp.sum(q * kj * e, axis=-1, keepdims=True)
                o = o + cj * back(v, j)
            bl = b[C - 1:C, :]
            kd = kin * jnp.exp(bl - b)
            upd = jnp.dot(v.T.astype(BF16), kd.astype(BF16), preferred_element_type=F32)
            st_ref[h] = St * jnp.exp(bl) + upd
            sgg, _ = _sigmoid_pair(gate)
            o_ref[rows, cols] = (_rms(o, gn_ref[:, cols]) * (gate * sgg)).astype(BF16)
        return carry

    lax.fori_loop(0, tt // C, chunk_body, 0)

    @pl.when(t == nt - 1)
    def _():
        for h in range(heads):
            so_ref[h] = st_ref[h].T


def hgrn2(main, s0, lb, gn, chunk, valid):
    B, T, _ = main.shape
    heads = s0.shape[1]
    hp = math.gcd(HG_HEADS_PER_STEP, heads)
    G = heads // hp
    W = heads * HEAD
    tt = min(T, 256)
    blk = lambda c: pl.BlockSpec((None, tt, hp * HEAD), lambda b, g, t: (b, t, c * G + g))
    vec = pl.BlockSpec((1, hp * HEAD), lambda b, g, t: (0, g))
    sblk = pl.BlockSpec((None, hp, HEAD, HEAD), lambda b, g, t: (b, g, 0, 0))
    return pl.pallas_call(
        functools.partial(_hgrn_kernel, tt=tt, chunk=chunk, heads=hp, valid=valid),
        grid=(B, G, T // tt),
        in_specs=[blk(0), blk(1), blk(2), blk(3), sblk, vec, vec],
        out_specs=[blk(0), sblk],
        out_shape=[jax.ShapeDtypeStruct((B, T, W), BF16),
                   jax.ShapeDtypeStruct(s0.shape, F32)],
        scratch_shapes=[pltpu.VMEM((hp, HEAD, HEAD), F32)],
        compiler_params=_params(("parallel", "parallel", "arbitrary")),
        name="hgrn2",
    )(main, main, main, main, s0, lb, gn)


def _lambda(lq1, lk1, lq2, lk2, lam_init):
    a = jnp.exp(jnp.sum(lq1 * lk1, axis=-1, keepdims=True))
    b = jnp.exp(jnp.sum(lq2 * lk2, axis=-1, keepdims=True))
    return a - b + lam_init


def _pattn_kernel(qi_ref, kj_ref, q_ref, kc_ref, kn_ref, v_ref, lq1, lk1, lq2, lk2, sub_ref, o_ref,
                  qst, m_ref, acc_ref, s_even, s_odd, *, tq, rs, lam_init):
    n = pl.program_id(2)
    qi = qi_ref[n]
    kj = kj_ref[n]

    def scores(k_ref):
        return lax.dot_general(qst[...], k_ref[...], (((1,), (1,)), ((), ())), preferred_element_type=F32)

    @pl.when(kj == 0)
    def _():
        q = q_ref[...]
        lane = lax.broadcasted_iota(jnp.int32, q.shape, 1)
        zero = jnp.zeros_like(q)
        qst[0:tq, :] = jnp.where(lane < QK_HALF, q, zero)
        qst[tq:2 * tq, :] = jnp.where(lane >= QK_HALF, q, zero)
        m_ref[...] = jnp.full(m_ref.shape, MASK_VALUE, F32)
        acc_ref[...] = jnp.zeros(acc_ref.shape, F32)
        s_even[...] = scores(kc_ref)

    def consume(s_ref, diag):
        for c in range(2 * tq // rs):
            rows = slice(c * rs, (c + 1) * rs)
            q0 = (c * rs) % tq
            kw = q0 + rs if diag else tq
            s = s_ref[rows, 0:kw]
            if diag:
                rq = lax.broadcasted_iota(jnp.int32, s.shape, 0) + q0
                ck = lax.broadcasted_iota(jnp.int32, s.shape, 1)
                s = jnp.where(ck <= rq, s, MASK_VALUE)
            m_prev = m_ref[rows, :]
            m_next = jnp.maximum(m_prev, jnp.max(s, axis=1, keepdims=True))
            p = jnp.exp2(s - m_next[:, 0:1])
            alpha = jnp.exp2(m_prev - m_next)
            pv = jnp.dot(p.astype(BF16), v_ref[0:kw, :], preferred_element_type=F32)
            acc_ref[rows, 0:HEAD] = alpha * acc_ref[rows, 0:HEAD] + pv[:, 0:HEAD]
            acc_ref[rows, HEAD:2 * HEAD] = alpha * acc_ref[rows, HEAD:2 * HEAD] + pv[:, HEAD:2 * HEAD]
            m_ref[rows, :] = m_next

    for parity, cur, nxt in ((0, s_even, s_odd), (1, s_odd, s_even)):
        mine = (kj & 1) == parity

        @pl.when(jnp.logical_and(kj < qi, mine))
        def _():
            nxt[...] = scores(kn_ref)
            consume(cur, False)

        @pl.when(jnp.logical_and(kj == qi, mine))
        def _():
            consume(cur, True)
            lam = _lambda(lq1[...], lk1[...], lq2[...], lk2[...], lam_init)
            o1 = acc_ref[0:tq, 0:HEAD] / acc_ref[0:tq, HEAD:2 * HEAD]
            o2 = acc_ref[tq:2 * tq, 0:HEAD] / acc_ref[tq:2 * tq, HEAD:2 * HEAD]
            o = o1 - lam * o2
            o_ref[...] = (_rms(o, sub_ref[...]) * (1.0 - lam_init)).astype(BF16)


def prompt_attention(qs, kb, vb, lams, subln, lam_init):
    B, heads, T, _ = qs.shape
    tq = min(T, 1024)
    rs = min(tq, 256)
    nq = T // tq
    pairs = [(i, j) for i in range(nq) for j in range(i + 1)]
    qi = jnp.asarray(np.array([p[0] for p in pairs], np.int32))
    kj = jnp.asarray(np.array([p[1] for p in pairs], np.int32))
    lam_spec = pl.BlockSpec((1, QK_HALF), lambda b, h, n, qi, kj: (0, 0))
    npairs = len(pairs)
    kc_spec = pl.BlockSpec((None, None, tq, HEAD), lambda b, h, n, qi, kj: (b, h, kj[n], 0))
    kn_spec = pl.BlockSpec((None, None, tq, HEAD),
                           lambda b, h, n, qi, kj: (b, h, kj[jnp.minimum(n + 1, npairs - 1)], 0))
    grid_spec = pltpu.PrefetchScalarGridSpec(
        num_scalar_prefetch=2,
        grid=(B, heads, len(pairs)),
        in_specs=[pl.BlockSpec((None, None, tq, HEAD), lambda b, h, n, qi, kj: (b, h, qi[n], 0)),
                  kc_spec, kn_spec,
                  pl.BlockSpec((None, None, tq, 2 * HEAD), lambda b, h, n, qi, kj: (b, h, kj[n], 0)),
                  lam_spec, lam_spec, lam_spec, lam_spec,
                  pl.BlockSpec((1, HEAD), lambda b, h, n, qi, kj: (0, h))],
        out_specs=pl.BlockSpec((None, tq, HEAD), lambda b, h, n, qi, kj: (b, qi[n], h)),
        scratch_shapes=[pltpu.VMEM((2 * tq, HEAD), BF16),
                        pltpu.VMEM((2 * tq, HEAD), F32),
                        pltpu.VMEM((2 * tq, 2 * HEAD), F32),
                        pltpu.VMEM((2 * tq, tq), F32),
                        pltpu.VMEM((2 * tq, tq), F32)],
    )
    return pl.pallas_call(
        functools.partial(_pattn_kernel, tq=tq, rs=rs, lam_init=lam_init),
        grid_spec=grid_spec,
        out_shape=jax.ShapeDtypeStruct((B, T, heads * HEAD), BF16),
        compiler_params=_params(("parallel", "parallel", "arbitrary")),
        name="prompt_attention",
    )(qi, kj, qs, kb, kb, vb, *lams, subln)


def _sattn_kernel(pt_ref, q_ref, kn_ref, vn_ref, lq1, lk1, lq2, lk2, sub_ref, *rest,
                  heads, tnew, lam_init, pps, j=None, nj=None):
    k_refs = rest[:pps]
    v_refs = rest[pps:2 * pps]
    o_ref = rest[2 * pps]
    qbd, m_ref, l_ref, acc_ref = rest[2 * pps + 1:]
    if j is None:
        j, nj = pl.program_id(1), pl.num_programs(1)
    R = heads * 2 * SUBLANES
    W = heads * HEAD

    @pl.when(j == 0)
    def _():
        q = q_ref[...]
        qrep = jnp.concatenate([q] * (heads * 2), axis=0)
        lane = lax.broadcasted_iota(jnp.int32, (R, W), 1)
        rowi = lax.broadcasted_iota(jnp.int32, (R, W), 0)
        qbd[...] = jnp.where((lane >> 6) == (rowi >> 3), qrep, 0.0)
        m_ref[...] = jnp.full(m_ref.shape, MASK_VALUE, F32)
        l_ref[...] = jnp.zeros(l_ref.shape, F32)
        acc_ref[...] = jnp.zeros(acc_ref.shape, F32)

    def page_rows(ref):
        return jnp.concatenate([ref[h] for h in range(heads)], axis=1).astype(BF16)

    qb = qbd[...].astype(BF16)
    k_all = jnp.concatenate([page_rows(r) for r in k_refs], axis=0)
    s = lax.dot_general(qb, k_all, (((1,), (1,)), ((), ())), preferred_element_type=F32)
    m_prev = m_ref[...]
    m_next = jnp.maximum(m_prev, s.max(axis=1, keepdims=True))
    alpha = jnp.exp(m_prev - m_next)
    pr = jnp.exp(s - m_next[:, 0:1])
    v_all = jnp.concatenate([page_rows(r) for r in v_refs], axis=0)
    m_ref[...] = m_next
    l_ref[...] = alpha * l_ref[...] + jnp.sum(pr, axis=1, keepdims=True)
    acc_ref[...] = alpha[:, 0:1] * acc_ref[...] + jnp.dot(pr.astype(BF16), v_all, preferred_element_type=F32)

    @pl.when(j == nj - 1)
    def _():
        qf = qbd[...]
        tok = lax.broadcasted_iota(jnp.int32, (R, LANES), 0) & (SUBLANES - 1)
        m_p = m_ref[...]
        sn = []
        for jn in range(tnew):
            sj = jnp.sum(qf * kn_ref[jn:jn + 1, :], axis=1, keepdims=True)
            sn.append(jnp.where(tok >= jn, sj, MASK_VALUE))
        m_n = m_p
        for sj in sn:
            m_n = jnp.maximum(m_n, sj)
        al = jnp.exp(m_p - m_n)
        l_f = al * l_ref[...]
        acc_f = al[:, 0:1] * acc_ref[...]
        for jn in range(tnew):
            pj = jnp.exp(sn[jn] - m_n)
            l_f = l_f + pj
            acc_f = acc_f + pj[:, 0:1] * vn_ref[jn:jn + 1, :]
        lam = _lambda(lq1[...], lk1[...], lq2[...], lk2[...], lam_init)
        for h in range(heads):
            cols = slice(h * HEAD, (h + 1) * HEAD)
            r1 = slice(h * 2 * SUBLANES, h * 2 * SUBLANES + SUBLANES)
            r2 = slice(h * 2 * SUBLANES + SUBLANES, (h + 1) * 2 * SUBLANES)
            o1 = acc_f[r1, cols] / l_f[r1, :]
            o2 = acc_f[r2, cols] / l_f[r2, :]
            o = o1 - lam * o2
            o_ref[:, cols] = (_rms(o, sub_ref[:, cols]) * (1.0 - lam_init)).astype(BF16)


def _hgrn_sattn_kernel(pt_ref, *refs, n_pages_in, hg_kw, sa_kw, NH, NA, nt_h, nj_a):
    h_in, refs = refs[:7], refs[7:]
    a_in, refs = refs[:8 + n_pages_in], refs[8 + n_pages_in:]
    h_out, a_out, h_scr, a_scr = refs[:2], refs[2:3], refs[3:4], refs[4:]
    s = pl.program_id(0)

    @pl.when(s < NH)
    def _():
        _hgrn_kernel(*h_in, *h_out, *h_scr, t=s % nt_h, nt=nt_h, **hg_kw)

    @pl.when(s < NA)
    def _():
        _sattn_kernel(pt_ref, *a_in, *a_out, *a_scr, j=s % nj_a, nj=nj_a, **sa_kw)


def hgrn2_with_sample_attention(main, s0, lb, gn, chunk, qs8, kn8, vn8, cache_k, cache_v, layer, page_table,
                                lams, subln, lam_init, tnew):
    B, T, _ = main.shape
    heads = s0.shape[1]
    W = heads * HEAD
    DB, _, WA = qs8.shape
    aheads = WA // HEAD
    page = cache_k.shape[3]
    n_pages = page_table.shape[1]
    pps = math.gcd(PAGES_PER_STEP, n_pages)
    nj_a = n_pages // pps
    NA = DB * nj_a
    cands = [c for c in (chunk, 2 * chunk, 4 * chunk) if T % c == 0]
    tt = min(cands, key=lambda c: abs(math.log((B * T // c) / NA)))
    nt_h = T // tt
    NH = B * nt_h
    R = aheads * 2 * SUBLANES

    def hb(s):
        return jnp.minimum(s, NH - 1) // nt_h

    def ht(s):
        return jnp.minimum(s, NH - 1) % nt_h

    def ab(s):
        return jnp.minimum(s, NA - 1) // nj_a

    def aj(s):
        return jnp.minimum(s, NA - 1) % nj_a

    blk = lambda c: pl.BlockSpec((None, tt, W), lambda s, pt: (hb(s), ht(s), c))
    vec = pl.BlockSpec((1, W), lambda s, pt: (0, 0))
    sblk = pl.BlockSpec((None, heads, HEAD, HEAD), lambda s, pt: (hb(s), 0, 0, 0))
    tok_spec = pl.BlockSpec((None, SUBLANES, WA), lambda s, pt: (ab(s), 0, 0))
    lam_spec = pl.BlockSpec((1, QK_HALF), lambda s, pt: (0, 0))

    def page_spec(p):
        return pl.BlockSpec((None, None, aheads, page, HEAD),
                            lambda s, pt: (layer, pt[ab(s), aj(s) * pps + p], 0, 0, 0))

    grid_spec = pltpu.PrefetchScalarGridSpec(
        num_scalar_prefetch=1,
        grid=(max(NH, NA),),
        in_specs=[blk(0), blk(1), blk(2), blk(3), sblk, vec, vec,
                  tok_spec, tok_spec, tok_spec, lam_spec, lam_spec, lam_spec, lam_spec,
                  pl.BlockSpec((1, WA), lambda s, pt: (0, 0))]
                 + [page_spec(p) for p in range(pps)] + [page_spec(p) for p in range(pps)],
        out_specs=[blk(0), sblk, tok_spec],
        scratch_shapes=[pltpu.VMEM((heads, HEAD, HEAD), F32),
                        pltpu.VMEM((R, WA), F32),
                        pltpu.VMEM((R, LANES), F32),
                        pltpu.VMEM((R, LANES), F32),
                        pltpu.VMEM((R, WA), F32)],
    )
    return pl.pallas_call(
        functools.partial(_hgrn_sattn_kernel, n_pages_in=2 * pps,
                          hg_kw=dict(tt=tt, chunk=chunk, heads=heads, valid=None),
                          sa_kw=dict(heads=aheads, tnew=tnew, lam_init=lam_init, pps=pps),
                          NH=NH, NA=NA, nt_h=nt_h, nj_a=nj_a),
        grid_spec=grid_spec,
        out_shape=[jax.ShapeDtypeStruct((B, T, W), BF16), jax.ShapeDtypeStruct(s0.shape, F32),
                   jax.ShapeDtypeStruct((DB, SUBLANES, WA), BF16)],
        compiler_params=_params(("arbitrary",)),
        name="hgrn2_sample_attention",
    )(page_table, main, main, main, main, s0, lb, gn, qs8, kn8, vn8, *lams, subln,
      *([cache_k] * pps), *([cache_v] * pps))


def _oproj_kernel(a_ref, b_ref, c_ref, w_ref, x_ref, gpost_ref, gnext_ref, xo_ref, h_ref, *, wa, wb):
    tm = x_ref.shape[0]
    rg = min(tm, OPROJ_ROWS)
    ys = []
    for r in range(tm // rg):
        rows = slice(r * rg, (r + 1) * rg)
        y = jnp.dot(a_ref[rows, :], w_ref[0:wa, :], preferred_element_type=F32)
        y = y + jnp.dot(b_ref[rows, :], w_ref[wa:wa + wb, :], preferred_element_type=F32)
        ys.append(y + jnp.dot(c_ref[rows, :], w_ref[wa + wb:, :], preferred_element_type=F32))
    for r in range(tm // rg):
        rows = slice(r * rg, (r + 1) * rg)
        x = x_ref[rows, :] + _rms(ys[r], gpost_ref[...])
        xo_ref[rows, :] = x
        h_ref[rows, :] = _rms(x, gnext_ref[...]).astype(BF16)


def out_proj(a, b, c, w, layer, x, gpost, gnext, tm):
    M, D = x.shape
    wa, wb, wc = a.shape[1], b.shape[1], c.shape[1]
    row = lambda width: pl.BlockSpec((tm, width), lambda i: (i, 0))
    vec = pl.BlockSpec((1, D), lambda i: (0, 0))
    return pl.pallas_call(
        functools.partial(_oproj_kernel, wa=wa, wb=wb),
        grid=(M // tm,),
        in_specs=[row(wa), row(wb), row(wc), pl.BlockSpec((None, D, D), lambda i: (layer, 0, 0)), row(D), vec, vec],
        out_specs=[row(D), row(D)],
        out_shape=[jax.ShapeDtypeStruct((M, D), F32), jax.ShapeDtypeStruct((M, D), BF16)],
        compiler_params=_params(("parallel",)),
        name="out_proj",
    )(a, b, c, w, x, gpost, gnext)


def _ffn_up_seq_kernel(h_ref, wg_ref, wu_ref, cw_ref, cb_ref, prev_ref, act_ref, st_ref,
                       wgb, wub, buf, *, tm, groups):
    b = pl.program_id(1)
    t = pl.program_id(2)
    H = SUBLANES
    tn = buf.shape[1]
    gw = tn // groups

    @pl.when(t == 0)
    def _():
        @pl.when(b == 0)
        def _():
            wgb[...] = wg_ref[...].astype(BF16)
            wub[...] = wu_ref[...].astype(BF16)

        buf[0:H - 2, :] = jnp.zeros((H - 2, tn), F32)
        buf[H - 2:H, :] = prev_ref[...]

    rg = min(tm, FFN_UP_ROWS)
    blocks = [(r, g) for r in range(tm // rg) for g in range(groups)]
    prods = []
    for r, g in blocks:
        cols = slice(g * gw, (g + 1) * gw)
        h = h_ref[r * rg:(r + 1) * rg, :]
        prods.append((jnp.dot(h, wgb[:, cols], preferred_element_type=F32),
                      jnp.dot(h, wub[:, cols], preferred_element_type=F32)))
    for (r, g), (u, up) in zip(blocks, prods):
        cols = slice(g * gw, (g + 1) * gw)
        r0 = H + r * rg
        buf[r0:r0 + rg, cols] = u
        c = cb_ref[:, cols] + cw_ref[2:3, cols] * u
        c = c + cw_ref[1:2, cols] * buf[r0 - 1:r0 - 1 + rg, cols]
        c = c + cw_ref[0:1, cols] * buf[r0 - 2:r0 - 2 + rg, cols]
        act_ref[r * rg:(r + 1) * rg, cols] = (c * (1.0 / (1.0 + jnp.exp(-c))) * up).astype(BF16)
    tail = buf[tm:tm + H, :]
    st_ref[...] = tail[H - 2:, :]
    buf[0:H, :] = tail


def ffn_up_seq(h3, wg, wu, layer, cw, cb, prev, tn):
    B, T, D = h3.shape
    Fd = wg.shape[2]
    tm = min(T, 1024)
    groups = max(1, tn // MXU_COLS)
    wspec = pl.BlockSpec((None, D, tn), lambda j, b, t: (layer, 0, j))
    return pl.pallas_call(
        functools.partial(_ffn_up_seq_kernel, tm=tm, groups=groups),
        grid=(Fd // tn, B, T // tm),
        in_specs=[pl.BlockSpec((None, tm, D), lambda j, b, t: (b, t, 0)), wspec, wspec,
                  pl.BlockSpec((CONV_W, tn), lambda j, b, t: (0, j)),
                  pl.BlockSpec((1, tn), lambda j, b, t: (0, j)),
                  pl.BlockSpec((None, CONV_W - 1, tn), lambda j, b, t: (b, 0, j))],
        out_specs=[pl.BlockSpec((None, tm, tn), lambda j, b, t: (b, t, j)),
                   pl.BlockSpec((None, CONV_W - 1, tn), lambda j, b, t: (b, 0, j))],
        out_shape=[jax.ShapeDtypeStruct((B, T, Fd), BF16),
                   jax.ShapeDtypeStruct((B, CONV_W - 1, Fd), F32)],
        scratch_shapes=[pltpu.VMEM((D, tn), BF16), pltpu.VMEM((D, tn), BF16),
                        pltpu.VMEM((SUBLANES + tm, tn), F32)],
        compiler_params=_params(("arbitrary", "arbitrary", "arbitrary")),
        name="ffn_up_prompt",
    )(h3, wg, wu, cw, cb, prev)


def _ffn_up_short_kernel(h_ref, wg_ref, wu_ref, cw_ref, cb_ref, p1_ref, p2_ref, act_ref, u_ref, *, T):
    h = h_ref[...]
    u = jnp.dot(h, wg_ref[...].astype(BF16), preferred_element_type=F32)
    up = jnp.dot(h, wu_ref[...].astype(BF16), preferred_element_type=F32)
    u_ref[...] = u
    r = lax.broadcasted_iota(jnp.int32, u.shape, 0) & (T - 1)
    um1 = jnp.where(r >= 1, pltpu.roll(u, 1, axis=0), p1_ref[...])
    um2 = jnp.where(r >= 2, pltpu.roll(u, 2, axis=0), p2_ref[...])
    c = cb_ref[...] + cw_ref[2:3, :] * u + cw_ref[1:2, :] * um1 + cw_ref[0:1, :] * um2
    sg, _ = _sigmoid_pair(c)
    act_ref[...] = (c * sg * up).astype(BF16)


def ffn_up_short(h, wg, wu, layer, cw, cb, p1, p2, T, tn):
    M, D = h.shape
    Fd = wg.shape[2]
    wspec = pl.BlockSpec((None, D, tn), lambda j: (layer, 0, j))
    col = pl.BlockSpec((M, tn), lambda j: (0, j))
    return pl.pallas_call(
        functools.partial(_ffn_up_short_kernel, T=T),
        grid=(Fd // tn,),
        in_specs=[pl.BlockSpec((M, D), lambda j: (0, 0)), wspec, wspec,
                  pl.BlockSpec((CONV_W, tn), lambda j: (0, j)),
                  pl.BlockSpec((1, tn), lambda j: (0, j)), col, col],
        out_specs=[col, col],
        out_shape=[jax.ShapeDtypeStruct((M, Fd), BF16), jax.ShapeDtypeStruct((M, Fd), F32)],
        compiler_params=_params(("parallel",)),
        name="ffn_up_sample",
    )(h, wg, wu, cw, cb, p1, p2)


def _ffn_down_kernel(a_ref, w_ref, x_ref, gpost_ref, gnext_ref, xo_ref, h_ref):
    y = jnp.dot(a_ref[...], w_ref[...], preferred_element_type=F32)
    x = x_ref[...] + _rms(y, gpost_ref[...])
    xo_ref[...] = x
    h_ref[...] = _rms(x, gnext_ref[...]).astype(BF16)


def ffn_down(act, w, layer, x, gpost, gnext, tm):
    M, D = x.shape
    Fd = act.shape[1]
    row = pl.BlockSpec((tm, D), lambda i: (i, 0))
    vec = pl.BlockSpec((1, D), lambda i: (0, 0))
    return pl.pallas_call(
        _ffn_down_kernel,
        grid=(M // tm,),
        in_specs=[pl.BlockSpec((tm, Fd), lambda i: (i, 0)),
                  pl.BlockSpec((None, Fd, D), lambda i: (layer, 0, 0), pipeline_mode=pl.Buffered(1)),
                  row, vec, vec],
        out_specs=[row, row],
        out_shape=[jax.ShapeDtypeStruct((M, D), F32), jax.ShapeDtypeStruct((M, D), BF16)],
        compiler_params=_params(("arbitrary",)),
        name="ffn_down",
    )(act, w, x, gpost, gnext)


def _largest_tile(n, cap, quantum):
    best = quantum
    for t in range(quantum, cap + 1, quantum):
        if n % t == 0:
            best = t
    return best


def _project(h, l, lw, B, T):
    M = B * T
    tm = min(M, 1024)
    WA = lw['pool_scale'].shape[1]
    WH = lw['hgrn_norm'].shape[1]
    WM = lw['w_in'].shape[2] - WA
    xa = matmul(h, lw['w_in'], l, 0, WA, tm, WA, "proj_pool").reshape(B, T, WA)
    main = matmul(h, lw['w_in'], l, WA, WM, tm, WH, "proj_main").reshape(B, T, WM)
    return xa, main


def _finish(x, a_out, o_b, o_c, l, lw, B, T, prev_conv, short):
    M, D = x.shape
    WA, WH, WC = a_out.shape[-1], o_b.shape[-1], o_c.shape[-1]
    x, h2 = out_proj(a_out.reshape(M, WA), o_b.reshape(M, WH), o_c.reshape(M, WC), lw['w_out'], l,
                     x, lw['norm_mix_post'], lw['norm_ffn_pre'], min(M, 2 * OPROJ_ROWS))
    Fd = lw['w_gate'].shape[2]
    tn = _largest_tile(Fd, 512, LANES)
    if not short:
        act, conv_new = ffn_up_seq(h2.reshape(B, T, D), lw['w_gate'], lw['w_up'], l, lw['conv_w'],
                                   lw['conv_b'], prev_conv, tn)
        act = act.reshape(M, Fd)
    else:
        z = jnp.zeros((B, T, Fd), F32)
        p1 = z.at[:, 0].set(prev_conv[:, 1]).reshape(M, Fd)
        p2 = z.at[:, 0].set(prev_conv[:, 0]).at[:, 1].set(prev_conv[:, 1]).reshape(M, Fd)
        act, u = ffn_up_short(h2, lw['w_gate'], lw['w_up'], l, lw['conv_w'], lw['conv_b'], p1, p2, T, tn)
        conv_new = u.reshape(B, T, Fd)[:, T - (CONV_W - 1):]
    x, h_next = ffn_down(act, lw['w_down'], l, x, lw['norm_ffn_post'], lw['norm_next'], min(M, FFN_DOWN_ROWS))
    return x, h_next, conv_new


def _layer(l, lw, xp, hp, xs, hs, *, B, T, DB, TS, P, zero_states, states, paged, kv_stacks):
    WC = lw['attn_subln'].shape[1]
    lam_init = 0.8 - 0.6 * math.exp(-0.3 * l)
    lams = (lw['lam_q1'], lw['lam_k1'], lw['lam_q2'], lw['lam_k2'])
    zp_pool, zp_S, zp_conv = zero_states
    sp_pool, sp_S, sp_conv = states
    cache_k, cache_v, page_table = paged

    xa_p, main_p = _project(hp, l, lw, B, T)
    xa_s, main_s = _project(hs, l, lw, DB, TS)
    a_p, pool_p = pool_mixer(xa_p, zp_pool, lw['pool_w'], lw['pool_scale'], 0)
    a_s, pool_s = pool_mixer(xa_s, sp_pool, lw['pool_w'], lw['pool_scale'], P)

    pad = ((0, 0), (0, SUBLANES - TS), (0, 0))
    main8 = jnp.pad(main_s, pad)
    ob8, S_s = hgrn2(main8, sp_S, lw['hgrn_lb'], lw['hgrn_norm'], SUBLANES, TS)
    qs8, k_rot8, _ = rope(main8, 4, 5, WC, lw['rope_tabs'], P)
    v_s = main_s[:, :, 6 * WC:]

    qs, kb, vb, k_stack, v_stack = rope_heads(main_p, 4, WC, lw['rope_tabs'], 0, *kv_stacks, l)
    ob_p, S_p, oc8 = hgrn2_with_sample_attention(
        main_p, zp_S, lw['hgrn_lb'], lw['hgrn_norm'], min(HG_CHUNK, T),
        qs8.astype(F32), k_rot8, jnp.pad(v_s, pad), cache_k, cache_v, l, page_table,
        lams, lw['attn_subln'], lam_init, TS)
    oc_p = prompt_attention(qs, kb, vb, lams, lw['attn_subln'], lam_init)

    xp, hp, conv_p = _finish(xp, a_p, ob_p, oc_p, l, lw, B, T, zp_conv, False)
    xs, hs, conv_s = _finish(xs, a_s, ob8[:, :TS], oc8[:, :TS], l, lw, DB, TS, sp_conv, True)
    return (xp, hp, xs, hs, (k_stack, v_stack), (pool_p, S_p, conv_p),
            (k_rot8[:, :TS], v_s, pool_s, S_s, conv_s))


def kernel(x_prompt, x_sample, cache_k, cache_v, state_pool, state_hgrn, state_conv, page_table, w_in, w_out, norm_mix_pre, norm_mix_post, norm_ffn_pre, norm_ffn_post, pool_w, pool_scale, hgrn_lower_bounds, hgrn_norm, attn_lam_q1, attn_lam_k1, attn_lam_q2, attn_lam_k2, attn_subln, ffn_w_gate, ffn_w_up, ffn_conv_w, ffn_conv_b, ffn_w_down):
    depth = w_in.shape[0]
    B, T, D = x_prompt.shape
    DB, TS, _ = x_sample.shape
    page = cache_k.shape[2]
    P = page_table.shape[1] * page
    WA = pool_scale.shape[1]
    WH = hgrn_norm.shape[1]
    WC = attn_subln.shape[1]
    HH = WH // HEAD
    Fd = ffn_w_gate.shape[2]
    assert TS >= CONV_W - 1 and TS <= SUBLANES and (TS & (TS - 1)) == 0

    sm = jax.nn.softmax(hgrn_lower_bounds.astype(F32), axis=0)
    lbs = jnp.cumsum(sm, axis=0) - sm[0]

    half = ROT_DIM // 2
    freqs = jnp.power(ROPE_THETA, -jnp.arange(0, ROT_DIM, 2, dtype=F32) / ROT_DIM)
    lane = np.arange(LANES)
    in_rot = (lane % QK_HALF) < ROT_DIM
    fl = jnp.where(jnp.asarray(in_rot), jnp.tile(freqs, LANES // half), 0.0).reshape(1, LANES)
    s1 = jnp.asarray(np.where(in_rot & ((lane % QK_HALF) < half), -1.0, 0.0), F32).reshape(1, LANES)
    s2 = jnp.asarray(np.where(in_rot & ((lane % QK_HALF) >= half), 1.0, 0.0), F32).reshape(1, LANES)

    ones = jnp.ones((1, D), F32)
    HC = WC // HEAD
    ck = jnp.swapaxes(cache_k, 2, 3)
    cv = jnp.swapaxes(cache_v, 2, 3)
    kv = (jnp.zeros((depth, B, HC, T, HEAD), F32), jnp.zeros((depth, B, HC, T, HEAD), F32))

    w_out_b = w_out.astype(BF16)
    w_down_b = ffn_w_down.astype(BF16)

    def layer_weights(l):
        return {
            'w_in': w_in,
            'w_out': w_out_b,
            'w_gate': ffn_w_gate,
            'w_up': ffn_w_up,
            'w_down': w_down_b,
            'pool_w': pool_w[l],
            'pool_scale': pool_scale[l].reshape(1, WA),
            'hgrn_lb': lbs[l].reshape(1, WH),
            'hgrn_norm': hgrn_norm[l].reshape(1, WH),
            'attn_subln': attn_subln[l].reshape(1, WC),
            'lam_q1': attn_lam_q1[l].reshape(1, QK_HALF),
            'lam_k1': attn_lam_k1[l].reshape(1, QK_HALF),
            'lam_q2': attn_lam_q2[l].reshape(1, QK_HALF),
            'lam_k2': attn_lam_k2[l].reshape(1, QK_HALF),
            'norm_mix_post': norm_mix_post[l].reshape(1, D),
            'norm_ffn_pre': norm_ffn_pre[l].reshape(1, D),
            'norm_ffn_post': norm_ffn_post[l].reshape(1, D),
            'norm_next': norm_mix_pre[l + 1].reshape(1, D) if l + 1 < depth else ones,
            'conv_w': ffn_conv_w[l],
            'conv_b': ffn_conv_b[l].reshape(1, Fd),
            'rope_tabs': (fl, s1, s2),
        }

    xp = x_prompt.reshape(B * T, D)
    xs = x_sample.reshape(DB * TS, D)
    g0 = norm_mix_pre[0].reshape(1, D)
    hp = rmsnorm_bf16(xp, g0, min(B * T, 512))
    hs = rmsnorm_bf16(xs, g0, DB * TS)
    zp_pool = jnp.zeros((B, POOL_STATE, WA), F32)
    zp_S = jnp.zeros((B, HH, HEAD, HEAD), F32)
    zp_conv = jnp.zeros((B, CONV_W - 1, Fd), F32)

    st_p, st_s = [], []
    for l in range(depth):
        xp, hp, xs, hs, kv, sp, ss = _layer(
            l, layer_weights(l), xp, hp, xs, hs, B=B, T=T, DB=DB, TS=TS, P=P,
            zero_states=(zp_pool, zp_S, zp_conv), states=(state_pool[l], state_hgrn[l], state_conv[l]),
            paged=(ck, cv, page_table), kv_stacks=kv)
        st_p.append(sp)
        st_s.append(ss)

    def stack(sts, i, shape):
        return jnp.stack([s[i] for s in sts]).reshape(shape)

    return (xp.reshape(B, T, D), xs.reshape(DB, TS, D),
            jnp.swapaxes(kv[0], 2, 3), jnp.swapaxes(kv[1], 2, 3),
            stack(st_p, 0, (depth, B, POOL_STATE, WA)), stack(st_p, 1, (depth, B, HH, HEAD, HEAD)),
            stack(st_p, 2, (depth, B, CONV_W - 1, Fd)),
            stack(st_s, 0, (depth, DB, TS, HC, HEAD)), stack(st_s, 1, (depth, DB, TS, HC, HEAD)),
            stack(st_s, 2, (depth, DB, POOL_STATE, WA)), stack(st_s, 3, (depth, DB, HH, HEAD, HEAD)),
            stack(st_s, 4, (depth, DB, CONV_W - 1, Fd)))
```

```python
import functools
import math

import numpy as np
import jax
import jax.numpy as jnp
from jax import lax
from jax.experimental import pallas as pl
from jax.experimental.pallas import tpu as pltpu

F32 = jnp.float32
BF16 = jnp.bfloat16

LANES = 128
SUBLANES = 8
MXU_COLS = 256
VMEM_LIMIT = 48 * 1024 * 1024

POOL_WINDOWS = (2, 4, 8, 16)
POOL_STATE = max(POOL_WINDOWS) - 1
HEAD = 128
QK_HALF = 64
QK_SCALE_LOG2E = QK_HALF ** -0.5 * math.log2(math.e)
ROT_DIM = 16
ROPE_THETA = 500000.0
LB_FLOOR = 1e-30
MASK_VALUE = -1e30
CONV_W = 3
EPS = 1e-6
HG_CHUNK = 64
HG_DIAG = 8
PHASES = ("init", "main", "final")
HG_UNROLL = 2
HG_HEADS_PER_STEP = 6
PAGES_PER_STEP = 16
OPROJ_ROWS = 256
FFN_UP_ROWS = 256
FFN_DOWN_ROWS = 256


def _params(sem):
    return pltpu.CompilerParams(dimension_semantics=sem, vmem_limit_bytes=VMEM_LIMIT)


def _rms(x, g):
    return x * lax.rsqrt(jnp.mean(x * x, axis=-1, keepdims=True) + EPS) * g


def _sigmoid_pair(z):
    e = jnp.exp(-jnp.abs(z))
    r = 1.0 / (1.0 + e)
    er = e * r
    pos = z >= 0
    return jnp.where(pos, r, er), jnp.where(pos, er, r)


def _norm_kernel(x_ref, g_ref, h_ref):
    h_ref[...] = _rms(x_ref[...], g_ref[...]).astype(BF16)


def rmsnorm_bf16(x, g, tm):
    M, D = x.shape
    return pl.pallas_call(
        _norm_kernel,
        grid=(M // tm,),
        in_specs=[pl.BlockSpec((tm, D), lambda i: (i, 0)),
                  pl.BlockSpec((1, D), lambda i: (0, 0))],
        out_specs=pl.BlockSpec((tm, D), lambda i: (i, 0)),
        out_shape=jax.ShapeDtypeStruct((M, D), BF16),
        compiler_params=_params(("parallel",)),
        name="rmsnorm",
    )(x, g)


def _mm_kernel(x_ref, *rest, nsub):
    w_refs, o_ref, wb = rest[:nsub], rest[nsub], rest[nsub + 1]

    @pl.when(pl.program_id(1) == 0)
    def _():
        for c in range(nsub):
            wb[:, c * MXU_COLS:(c + 1) * MXU_COLS] = w_refs[c][...].astype(BF16)

    o_ref[...] = jnp.dot(x_ref[...], wb[...], preferred_element_type=F32)


def matmul(x, w, layer, col0, N, tm, tn, name):
    M, K = x.shape
    nsub = tn // MXU_COLS
    assert col0 % MXU_COLS == 0 and tn % MXU_COLS == 0 and N % tn == 0
    wspec = lambda c: pl.BlockSpec((None, K, MXU_COLS),
                                   lambda j, i: (layer, 0, col0 // MXU_COLS + j * nsub + c))
    return pl.pallas_call(
        functools.partial(_mm_kernel, nsub=nsub),
        grid=(N // tn, M // tm),
        in_specs=[pl.BlockSpec((tm, K), lambda j, i: (i, 0))] + [wspec(c) for c in range(nsub)],
        out_specs=pl.BlockSpec((tm, tn), lambda j, i: (i, j)),
        out_shape=jax.ShapeDtypeStruct((M, N), F32),
        scratch_shapes=[pltpu.VMEM((K, tn), BF16)],
        compiler_params=_params(("arbitrary", "arbitrary")),
        name=name,
    )(x, *([w] * nsub))


def _pool_kernel(xa_ref, prev_ref, wp_ref, sc_ref, y_ref, st_ref, buf, *, tt, pos0):
    t = pl.program_id(1)
    P = POOL_STATE + 1

    @pl.when(t == 0)
    def _():
        buf[0:1, :] = jnp.zeros((1, buf.shape[1]), F32)
        buf[1:P, :] = prev_ref[...]

    buf[P:P + tt, :] = xa_ref[...]
    pos = pos0 + t * tt + lax.broadcasted_iota(jnp.int32, (tt, LANES), 0)
    for g, w in enumerate(POOL_WINDOWS):
        cols = slice(g * LANES, (g + 1) * LANES)
        x = buf[P:P + tt, cols]
        acc = x
        for j in range(1, w):
            acc = acc + buf[P - j:P - j + tt, cols]
        cnt = jnp.minimum(w, pos + 1).astype(F32)
        d = acc / cnt - x
        y = jnp.dot(d.astype(BF16), wp_ref[g].astype(BF16), preferred_element_type=F32)
        y_ref[:, cols] = (y * sc_ref[:, cols]).astype(BF16)
    tail = buf[tt:tt + P, :]
    st_ref[...] = tail[1:, :]
    buf[0:P, :] = tail


def pool_mixer(xa, prev, wp, scale, pos0):
    B, T, W = xa.shape
    tt = min(T, 512)
    return pl.pallas_call(
        functools.partial(_pool_kernel, tt=tt, pos0=pos0),
        grid=(B, T // tt),
        in_specs=[pl.BlockSpec((None, tt, W), lambda b, t: (b, t, 0)),
                  pl.BlockSpec((None, POOL_STATE, W), lambda b, t: (b, 0, 0)),
                  pl.BlockSpec((len(POOL_WINDOWS), LANES, LANES), lambda b, t: (0, 0, 0)),
                  pl.BlockSpec((1, W), lambda b, t: (0, 0))],
        out_specs=[pl.BlockSpec((None, tt, W), lambda b, t: (b, t, 0)),
                   pl.BlockSpec((None, POOL_STATE, W), lambda b, t: (b, 0, 0))],
        out_shape=[jax.ShapeDtypeStruct((B, T, W), BF16),
                   jax.ShapeDtypeStruct((B, POOL_STATE, W), F32)],
        scratch_shapes=[pltpu.VMEM((POOL_STATE + 1 + tt, W), F32)],
        compiler_params=_params(("parallel", "arbitrary")),
        name="pool_mixer",
    )(xa, prev, wp, scale)


def _rope_kernel(q_ref, k_ref, fl_ref, s1_ref, s2_ref, qs_ref, kr_ref, kb_ref, *, tt, pos0, heads):
    t = pl.program_id(1)
    pos = (pos0 + t * tt + lax.broadcasted_iota(jnp.int32, (tt, LANES), 0)).astype(F32)
    ang = pos * fl_ref[...]
    c = jnp.cos(ang)
    s = jnp.sin(ang)
    sa = s * s1_ref[...]
    sb = s * s2_ref[...]
    half = ROT_DIM // 2
    for h in range(heads):
        cols = slice(h * HEAD, (h + 1) * HEAD)
        xq = q_ref[:, cols]
        yq = xq * c + pltpu.roll(xq, LANES - half, axis=1) * sa + pltpu.roll(xq, half, axis=1) * sb
        qs_ref[:, cols] = (yq * (QK_HALF ** -0.5)).astype(BF16)
        xk = k_ref[:, cols]
        yk = xk * c + pltpu.roll(xk, LANES - half, axis=1) * sa + pltpu.roll(xk, half, axis=1) * sb
        kr_ref[:, cols] = yk
        kb_ref[:, cols] = yk.astype(BF16)


def rope(main, qblk, kblk, width, tabs, pos0):
    B, T, _ = main.shape
    tt = min(T, 512)
    heads = width // HEAD
    tab = pl.BlockSpec((1, LANES), lambda b, t: (0, 0))
    blk = lambda c: pl.BlockSpec((None, tt, width), lambda b, t: (b, t, c))
    return pl.pallas_call(
        functools.partial(_rope_kernel, tt=tt, pos0=pos0, heads=heads),
        grid=(B, T // tt),
        in_specs=[blk(qblk), blk(kblk), tab, tab, tab],
        out_specs=[blk(0), blk(0), blk(0)],
        out_shape=[jax.ShapeDtypeStruct((B, T, width), BF16),
                   jax.ShapeDtypeStruct((B, T, width), F32),
                   jax.ShapeDtypeStruct((B, T, width), BF16)],
        compiler_params=_params(("parallel", "parallel")),
        name="rope",
    )(main, main, *tabs)


def _rope_heads_kernel(q_ref, k_ref, v_ref, fl_ref, s1_ref, s2_ref, kin_ref, vin_ref,
                       qs_ref, kb_ref, vb_ref, ko_ref, vo_ref, *, tt, pos0, heads):
    del kin_ref, vin_ref
    t = pl.program_id(1)
    pos = (pos0 + t * tt + lax.broadcasted_iota(jnp.int32, (tt, LANES), 0)).astype(F32)
    ang = pos * fl_ref[...]
    c = jnp.cos(ang)
    s = jnp.sin(ang)
    sa = s * s1_ref[...]
    sb = s * s2_ref[...]
    half = ROT_DIM // 2
    for h in range(heads):
        cols = slice(h * HEAD, (h + 1) * HEAD)
        xq = q_ref[:, cols]
        yq = xq * c + pltpu.roll(xq, LANES - half, axis=1) * sa + pltpu.roll(xq, half, axis=1) * sb
        qs_ref[h] = (yq * QK_SCALE_LOG2E).astype(BF16)
        xk = k_ref[:, cols]
        yk = xk * c + pltpu.roll(xk, LANES - half, axis=1) * sa + pltpu.roll(xk, half, axis=1) * sb
        ko_ref[h] = yk
        kb_ref[h] = yk.astype(BF16)
        xv = v_ref[:, cols]
        vo_ref[h] = xv
        vb_ref[h, :, 0:HEAD] = xv.astype(BF16)
        vb_ref[h, :, HEAD:2 * HEAD] = jnp.ones((tt, HEAD), BF16)


def rope_heads(main, qblk, width, tabs, pos0, k_stack, v_stack, layer):
    B, T, _ = main.shape
    tt = min(T, 512)
    heads = width // HEAD
    tab = pl.BlockSpec((1, LANES), lambda b, t: (0, 0))
    blk = lambda c: pl.BlockSpec((None, tt, width), lambda b, t: (b, t, c))
    hm = pl.BlockSpec((None, heads, tt, HEAD), lambda b, t: (b, 0, t, 0))
    st = pl.BlockSpec((None, None, heads, tt, HEAD), lambda b, t: (layer, b, 0, t, 0))
    anyspec = pl.BlockSpec(memory_space=pl.ANY)
    hshape = jax.ShapeDtypeStruct((B, heads, T, HEAD), BF16)
    vshape = jax.ShapeDtypeStruct((B, heads, T, 2 * HEAD), BF16)
    vm = pl.BlockSpec((None, heads, tt, 2 * HEAD), lambda b, t: (b, 0, t, 0))
    return pl.pallas_call(
        functools.partial(_rope_heads_kernel, tt=tt, pos0=pos0, heads=heads),
        grid=(B, T // tt),
        in_specs=[blk(qblk), blk(qblk + 1), blk(qblk + 2), tab, tab, tab, anyspec, anyspec],
        out_specs=[hm, hm, vm, st, st],
        out_shape=[hshape, hshape, vshape,
                   jax.ShapeDtypeStruct(k_stack.shape, F32), jax.ShapeDtypeStruct(v_stack.shape, F32)],
        input_output_aliases={6: 3, 7: 4},
        compiler_params=_params(("parallel", "parallel")),
        name="rope_heads",
    )(main, main, main, *tabs, k_stack, v_stack)


def _hgrn_kernel(q_ref, f_ref, i_ref, g_ref, s0_ref, lb_ref, gn_ref, o_ref, so_ref, st_ref,
                 *, tt, chunk, heads, valid, t=None, nt=None, phases=PHASES):
    if t is None:
        t, nt = pl.program_id(2), pl.num_programs(2)
    C = chunk
    levels = [n for n in (64, 32, 16) if n <= C]

    def load_state():
        for h in range(heads):
            st_ref[h] = s0_ref[h].T

    def store_state():
        for h in range(heads):
            so_ref[h] = st_ref[h].T

    if "init" in phases:
        pl.when(t == 0)(load_state)
    if "main" not in phases:
        if "final" in phases:
            pl.when(t == nt - 1)(store_state)
        return

    row = lax.broadcasted_iota(jnp.int32, (C, LANES), 0)
    rr = lax.broadcasted_iota(jnp.int32, (C, C), 0)
    cc = lax.broadcasted_iota(jnp.int32, (C, C), 1)
    lvl_masks = []
    for n in levels:
        sh = int(math.log2(n))
        same = (rr >> sh) == (cc >> sh)
        up = (rr & (n - 1)) >= n // 2
        lo = (cc & (n - 1)) < n // 2
        lvl_masks.append(jnp.where(same, jnp.where(up, jnp.where(lo, 1.0, 0.0), 0.0), 0.0))

    def chunk_body(c, carry):
        r0 = pl.multiple_of(c * C, C)
        rows = pl.ds(r0, C)
        for h in range(heads):
            cols = slice(h * HEAD, (h + 1) * HEAD)
            q = q_ref[rows, cols]
            z = f_ref[rows, cols]
            v = i_ref[rows, cols]
            gate = g_ref[rows, cols]
            lb = lb_ref[:, cols]
            lbm = jnp.maximum(lb, LB_FLOOR)
            oml = 1.0 - lb
            sg, sgn = _sigmoid_pair(z)
            f = lbm + oml * sg
            kin = oml * sgn - (lbm - lb)
            if valid is not None:
                ok = (r0 + row) < valid
                f = jnp.where(ok, f, 1.0)
                kin = jnp.where(ok, kin, 0.0)
            lf = jnp.log(f)
            b = lf
            s = 1
            while s < C:
                b = b + jnp.where(row >= s, pltpu.roll(b, s, axis=0), 0.0)
                s *= 2
            St = st_ref[h]
            qd = q * jnp.exp(b)
            o = lax.dot_general(qd.astype(BF16), St.astype(BF16), (((1,), (1,)), ((), ())),
                                preferred_element_type=F32)
            if levels:
                att = jnp.zeros((C, C), F32)
                for n, msk in zip(levels, lvl_masks):
                    pieces = []
                    for m in range(C // n):
                        rb = m * n + n // 2 - 1
                        pieces.append(jnp.broadcast_to(b[rb:rb + 1, :], (n, LANES)))
                    R = pieces[0] if len(pieces) == 1 else jnp.concatenate(pieces, axis=0)
                    qn = q * jnp.exp(jnp.minimum(b - R, 0.0))
                    kn = kin * jnp.exp(jnp.minimum(R - b, 0.0))
                    a = lax.dot_general(qn.astype(BF16), kn.astype(BF16), (((1,), (1,)), ((), ())),
                                        preferred_element_type=F32)
                    att = att + a * msk
                o = o + jnp.dot(att.astype(BF16), v.astype(BF16), preferred_element_type=F32)
            def back(x, j):
                return pltpu.roll(x.reshape(C // HG_DIAG, HG_DIAG, LANES), j, axis=1).reshape(C, LANES)

            rin = row & (HG_DIAG - 1)
            o = o + jnp.sum(q * kin, axis=-1, keepdims=True) * v
            e = f
            for j in range(1, HG_DIAG):
                if j > 1:
                    e = e * back(f, j - 1)
                kj = jnp.where(rin >= j, back(kin, j), 0.0)
                cj = jnp.sum(q * kj * e, axis=-1, keepdims=True)
                o = o + cj * back(v, j)
            bl = b[C - 1:C, :]
            kd = kin * jnp.exp(bl - b)
            upd = jnp.dot(v.T.astype(BF16), kd.astype(BF16), preferred_element_type=F32)
            st_ref[h] = St * jnp.exp(bl) + upd
            o_ref[rows, cols] = (_rms(o, gn_ref[:, cols]) * (gate / (1.0 + jnp.exp(-gate)))).astype(BF16)
        return carry

    if tt // C <= HG_UNROLL:
        for c in range(tt // C):
            chunk_body(c, 0)
    else:
        lax.fori_loop(0, tt // C, chunk_body, 0)

    if "final" in phases:
        pl.when(t == nt - 1)(store_state)


def hgrn2(main, s0, lb, gn, chunk, valid):
    B, T, _ = main.shape
    heads = s0.shape[1]
    hp = math.gcd(HG_HEADS_PER_STEP, heads)
    G = heads // hp
    W = heads * HEAD
    tt = min(T, 256)
    blk = lambda c: pl.BlockSpec((None, tt, hp * HEAD), lambda b, g, t: (b, t, c * G + g))
    vec = pl.BlockSpec((1, hp * HEAD), lambda b, g, t: (0, g))
    sblk = pl.BlockSpec((None, hp, HEAD, HEAD), lambda b, g, t: (b, g, 0, 0))
    return pl.pallas_call(
        functools.partial(_hgrn_kernel, tt=tt, chunk=chunk, heads=hp, valid=valid),
        grid=(B, G, T // tt),
        in_specs=[blk(0), blk(1), blk(2), blk(3), sblk, vec, vec],
        out_specs=[blk(0), sblk],
        out_shape=[jax.ShapeDtypeStruct((B, T, W), BF16),
                   jax.ShapeDtypeStruct(s0.shape, F32)],
        scratch_shapes=[pltpu.VMEM((hp, HEAD, HEAD), F32)],
        compiler_params=_params(("parallel", "parallel", "arbitrary")),
        name="hgrn2",
    )(main, main, main, main, s0, lb, gn)


def _lambda(lq1, lk1, lq2, lk2, lam_init):
    a = jnp.exp(jnp.sum(lq1 * lk1, axis=-1, keepdims=True))
    b = jnp.exp(jnp.sum(lq2 * lk2, axis=-1, keepdims=True))
    return a - b + lam_init


def _pattn_kernel(qi_ref, kj_ref, q_ref, kc_ref, kn_ref, v_ref, lq1, lk1, lq2, lk2, sub_ref, o_ref,
                  qst, m_ref, acc_ref, s_even, s_odd, *, tq, rs, lam_init):
    n = pl.program_id(2)
    qi = qi_ref[n]
    kj = kj_ref[n]

    def scores(k_ref):
        return lax.dot_general(qst[...], k_ref[...], (((1,), (1,)), ((), ())), preferred_element_type=F32)

    @pl.when(kj == 0)
    def _():
        q = q_ref[...]
        lane = lax.broadcasted_iota(jnp.int32, q.shape, 1)
        zero = jnp.zeros_like(q)
        qst[0:tq, :] = jnp.where(lane < QK_HALF, q, zero)
        qst[tq:2 * tq, :] = jnp.where(lane >= QK_HALF, q, zero)
        m_ref[...] = jnp.full(m_ref.shape, MASK_VALUE, F32)
        acc_ref[...] = jnp.zeros(acc_ref.shape, F32)
        s_even[...] = scores(kc_ref)

    def consume(s_ref, diag):
        for c in range(2 * tq // rs):
            rows = slice(c * rs, (c + 1) * rs)
            q0 = (c * rs) % tq
            kw = q0 + rs if diag else tq
            s = s_ref[rows, 0:kw]
            if diag:
                rq = lax.broadcasted_iota(jnp.int32, s.shape, 0) + q0
                ck = lax.broadcasted_iota(jnp.int32, s.shape, 1)
                s = jnp.where(ck <= rq, s, MASK_VALUE)
            m_prev = m_ref[rows, :]
            m_next = jnp.maximum(m_prev, jnp.max(s, axis=1, keepdims=True))
            p = jnp.exp2(s - m_next[:, 0:1])
            alpha = jnp.exp2(m_prev - m_next)
            pv = jnp.dot(p.astype(BF16), v_ref[0:kw, :], preferred_element_type=F32)
            acc_ref[rows, 0:HEAD] = alpha * acc_ref[rows, 0:HEAD] + pv[:, 0:HEAD]
            acc_ref[rows, HEAD:2 * HEAD] = alpha * acc_ref[rows, HEAD:2 * HEAD] + pv[:, HEAD:2 * HEAD]
            m_ref[rows, :] = m_next

    for parity, cur, nxt in ((0, s_even, s_odd), (1, s_odd, s_even)):
        mine = (kj & 1) == parity

        @pl.when(jnp.logical_and(kj < qi, mine))
        def _():
            nxt[...] = scores(kn_ref)
            consume(cur, False)

        @pl.when(jnp.logical_and(kj == qi, mine))
        def _():
            consume(cur, True)
            lam = _lambda(lq1[...], lk1[...], lq2[...], lk2[...], lam_init)
            o1 = acc_ref[0:tq, 0:HEAD] / acc_ref[0:tq, HEAD:2 * HEAD]
            o2 = acc_ref[tq:2 * tq, 0:HEAD] / acc_ref[tq:2 * tq, HEAD:2 * HEAD]
            o = o1 - lam * o2
            o_ref[...] = (_rms(o, sub_ref[...]) * (1.0 - lam_init)).astype(BF16)


def prompt_attention(qs, kb, vb, lams, subln, lam_init):
    B, heads, T, _ = qs.shape
    tq = min(T, 1024)
    rs = min(tq, 256)
    nq = T // tq
    pairs = [(i, j) for i in range(nq) for j in range(i + 1)]
    qi = jnp.asarray(np.array([p[0] for p in pairs], np.int32))
    kj = jnp.asarray(np.array([p[1] for p in pairs], np.int32))
    lam_spec = pl.BlockSpec((1, QK_HALF), lambda b, h, n, qi, kj: (0, 0))
    npairs = len(pairs)
    kc_spec = pl.BlockSpec((None, None, tq, HEAD), lambda b, h, n, qi, kj: (b, h, kj[n], 0))
    kn_spec = pl.BlockSpec((None, None, tq, HEAD),
                           lambda b, h, n, qi, kj: (b, h, kj[jnp.minimum(n + 1, npairs - 1)], 0))
    grid_spec = pltpu.PrefetchScalarGridSpec(
        num_scalar_prefetch=2,
        grid=(B, heads, len(pairs)),
        in_specs=[pl.BlockSpec((None, None, tq, HEAD), lambda b, h, n, qi, kj: (b, h, qi[n], 0)),
                  kc_spec, kn_spec,
                  pl.BlockSpec((None, None, tq, 2 * HEAD), lambda b, h, n, qi, kj: (b, h, kj[n], 0)),
                  lam_spec, lam_spec, lam_spec, lam_spec,
                  pl.BlockSpec((1, HEAD), lambda b, h, n, qi, kj: (0, h))],
        out_specs=pl.BlockSpec((None, tq, HEAD), lambda b, h, n, qi, kj: (b, qi[n], h)),
        scratch_shapes=[pltpu.VMEM((2 * tq, HEAD), BF16),
                        pltpu.VMEM((2 * tq, HEAD), F32),
                        pltpu.VMEM((2 * tq, 2 * HEAD), F32),
                        pltpu.VMEM((2 * tq, tq), F32),
                        pltpu.VMEM((2 * tq, tq), F32)],
    )
    return pl.pallas_call(
        functools.partial(_pattn_kernel, tq=tq, rs=rs, lam_init=lam_init),
        grid_spec=grid_spec,
        out_shape=jax.ShapeDtypeStruct((B, T, heads * HEAD), BF16),
        compiler_params=_params(("parallel", "parallel", "arbitrary")),
        name="prompt_attention",
    )(qi, kj, qs, kb, kb, vb, *lams, subln)


def _sattn_kernel(pt_ref, q_ref, kn_ref, vn_ref, lq1, lk1, lq2, lk2, sub_ref, *rest,
                  heads, tnew, lam_init, pps, j=None, nj=None, phases=PHASES):
    k_refs = rest[:pps]
    v_refs = rest[pps:2 * pps]
    o_ref = rest[2 * pps]
    qbd, m_ref, l_ref, acc_ref = rest[2 * pps + 1:]
    if j is None:
        j, nj = pl.program_id(1), pl.num_programs(1)
    R = heads * 2 * SUBLANES
    W = heads * HEAD

    def setup():
        q = q_ref[...]
        qrep = jnp.concatenate([q] * (heads * 2), axis=0)
        lane = lax.broadcasted_iota(jnp.int32, (R, W), 1)
        rowi = lax.broadcasted_iota(jnp.int32, (R, W), 0)
        qbd[...] = jnp.where((lane >> 6) == (rowi >> 3), qrep, 0.0)
        m_ref[...] = jnp.full(m_ref.shape, MASK_VALUE, F32)
        l_ref[...] = jnp.zeros(l_ref.shape, F32)
        acc_ref[...] = jnp.zeros(acc_ref.shape, F32)

    def page_rows(ref):
        return jnp.concatenate([ref[h] for h in range(heads)], axis=1).astype(BF16)

    def sweep():
        qb = qbd[...].astype(BF16)
        k_all = jnp.concatenate([page_rows(r) for r in k_refs], axis=0)
        s = lax.dot_general(qb, k_all, (((1,), (1,)), ((), ())), preferred_element_type=F32)
        m_prev = m_ref[...]
        m_next = jnp.maximum(m_prev, s.max(axis=1, keepdims=True))
        alpha = jnp.exp(m_prev - m_next)
        pr = jnp.exp(s - m_next[:, 0:1])
        v_all = jnp.concatenate([page_rows(r) for r in v_refs], axis=0)
        m_ref[...] = m_next
        l_ref[...] = alpha * l_ref[...] + jnp.sum(pr, axis=1, keepdims=True)
        acc_ref[...] = alpha[:, 0:1] * acc_ref[...] + jnp.dot(pr.astype(BF16), v_all,
                                                              preferred_element_type=F32)

    def finish():
        qf = qbd[...]
        tok = lax.broadcasted_iota(jnp.int32, (R, LANES), 0) & (SUBLANES - 1)
        m_p = m_ref[...]
        sn = []
        for jn in range(tnew):
            sj = jnp.sum(qf * kn_ref[jn:jn + 1, :], axis=1, keepdims=True)
            sn.append(jnp.where(tok >= jn, sj, MASK_VALUE))
        m_n = m_p
        for sj in sn:
            m_n = jnp.maximum(m_n, sj)
        al = jnp.exp(m_p - m_n)
        l_f = al * l_ref[...]
        acc_f = al[:, 0:1] * acc_ref[...]
        for jn in range(tnew):
            pj = jnp.exp(sn[jn] - m_n)
            l_f = l_f + pj
            acc_f = acc_f + pj[:, 0:1] * vn_ref[jn:jn + 1, :]
        lam = _lambda(lq1[...], lk1[...], lq2[...], lk2[...], lam_init)
        for h in range(heads):
            cols = slice(h * HEAD, (h + 1) * HEAD)
            r1 = slice(h * 2 * SUBLANES, h * 2 * SUBLANES + SUBLANES)
            r2 = slice(h * 2 * SUBLANES + SUBLANES, (h + 1) * 2 * SUBLANES)
            o1 = acc_f[r1, cols] / l_f[r1, :]
            o2 = acc_f[r2, cols] / l_f[r2, :]
            o = o1 - lam * o2
            o_ref[:, cols] = (_rms(o, sub_ref[:, cols]) * (1.0 - lam_init)).astype(BF16)

    if "init" in phases:
        pl.when(j == 0)(setup)
    if "main" in phases:
        sweep()
    if "final" in phases:
        pl.when(j == nj - 1)(finish)


def _hgrn_sattn_kernel(pt_ref, *refs, n_pages_in, hg_kw, sa_kw, NH, NA, nt_h, nj_a):
    h_in, refs = refs[:7], refs[7:]
    a_in, refs = refs[:8 + n_pages_in], refs[8 + n_pages_in:]
    h_out, a_out, h_scr, a_scr = refs[:2], refs[2:3], refs[3:4], refs[4:]
    s = pl.program_id(0)

    def recurrence(phases=PHASES):
        _hgrn_kernel(*h_in, *h_out, *h_scr, t=s % nt_h, nt=nt_h, phases=phases, **hg_kw)

    def attention(phases=PHASES):
        _sattn_kernel(pt_ref, *a_in, *a_out, *a_scr, j=s % nj_a, nj=nj_a, phases=phases, **sa_kw)

    if NH == NA:
        for ph in PHASES:
            recurrence((ph,))
            attention((ph,))
    else:
        pl.when(s < NH)(recurrence)
        pl.when(s < NA)(attention)


def hgrn2_with_sample_attention(main, s0, lb, gn, chunk, qs8, kn8, vn8, cache_k, cache_v, layer, page_table,
                                lams, subln, lam_init, tnew):
    B, T, _ = main.shape
    heads = s0.shape[1]
    W = heads * HEAD
    DB, _, WA = qs8.shape
    aheads = WA // HEAD
    page = cache_k.shape[3]
    n_pages = page_table.shape[1]
    pps = math.gcd(PAGES_PER_STEP, n_pages)
    nj_a = n_pages // pps
    NA = DB * nj_a
    cands = [c for c in (chunk, 2 * chunk, 4 * chunk) if T % c == 0]
    tt = min(cands, key=lambda c: abs(math.log((B * T // c) / NA)))
    nt_h = T // tt
    NH = B * nt_h
    R = aheads * 2 * SUBLANES

    def hb(s):
        return jnp.minimum(s, NH - 1) // nt_h

    def ht(s):
        return jnp.minimum(s, NH - 1) % nt_h

    def ab(s):
        return jnp.minimum(s, NA - 1) // nj_a

    def aj(s):
        return jnp.minimum(s, NA - 1) % nj_a

    blk = lambda c: pl.BlockSpec((None, tt, W), lambda s, pt: (hb(s), ht(s), c))
    vec = pl.BlockSpec((1, W), lambda s, pt: (0, 0))
    sblk = pl.BlockSpec((None, heads, HEAD, HEAD), lambda s, pt: (hb(s), 0, 0, 0))
    tok_spec = pl.BlockSpec((None, SUBLANES, WA), lambda s, pt: (ab(s), 0, 0))
    lam_spec = pl.BlockSpec((1, QK_HALF), lambda s, pt: (0, 0))

    def page_spec(p):
        return pl.BlockSpec((None, None, aheads, page, HEAD),
                            lambda s, pt: (layer, pt[ab(s), aj(s) * pps + p], 0, 0, 0))

    grid_spec = pltpu.PrefetchScalarGridSpec(
        num_scalar_prefetch=1,
        grid=(max(NH, NA),),
        in_specs=[blk(0), blk(1), blk(2), blk(3), sblk, vec, vec,
                  tok_spec, tok_spec, tok_spec, lam_spec, lam_spec, lam_spec, lam_spec,
                  pl.BlockSpec((1, WA), lambda s, pt: (0, 0))]
                 + [page_spec(p) for p in range(pps)] + [page_spec(p) for p in range(pps)],
        out_specs=[blk(0), sblk, tok_spec],
        scratch_shapes=[pltpu.VMEM((heads, HEAD, HEAD), F32),
                        pltpu.VMEM((R, WA), F32),
                        pltpu.VMEM((R, LANES), F32),
                        pltpu.VMEM((R, LANES), F32),
                        pltpu.VMEM((R, WA), F32)],
    )
    return pl.pallas_call(
        functools.partial(_hgrn_sattn_kernel, n_pages_in=2 * pps,
                          hg_kw=dict(tt=tt, chunk=chunk, heads=heads, valid=None),
                          sa_kw=dict(heads=aheads, tnew=tnew, lam_init=lam_init, pps=pps),
                          NH=NH, NA=NA, nt_h=nt_h, nj_a=nj_a),
        grid_spec=grid_spec,
        out_shape=[jax.ShapeDtypeStruct((B, T, W), BF16), jax.ShapeDtypeStruct(s0.shape, F32),
                   jax.ShapeDtypeStruct((DB, SUBLANES, WA), BF16)],
        compiler_params=_params(("arbitrary",)),
        name="hgrn2_sample_attention",
    )(page_table, main, main, main, main, s0, lb, gn, qs8, kn8, vn8, *lams, subln,
      *([cache_k] * pps), *([cache_v] * pps))


def _oproj_kernel(a_ref, b_ref, c_ref, w_ref, x_ref, gpost_ref, gnext_ref, xo_ref, h_ref, *, wa, wb):
    tm = x_ref.shape[0]
    rg = min(tm, OPROJ_ROWS)
    ys = []
    for r in range(tm // rg):
        rows = slice(r * rg, (r + 1) * rg)
        y = jnp.dot(a_ref[rows, :], w_ref[0:wa, :], preferred_element_type=F32)
        y = y + jnp.dot(b_ref[rows, :], w_ref[wa:wa + wb, :], preferred_element_type=F32)
        ys.append(y + jnp.dot(c_ref[rows, :], w_ref[wa + wb:, :], preferred_element_type=F32))
    for r in range(tm // rg):
        rows = slice(r * rg, (r + 1) * rg)
        x = x_ref[rows, :] + _rms(ys[r], gpost_ref[...])
        xo_ref[rows, :] = x
        h_ref[rows, :] = _rms(x, gnext_ref[...]).astype(BF16)


def out_proj(a, b, c, w, layer, x, gpost, gnext, tm):
    M, D = x.shape
    wa, wb, wc = a.shape[1], b.shape[1], c.shape[1]
    row = lambda width: pl.BlockSpec((tm, width), lambda i: (i, 0))
    vec = pl.BlockSpec((1, D), lambda i: (0, 0))
    return pl.pallas_call(
        functools.partial(_oproj_kernel, wa=wa, wb=wb),
        grid=(M // tm,),
        in_specs=[row(wa), row(wb), row(wc), pl.BlockSpec((None, D, D), lambda i: (layer, 0, 0)), row(D), vec, vec],
        out_specs=[row(D), row(D)],
        out_shape=[jax.ShapeDtypeStruct((M, D), F32), jax.ShapeDtypeStruct((M, D), BF16)],
        compiler_params=_params(("parallel",)),
        name="out_proj",
    )(a, b, c, w, x, gpost, gnext)


def _ffn_up_seq_kernel(h_ref, wg_ref, wu_ref, cw_ref, cb_ref, prev_ref, act_ref, st_ref,
                       wgb, wub, buf, *, tm, groups):
    b = pl.program_id(1)
    t = pl.program_id(2)
    H = SUBLANES
    tn = buf.shape[1]
    gw = tn // groups

    @pl.when(t == 0)
    def _():
        @pl.when(b == 0)
        def _():
            wgb[...] = wg_ref[...].astype(BF16)
            wub[...] = wu_ref[...].astype(BF16)

        buf[0:H - 2, :] = jnp.zeros((H - 2, tn), F32)
        buf[H - 2:H, :] = prev_ref[...]

    rg = min(tm, FFN_UP_ROWS)
    blocks = [(r, g) for r in range(tm // rg) for g in range(groups)]
    prods = []
    for r, g in blocks:
        cols = slice(g * gw, (g + 1) * gw)
        h = h_ref[r * rg:(r + 1) * rg, :]
        prods.append((jnp.dot(h, wgb[:, cols], preferred_element_type=F32),
                      jnp.dot(h, wub[:, cols], preferred_element_type=F32)))
    for (r, g), (u, up) in zip(blocks, prods):
        cols = slice(g * gw, (g + 1) * gw)
        r0 = H + r * rg
        buf[r0:r0 + rg, cols] = u
        c = cb_ref[:, cols] + cw_ref[2:3, cols] * u
        c = c + cw_ref[1:2, cols] * buf[r0 - 1:r0 - 1 + rg, cols]
        c = c + cw_ref[0:1, cols] * buf[r0 - 2:r0 - 2 + rg, cols]
        act_ref[r * rg:(r + 1) * rg, cols] = (c * (1.0 / (1.0 + jnp.exp(-c))) * up).astype(BF16)
    tail = buf[tm:tm + H, :]
    st_ref[...] = tail[H - 2:, :]
    buf[0:H, :] = tail


def ffn_up_seq(h3, wg, wu, layer, cw, cb, prev, tn):
    B, T, D = h3.shape
    Fd = wg.shape[2]
    tm = min(T, 1024)
    groups = max(1, tn // MXU_COLS)
    wspec = pl.BlockSpec((None, D, tn), lambda j, b, t: (layer, 0, j))
    return pl.pallas_call(
        functools.partial(_ffn_up_seq_kernel, tm=tm, groups=groups),
        grid=(Fd // tn, B, T // tm),
        in_specs=[pl.BlockSpec((None, tm, D), lambda j, b, t: (b, t, 0)), wspec, wspec,
                  pl.BlockSpec((CONV_W, tn), lambda j, b, t: (0, j)),
                  pl.BlockSpec((1, tn), lambda j, b, t: (0, j)),
                  pl.BlockSpec((None, CONV_W - 1, tn), lambda j, b, t: (b, 0, j))],
        out_specs=[pl.BlockSpec((None, tm, tn), lambda j, b, t: (b, t, j)),
                   pl.BlockSpec((None, CONV_W - 1, tn), lambda j, b, t: (b, 0, j))],
        out_shape=[jax.ShapeDtypeStruct((B, T, Fd), BF16),
                   jax.ShapeDtypeStruct((B, CONV_W - 1, Fd), F32)],
        scratch_shapes=[pltpu.VMEM((D, tn), BF16), pltpu.VMEM((D, tn), BF16),
                        pltpu.VMEM((SUBLANES + tm, tn), F32)],
        compiler_params=_params(("arbitrary", "arbitrary", "arbitrary")),
        name="ffn_up_prompt",
    )(h3, wg, wu, cw, cb, prev)


def _ffn_up_short_kernel(h_ref, wg_ref, wu_ref, cw_ref, cb_ref, p1_ref, p2_ref, act_ref, u_ref, *, T):
    h = h_ref[...]
    u = jnp.dot(h, wg_ref[...].astype(BF16), preferred_element_type=F32)
    up = jnp.dot(h, wu_ref[...].astype(BF16), preferred_element_type=F32)
    u_ref[...] = u
    r = lax.broadcasted_iota(jnp.int32, u.shape, 0) & (T - 1)
    um1 = jnp.where(r >= 1, pltpu.roll(u, 1, axis=0), p1_ref[...])
    um2 = jnp.where(r >= 2, pltpu.roll(u, 2, axis=0), p2_ref[...])
    c = cb_ref[...] + cw_ref[2:3, :] * u + cw_ref[1:2, :] * um1 + cw_ref[0:1, :] * um2
    sg, _ = _sigmoid_pair(c)
    act_ref[...] = (c * sg * up).astype(BF16)


def ffn_up_short(h, wg, wu, layer, cw, cb, p1, p2, T, tn):
    M, D = h.shape
    Fd = wg.shape[2]
    wspec = pl.BlockSpec((None, D, tn), lambda j: (layer, 0, j))
    col = pl.BlockSpec((M, tn), lambda j: (0, j))
    return pl.pallas_call(
        functools.partial(_ffn_up_short_kernel, T=T),
        grid=(Fd // tn,),
        in_specs=[pl.BlockSpec((M, D), lambda j: (0, 0)), wspec, wspec,
                  pl.BlockSpec((CONV_W, tn), lambda j: (0, j)),
                  pl.BlockSpec((1, tn), lambda j: (0, j)), col, col],
        out_specs=[col, col],
        out_shape=[jax.ShapeDtypeStruct((M, Fd), BF16), jax.ShapeDtypeStruct((M, Fd), F32)],
        compiler_params=_params(("parallel",)),
        name="ffn_up_sample",
    )(h, wg, wu, cw, cb, p1, p2)


def _ffn_down_kernel(a_ref, w_ref, x_ref, gpost_ref, gnext_ref, xo_ref, h_ref):
    y = jnp.dot(a_ref[...], w_ref[...], preferred_element_type=F32)
    x = x_ref[...] + _rms(y, gpost_ref[...])
    xo_ref[...] = x
    h_ref[...] = _rms(x, gnext_ref[...]).astype(BF16)


def ffn_down(act, w, layer, x, gpost, gnext, tm):
    M, D = x.shape
    Fd = act.shape[1]
    row = pl.BlockSpec((tm, D), lambda i: (i, 0))
    vec = pl.BlockSpec((1, D), lambda i: (0, 0))
    return pl.pallas_call(
        _ffn_down_kernel,
        grid=(M // tm,),
        in_specs=[pl.BlockSpec((tm, Fd), lambda i: (i, 0)),
                  pl.BlockSpec((None, Fd, D), lambda i: (layer, 0, 0), pipeline_mode=pl.Buffered(1)),
                  row, vec, vec],
        out_specs=[row, row],
        out_shape=[jax.ShapeDtypeStruct((M, D), F32), jax.ShapeDtypeStruct((M, D), BF16)],
        compiler_params=_params(("arbitrary",)),
        name="ffn_down",
    )(act, w, x, gpost, gnext)


def _largest_tile(n, cap, quantum):
    best = quantum
    for t in range(quantum, cap + 1, quantum):
        if n % t == 0:
            best = t
    return best


def _project(h, l, lw, B, T):
    M = B * T
    tm = min(M, 1024)
    WA = lw['pool_scale'].shape[1]
    WH = lw['hgrn_norm'].shape[1]
    WM = lw['w_in'].shape[2] - WA
    xa = matmul(h, lw['w_in'], l, 0, WA, tm, WA, "proj_pool").reshape(B, T, WA)
    main = matmul(h, lw['w_in'], l, WA, WM, tm, WH, "proj_main").reshape(B, T, WM)
    return xa, main


def _finish(x, a_out, o_b, o_c, l, lw, B, T, prev_conv, short):
    M, D = x.shape
    WA, WH, WC = a_out.shape[-1], o_b.shape[-1], o_c.shape[-1]
    x, h2 = out_proj(a_out.reshape(M, WA), o_b.reshape(M, WH), o_c.reshape(M, WC), lw['w_out'], l,
                     x, lw['norm_mix_post'], lw['norm_ffn_pre'], min(M, 2 * OPROJ_ROWS))
    Fd = lw['w_gate'].shape[2]
    tn = _largest_tile(Fd, 512, LANES)
    if not short:
        act, conv_new = ffn_up_seq(h2.reshape(B, T, D), lw['w_gate'], lw['w_up'], l, lw['conv_w'],
                                   lw['conv_b'], prev_conv, tn)
        act = act.reshape(M, Fd)
    else:
        z = jnp.zeros((B, T, Fd), F32)
        p1 = z.at[:, 0].set(prev_conv[:, 1]).reshape(M, Fd)
        p2 = z.at[:, 0].set(prev_conv[:, 0]).at[:, 1].set(prev_conv[:, 1]).reshape(M, Fd)
        act, u = ffn_up_short(h2, lw['w_gate'], lw['w_up'], l, lw['conv_w'], lw['conv_b'], p1, p2, T, tn)
        conv_new = u.reshape(B, T, Fd)[:, T - (CONV_W - 1):]
    x, h_next = ffn_down(act, lw['w_down'], l, x, lw['norm_ffn_post'], lw['norm_next'], min(M, FFN_DOWN_ROWS))
    return x, h_next, conv_new


def _layer(l, lw, xp, hp, xs, hs, *, B, T, DB, TS, P, zero_states, states, paged, kv_stacks):
    WC = lw['attn_subln'].shape[1]
    lam_init = 0.8 - 0.6 * math.exp(-0.3 * l)
    lams = (lw['lam_q1'], lw['lam_k1'], lw['lam_q2'], lw['lam_k2'])
    zp_pool, zp_S, zp_conv = zero_states
    sp_pool, sp_S, sp_conv = states
    cache_k, cache_v, page_table = paged

    xa_p, main_p = _project(hp, l, lw, B, T)
    xa_s, main_s = _project(hs, l, lw, DB, TS)
    a_p, pool_p = pool_mixer(xa_p, zp_pool, lw['pool_w'], lw['pool_scale'], 0)
    a_s, pool_s = pool_mixer(xa_s, sp_pool, lw['pool_w'], lw['pool_scale'], P)

    pad = ((0, 0), (0, SUBLANES - TS), (0, 0))
    main8 = jnp.pad(main_s, pad)
    ob8, S_s = hgrn2(main8, sp_S, lw['hgrn_lb'], lw['hgrn_norm'], SUBLANES, TS)
    qs8, k_rot8, _ = rope(main8, 4, 5, WC, lw['rope_tabs'], P)
    v_s = main_s[:, :, 6 * WC:]

    qs, kb, vb, k_stack, v_stack = rope_heads(main_p, 4, WC, lw['rope_tabs'], 0, *kv_stacks, l)
    ob_p, S_p, oc8 = hgrn2_with_sample_attention(
        main_p, zp_S, lw['hgrn_lb'], lw['hgrn_norm'], min(HG_CHUNK, T),
        qs8.astype(F32), k_rot8, jnp.pad(v_s, pad), cache_k, cache_v, l, page_table,
        lams, lw['attn_subln'], lam_init, TS)
    oc_p = prompt_attention(qs, kb, vb, lams, lw['attn_subln'], lam_init)

    xp, hp, conv_p = _finish(xp, a_p, ob_p, oc_p, l, lw, B, T, zp_conv, False)
    xs, hs, conv_s = _finish(xs, a_s, ob8[:, :TS], oc8[:, :TS], l, lw, DB, TS, sp_conv, True)
    return (xp, hp, xs, hs, (k_stack, v_stack), (pool_p, S_p, conv_p),
            (k_rot8[:, :TS], v_s, pool_s, S_s, conv_s))


def kernel(x_prompt, x_sample, cache_k, cache_v, state_pool, state_hgrn, state_conv, page_table, w_in, w_out, norm_mix_pre, norm_mix_post, norm_ffn_pre, norm_ffn_post, pool_w, pool_scale, hgrn_lower_bounds, hgrn_norm, attn_lam_q1, attn_lam_k1, attn_lam_q2, attn_lam_k2, attn_subln, ffn_w_gate, ffn_w_up, ffn_conv_w, ffn_conv_b, ffn_w_down):
    depth = w_in.shape[0]
    B, T, D = x_prompt.shape
    DB, TS, _ = x_sample.shape
    page = cache_k.shape[2]
    P = page_table.shape[1] * page
    WA = pool_scale.shape[1]
    WH = hgrn_norm.shape[1]
    WC = attn_subln.shape[1]
    HH = WH // HEAD
    Fd = ffn_w_gate.shape[2]
    assert TS >= CONV_W - 1 and TS <= SUBLANES and (TS & (TS - 1)) == 0

    sm = jax.nn.softmax(hgrn_lower_bounds.astype(F32), axis=0)
    lbs = jnp.cumsum(sm, axis=0) - sm[0]

    half = ROT_DIM // 2
    freqs = jnp.power(ROPE_THETA, -jnp.arange(0, ROT_DIM, 2, dtype=F32) / ROT_DIM)
    lane = np.arange(LANES)
    in_rot = (lane % QK_HALF) < ROT_DIM
    fl = jnp.where(jnp.asarray(in_rot), jnp.tile(freqs, LANES // half), 0.0).reshape(1, LANES)
    s1 = jnp.asarray(np.where(in_rot & ((lane % QK_HALF) < half), -1.0, 0.0), F32).reshape(1, LANES)
    s2 = jnp.asarray(np.where(in_rot & ((lane % QK_HALF) >= half), 1.0, 0.0), F32).reshape(1, LANES)

    ones = jnp.ones((1, D), F32)
    HC = WC // HEAD
    ck = jnp.swapaxes(cache_k, 2, 3)
    cv = jnp.swapaxes(cache_v, 2, 3)
    kv = (jnp.zeros((depth, B, HC, T, HEAD), F32), jnp.zeros((depth, B, HC, T, HEAD), F32))

    w_out_b = w_out.astype(BF16)
    w_down_b = ffn_w_down.astype(BF16)

    def layer_weights(l):
        return {
            'w_in': w_in,
            'w_out': w_out_b,
            'w_gate': ffn_w_gate,
            'w_up': ffn_w_up,
            'w_down': w_down_b,
            'pool_w': pool_w[l],
            'pool_scale': pool_scale[l].reshape(1, WA),
            'hgrn_lb': lbs[l].reshape(1, WH),
            'hgrn_norm': hgrn_norm[l].reshape(1, WH),
            'attn_subln': attn_subln[l].reshape(1, WC),
            'lam_q1': attn_lam_q1[l].reshape(1, QK_HALF),
            'lam_k1': attn_lam_k1[l].reshape(1, QK_HALF),
            'lam_q2': attn_lam_q2[l].reshape(1, QK_HALF),
            'lam_k2': attn_lam_k2[l].reshape(1, QK_HALF),
            'norm_mix_post': norm_mix_post[l].reshape(1, D),
            'norm_ffn_pre': norm_ffn_pre[l].reshape(1, D),
            'norm_ffn_post': norm_ffn_post[l].reshape(1, D),
            'norm_next': norm_mix_pre[l + 1].reshape(1, D) if l + 1 < depth else ones,
            'conv_w': ffn_conv_w[l],
            'conv_b': ffn_conv_b[l].reshape(1, Fd),
            'rope_tabs': (fl, s1, s2),
        }

    xp = x_prompt.reshape(B * T, D)
    xs = x_sample.reshape(DB * TS, D)
    g0 = norm_mix_pre[0].reshape(1, D)
    hp = rmsnorm_bf16(xp, g0, min(B * T, 512))
    hs = rmsnorm_bf16(xs, g0, DB * TS)
    zp_pool = jnp.zeros((B, POOL_STATE, WA), F32)
    zp_S = jnp.zeros((B, HH, HEAD, HEAD), F32)
    zp_conv = jnp.zeros((B, CONV_W - 1, Fd), F32)

    st_p, st_s = [], []
    for l in range(depth):
        xp, hp, xs, hs, kv, sp, ss = _layer(
            l, layer_weights(l), xp, hp, xs, hs, B=B, T=T, DB=DB, TS=TS, P=P,
            zero_states=(zp_pool, zp_S, zp_conv), states=(state_pool[l], state_hgrn[l], state_conv[l]),
            paged=(ck, cv, page_table), kv_stacks=kv)
        st_p.append(sp)
        st_s.append(ss)

    def stack(sts, i, shape):
        return jnp.stack([s[i] for s in sts]).reshape(shape)

    return (xp.reshape(B, T, D), xs.reshape(DB, TS, D),
            jnp.swapaxes(kv[0], 2, 3), jnp.swapaxes(kv[1], 2, 3),
            stack(st_p, 0, (depth, B, POOL_STATE, WA)), stack(st_p, 1, (depth, B, HH, HEAD, HEAD)),
            stack(st_p, 2, (depth, B, CONV_W - 1, Fd)),
            stack(st_s, 0, (depth, DB, TS, HC, HEAD)), stack(st_s, 1, (depth, DB, TS, HC, HEAD)),
            stack(st_s, 2, (depth, DB, POOL_STATE, WA)), stack(st_s, 3, (depth, DB, HH, HEAD, HEAD)),
            stack(st_s, 4, (depth, DB, CONV_W - 1, Fd)))
```

```python
import functools
import math

import numpy as np
import jax
import jax.numpy as jnp
from jax import lax
from jax.experimental import pallas as pl
from jax.experimental.pallas import tpu as pltpu

F32 = jnp.float32
BF16 = jnp.bfloat16

LANES = 128
SUBLANES = 8
MXU_COLS = 256
VMEM_LIMIT = 48 * 1024 * 1024

POOL_WINDOWS = (2, 4, 8, 16)
POOL_STATE = max(POOL_WINDOWS) - 1
HEAD = 128
QK_HALF = 64
QK_SCALE_LOG2E = QK_HALF ** -0.5 * math.log2(math.e)
ROT_DIM = 16
ROPE_THETA = 500000.0
LB_FLOOR = 1e-30
MASK_VALUE = -1e30
CONV_W = 3
EPS = 1e-6
HG_CHUNK = 64
HG_DIAG = 8
PHASES = ("init", "main", "final")
HG_UNROLL = 2
HG_HEADS_PER_STEP = 6
PAGES_PER_STEP = 16
OPROJ_ROWS = 256
FFN_UP_ROWS = 256
FFN_DOWN_ROWS = 256


def _params(sem):
    return pltpu.CompilerParams(dimension_semantics=sem, vmem_limit_bytes=VMEM_LIMIT)


def _rms(x, g):
    return x * lax.rsqrt(jnp.mean(x * x, axis=-1, keepdims=True) + EPS) * g


def _sigmoid_pair(z):
    e = jnp.exp(-jnp.abs(z))
    r = 1.0 / (1.0 + e)
    er = e * r
    pos = z >= 0
    return jnp.where(pos, r, er), jnp.where(pos, er, r)


def _norm_kernel(x_ref, g_ref, h_ref):
    h_ref[...] = _rms(x_ref[...], g_ref[...]).astype(BF16)


def rmsnorm_bf16(x, g, tm):
    M, D = x.shape
    return pl.pallas_call(
        _norm_kernel,
        grid=(M // tm,),
        in_specs=[pl.BlockSpec((tm, D), lambda i: (i, 0)),
                  pl.BlockSpec((1, D), lambda i: (0, 0))],
        out_specs=pl.BlockSpec((tm, D), lambda i: (i, 0)),
        out_shape=jax.ShapeDtypeStruct((M, D), BF16),
        compiler_params=_params(("parallel",)),
        name="rmsnorm",
    )(x, g)


def _mm_kernel(x_ref, *rest, nsub):
    w_refs, o_ref, wb = rest[:nsub], rest[nsub], rest[nsub + 1]

    @pl.when(pl.program_id(1) == 0)
    def _():
        for c in range(nsub):
            wb[:, c * MXU_COLS:(c + 1) * MXU_COLS] = w_refs[c][...].astype(BF16)

    o_ref[...] = jnp.dot(x_ref[...], wb[...], preferred_element_type=F32)


def matmul(x, w, layer, col0, N, tm, tn, name):
    M, K = x.shape
    nsub = tn // MXU_COLS
    assert col0 % MXU_COLS == 0 and tn % MXU_COLS == 0 and N % tn == 0
    wspec = lambda c: pl.BlockSpec((None, K, MXU_COLS),
                                   lambda j, i: (layer, 0, col0 // MXU_COLS + j * nsub + c))
    return pl.pallas_call(
        functools.partial(_mm_kernel, nsub=nsub),
        grid=(N // tn, M // tm),
        in_specs=[pl.BlockSpec((tm, K), lambda j, i: (i, 0))] + [wspec(c) for c in range(nsub)],
        out_specs=pl.BlockSpec((tm, tn), lambda j, i: (i, j)),
        out_shape=jax.ShapeDtypeStruct((M, N), F32),
        scratch_shapes=[pltpu.VMEM((K, tn), BF16)],
        compiler_params=_params(("arbitrary", "arbitrary")),
        name=name,
    )(x, *([w] * nsub))


def _pool_kernel(xa_ref, *rest, tt, pos0, nsub):
    w_refs, rest = rest[:nsub], rest[nsub:]
    prev_ref, wp_ref, sc_ref, y_ref, st_ref, buf = rest[:6]
    t = pl.program_id(1)
    P = POOL_STATE + 1

    @pl.when(t == 0)
    def _():
        buf[0:1, :] = jnp.zeros((1, buf.shape[1]), F32)
        buf[1:P, :] = prev_ref[...]

    if nsub:
        wb = rest[6]

        @pl.when(jnp.logical_and(pl.program_id(0) == 0, t == 0))
        def _():
            for c in range(nsub):
                wb[:, c * MXU_COLS:(c + 1) * MXU_COLS] = w_refs[c][...].astype(BF16)

        buf[P:P + tt, :] = jnp.dot(xa_ref[...], wb[...], preferred_element_type=F32)
    else:
        buf[P:P + tt, :] = xa_ref[...]
    pos = pos0 + t * tt + lax.broadcasted_iota(jnp.int32, (tt, LANES), 0)
    for g, w in enumerate(POOL_WINDOWS):
        cols = slice(g * LANES, (g + 1) * LANES)
        x = buf[P:P + tt, cols]
        acc = x
        for j in range(1, w):
            acc = acc + buf[P - j:P - j + tt, cols]
        cnt = jnp.minimum(w, pos + 1).astype(F32)
        d = acc / cnt - x
        y = jnp.dot(d.astype(BF16), wp_ref[g].astype(BF16), preferred_element_type=F32)
        y_ref[:, cols] = (y * sc_ref[:, cols]).astype(BF16)
    tail = buf[tt:tt + P, :]
    st_ref[...] = tail[1:, :]
    buf[0:P, :] = tail


def pool_mixer(xa, prev, wp, scale, pos0, w_in=None, layer=0):
    B, T, _ = xa.shape
    W = prev.shape[2]
    tt = min(T, 512)
    nsub = 0 if w_in is None else W // MXU_COLS
    wspec = lambda c: pl.BlockSpec((None, xa.shape[2], MXU_COLS), lambda b, t: (layer, 0, c))
    return pl.pallas_call(
        functools.partial(_pool_kernel, tt=tt, pos0=pos0, nsub=nsub),
        grid=(B, T // tt),
        in_specs=[pl.BlockSpec((None, tt, xa.shape[2]), lambda b, t: (b, t, 0))]
                 + [wspec(c) for c in range(nsub)]
                 + [pl.BlockSpec((None, POOL_STATE, W), lambda b, t: (b, 0, 0)),
                  pl.BlockSpec((len(POOL_WINDOWS), LANES, LANES), lambda b, t: (0, 0, 0)),
                  pl.BlockSpec((1, W), lambda b, t: (0, 0))],
        out_specs=[pl.BlockSpec((None, tt, W), lambda b, t: (b, t, 0)),
                   pl.BlockSpec((None, POOL_STATE, W), lambda b, t: (b, 0, 0))],
        out_shape=[jax.ShapeDtypeStruct((B, T, W), BF16),
                   jax.ShapeDtypeStruct((B, POOL_STATE, W), F32)],
        scratch_shapes=[pltpu.VMEM((POOL_STATE + 1 + tt, W), F32)]
                       + ([pltpu.VMEM((xa.shape[2], W), BF16)] if nsub else []),
        compiler_params=_params(("arbitrary", "arbitrary")),
        name="pool_mixer",
    )(xa, *([w_in] * nsub), prev, wp, scale)


def _rope_kernel(q_ref, k_ref, fl_ref, s1_ref, s2_ref, qs_ref, kr_ref, kb_ref, *, tt, pos0, heads):
    t = pl.program_id(1)
    pos = (pos0 + t * tt + lax.broadcasted_iota(jnp.int32, (tt, LANES), 0)).astype(F32)
    ang = pos * fl_ref[...]
    c = jnp.cos(ang)
    s = jnp.sin(ang)
    sa = s * s1_ref[...]
    sb = s * s2_ref[...]
    half = ROT_DIM // 2
    for h in range(heads):
        cols = slice(h * HEAD, (h + 1) * HEAD)
        xq = q_ref[:, cols]
        yq = xq * c + pltpu.roll(xq, LANES - half, axis=1) * sa + pltpu.roll(xq, half, axis=1) * sb
        qs_ref[:, cols] = (yq * (QK_HALF ** -0.5)).astype(BF16)
        xk = k_ref[:, cols]
        yk = xk * c + pltpu.roll(xk, LANES - half, axis=1) * sa + pltpu.roll(xk, half, axis=1) * sb
        kr_ref[:, cols] = yk
        kb_ref[:, cols] = yk.astype(BF16)


def rope(main, qblk, kblk, width, tabs, pos0):
    B, T, _ = main.shape
    tt = min(T, 512)
    heads = width // HEAD
    tab = pl.BlockSpec((1, LANES), lambda b, t: (0, 0))
    blk = lambda c: pl.BlockSpec((None, tt, width), lambda b, t: (b, t, c))
    return pl.pallas_call(
        functools.partial(_rope_kernel, tt=tt, pos0=pos0, heads=heads),
        grid=(B, T // tt),
        in_specs=[blk(qblk), blk(kblk), tab, tab, tab],
        out_specs=[blk(0), blk(0), blk(0)],
        out_shape=[jax.ShapeDtypeStruct((B, T, width), BF16),
                   jax.ShapeDtypeStruct((B, T, width), F32),
                   jax.ShapeDtypeStruct((B, T, width), BF16)],
        compiler_params=_params(("parallel", "parallel")),
        name="rope",
    )(main, main, *tabs)


def _rope_heads_kernel(q_ref, k_ref, v_ref, fl_ref, s1_ref, s2_ref, kin_ref, vin_ref,
                       qs_ref, kb_ref, vb_ref, ko_ref, vo_ref, *, tt, pos0, heads):
    del kin_ref, vin_ref
    t = pl.program_id(1)
    pos = (pos0 + t * tt + lax.broadcasted_iota(jnp.int32, (tt, LANES), 0)).astype(F32)
    ang = pos * fl_ref[...]
    c = jnp.cos(ang)
    s = jnp.sin(ang)
    sa = s * s1_ref[...]
    sb = s * s2_ref[...]
    half = ROT_DIM // 2
    for h in range(heads):
        cols = slice(h * HEAD, (h + 1) * HEAD)
        xq = q_ref[:, cols]
        yq = xq * c + pltpu.roll(xq, LANES - half, axis=1) * sa + pltpu.roll(xq, half, axis=1) * sb
        qs_ref[h] = (yq * QK_SCALE_LOG2E).astype(BF16)
        xk = k_ref[:, cols]
        yk = xk * c + pltpu.roll(xk, LANES - half, axis=1) * sa + pltpu.roll(xk, half, axis=1) * sb
        ko_ref[h] = yk
        kb_ref[h] = yk.astype(BF16)
        xv = v_ref[:, cols]
        vo_ref[h] = xv
        vb_ref[h, :, 0:HEAD] = xv.astype(BF16)
        vb_ref[h, :, HEAD:2 * HEAD] = jnp.ones((tt, HEAD), BF16)


def rope_heads(main, qblk, width, tabs, pos0, k_stack, v_stack, layer):
    B, T, _ = main.shape
    tt = min(T, 512)
    heads = width // HEAD
    tab = pl.BlockSpec((1, LANES), lambda b, t: (0, 0))
    blk = lambda c: pl.BlockSpec((None, tt, width), lambda b, t: (b, t, c))
    hm = pl.BlockSpec((None, heads, tt, HEAD), lambda b, t: (b, 0, t, 0))
    st = pl.BlockSpec((None, None, heads, tt, HEAD), lambda b, t: (layer, b, 0, t, 0))
    anyspec = pl.BlockSpec(memory_space=pl.ANY)
    hshape = jax.ShapeDtypeStruct((B, heads, T, HEAD), BF16)
    vshape = jax.ShapeDtypeStruct((B, heads, T, 2 * HEAD), BF16)
    vm = pl.BlockSpec((None, heads, tt, 2 * HEAD), lambda b, t: (b, 0, t, 0))
    return pl.pallas_call(
        functools.partial(_rope_heads_kernel, tt=tt, pos0=pos0, heads=heads),
        grid=(B, T // tt),
        in_specs=[blk(qblk), blk(qblk + 1), blk(qblk + 2), tab, tab, tab, anyspec, anyspec],
        out_specs=[hm, hm, vm, st, st],
        out_shape=[hshape, hshape, vshape,
                   jax.ShapeDtypeStruct(k_stack.shape, F32), jax.ShapeDtypeStruct(v_stack.shape, F32)],
        input_output_aliases={6: 3, 7: 4},
        compiler_params=_params(("parallel", "parallel")),
        name="rope_heads",
    )(main, main, main, *tabs, k_stack, v_stack)


def _hgrn_kernel(q_ref, f_ref, i_ref, g_ref, s0_ref, lb_ref, gn_ref, o_ref, so_ref, st_ref,
                 *, tt, chunk, heads, valid, t=None, nt=None, phases=PHASES):
    if t is None:
        t, nt = pl.program_id(2), pl.num_programs(2)
    C = chunk
    levels = [n for n in (64, 32, 16) if n <= C]

    def load_state():
        for h in range(heads):
            st_ref[h] = s0_ref[h].T

    def store_state():
        for h in range(heads):
            so_ref[h] = st_ref[h].T

    if "init" in phases:
        pl.when(t == 0)(load_state)
    if "main" not in phases:
        if "final" in phases:
            pl.when(t == nt - 1)(store_state)
        return

    row = lax.broadcasted_iota(jnp.int32, (C, LANES), 0)
    rr = lax.broadcasted_iota(jnp.int32, (C, C), 0)
    cc = lax.broadcasted_iota(jnp.int32, (C, C), 1)
    lvl_masks = []
    for n in levels:
        sh = int(math.log2(n))
        same = (rr >> sh) == (cc >> sh)
        up = (rr & (n - 1)) >= n // 2
        lo = (cc & (n - 1)) < n // 2
        lvl_masks.append(jnp.where(same, jnp.where(up, jnp.where(lo, 1.0, 0.0), 0.0), 0.0))

    def chunk_body(c, carry):
        r0 = pl.multiple_of(c * C, C)
        rows = pl.ds(r0, C)
        for h in range(heads):
            cols = slice(h * HEAD, (h + 1) * HEAD)
            q = q_ref[rows, cols]
            z = f_ref[rows, cols]
            v = i_ref[rows, cols]
            gate = g_ref[rows, cols]
            lb = lb_ref[:, cols]
            lbm = jnp.maximum(lb, LB_FLOOR)
            oml = 1.0 - lb
            sg, sgn = _sigmoid_pair(z)
            f = lbm + oml * sg
            kin = oml * sgn - (lbm - lb)
            if valid is not None:
                ok = (r0 + row) < valid
                f = jnp.where(ok, f, 1.0)
                kin = jnp.where(ok, kin, 0.0)
            lf = jnp.log(f)
            b = lf
            s = 1
            while s < C:
                b = b + jnp.where(row >= s, pltpu.roll(b, s, axis=0), 0.0)
                s *= 2
            St = st_ref[h]
            qd = q * jnp.exp(b)
            o = lax.dot_general(qd.astype(BF16), St.astype(BF16), (((1,), (1,)), ((), ())),
                                preferred_element_type=F32)
            if levels:
                att = jnp.zeros((C, C), F32)
                for n, msk in zip(levels, lvl_masks):
                    pieces = []
                    for m in range(C // n):
                        rb = m * n + n // 2 - 1
                        pieces.append(jnp.broadcast_to(b[rb:rb + 1, :], (n, LANES)))
                    R = pieces[0] if len(pieces) == 1 else jnp.concatenate(pieces, axis=0)
                    qn = q * jnp.exp(jnp.minimum(b - R, 0.0))
                    kn = kin * jnp.exp(jnp.minimum(R - b, 0.0))
                    a = lax.dot_general(qn.astype(BF16), kn.astype(BF16), (((1,), (1,)), ((), ())),
                                        preferred_element_type=F32)
                    att = att + a * msk
                o = o + jnp.dot(att.astype(BF16), v.astype(BF16), preferred_element_type=F32)
            def back(x, j):
                return pltpu.roll(x.reshape(C // HG_DIAG, HG_DIAG, LANES), j, axis=1).reshape(C, LANES)

            rin = row & (HG_DIAG - 1)
            o = o + jnp.sum(q * kin, axis=-1, keepdims=True) * v
            e = f
            for j in range(1, HG_DIAG):
                if j > 1:
                    e = e * back(f, j - 1)
                kj = jnp.where(rin >= j, back(kin, j), 0.0)
                cj = jnp.sum(q * kj * e, axis=-1, keepdims=True)
                o = o + cj * back(v, j)
            bl = b[C - 1:C, :]
            kd = kin * jnp.exp(bl - b)
            upd = jnp.dot(v.T.astype(BF16), kd.astype(BF16), preferred_element_type=F32)
            st_ref[h] = St * jnp.exp(bl) + upd
            o_ref[rows, cols] = (_rms(o, gn_ref[:, cols]) * (gate / (1.0 + jnp.exp(-gate)))).astype(BF16)
        return carry

    if tt // C <= HG_UNROLL:
        for c in range(tt // C):
            chunk_body(c, 0)
    else:
        lax.fori_loop(0, tt // C, chunk_body, 0)

    if "final" in phases:
        pl.when(t == nt - 1)(store_state)


def hgrn2(main, s0, lb, gn, chunk, valid):
    B, T, _ = main.shape
    heads = s0.shape[1]
    hp = math.gcd(HG_HEADS_PER_STEP, heads)
    G = heads // hp
    W = heads * HEAD
    tt = min(T, 256)
    blk = lambda c: pl.BlockSpec((None, tt, hp * HEAD), lambda b, g, t: (b, t, c * G + g))
    vec = pl.BlockSpec((1, hp * HEAD), lambda b, g, t: (0, g))
    sblk = pl.BlockSpec((None, hp, HEAD, HEAD), lambda b, g, t: (b, g, 0, 0))
    return pl.pallas_call(
        functools.partial(_hgrn_kernel, tt=tt, chunk=chunk, heads=hp, valid=valid),
        grid=(B, G, T // tt),
        in_specs=[blk(0), blk(1), blk(2), blk(3), sblk, vec, vec],
        out_specs=[blk(0), sblk],
        out_shape=[jax.ShapeDtypeStruct((B, T, W), BF16),
                   jax.ShapeDtypeStruct(s0.shape, F32)],
        scratch_shapes=[pltpu.VMEM((hp, HEAD, HEAD), F32)],
        compiler_params=_params(("parallel", "parallel", "arbitrary")),
        name="hgrn2",
    )(main, main, main, main, s0, lb, gn)


def _lambda(lq1, lk1, lq2, lk2, lam_init):
    a = jnp.exp(jnp.sum(lq1 * lk1, axis=-1, keepdims=True))
    b = jnp.exp(jnp.sum(lq2 * lk2, axis=-1, keepdims=True))
    return a - b + lam_init


def _pattn_kernel(qi_ref, kj_ref, q_ref, kc_ref, kn_ref, v_ref, lq1, lk1, lq2, lk2, sub_ref, o_ref,
                  qst, m_ref, acc_ref, s_even, s_odd, *, tq, rs, lam_init):
    n = pl.program_id(2)
    qi = qi_ref[n]
    kj = kj_ref[n]

    def scores(k_ref):
        return lax.dot_general(qst[...], k_ref[...], (((1,), (1,)), ((), ())), preferred_element_type=F32)

    @pl.when(kj == 0)
    def _():
        q = q_ref[...]
        lane = lax.broadcasted_iota(jnp.int32, q.shape, 1)
        zero = jnp.zeros_like(q)
        qst[0:tq, :] = jnp.where(lane < QK_HALF, q, zero)
        qst[tq:2 * tq, :] = jnp.where(lane >= QK_HALF, q, zero)
        m_ref[...] = jnp.full(m_ref.shape, MASK_VALUE, F32)
        acc_ref[...] = jnp.zeros(acc_ref.shape, F32)
        s_even[...] = scores(kc_ref)

    def consume(s_ref, diag):
        for c in range(2 * tq // rs):
            rows = slice(c * rs, (c + 1) * rs)
            q0 = (c * rs) % tq
            kw = q0 + rs if diag else tq
            s = s_ref[rows, 0:kw]
            if diag:
                rq = lax.broadcasted_iota(jnp.int32, s.shape, 0) + q0
                ck = lax.broadcasted_iota(jnp.int32, s.shape, 1)
                s = jnp.where(ck <= rq, s, MASK_VALUE)
            m_prev = m_ref[rows, :]
            m_next = jnp.maximum(m_prev, jnp.max(s, axis=1, keepdims=True))
            p = jnp.exp2(s - m_next[:, 0:1])
            alpha = jnp.exp2(m_prev - m_next)
            pv = jnp.dot(p.astype(BF16), v_ref[0:kw, :], preferred_element_type=F32)
            acc_ref[rows, 0:HEAD] = alpha * acc_ref[rows, 0:HEAD] + pv[:, 0:HEAD]
            acc_ref[rows, HEAD:2 * HEAD] = alpha * acc_ref[rows, HEAD:2 * HEAD] + pv[:, HEAD:2 * HEAD]
            m_ref[rows, :] = m_next

    for parity, cur, nxt in ((0, s_even, s_odd), (1, s_odd, s_even)):
        mine = (kj & 1) == parity

        @pl.when(jnp.logical_and(kj < qi, mine))
        def _():
            nxt[...] = scores(kn_ref)
            consume(cur, False)

        @pl.when(jnp.logical_and(kj == qi, mine))
        def _():
            consume(cur, True)
            lam = _lambda(lq1[...], lk1[...], lq2[...], lk2[...], lam_init)
            o1 = acc_ref[0:tq, 0:HEAD] / acc_ref[0:tq, HEAD:2 * HEAD]
            o2 = acc_ref[tq:2 * tq, 0:HEAD] / acc_ref[tq:2 * tq, HEAD:2 * HEAD]
            o = o1 - lam * o2
            o_ref[...] = (_rms(o, sub_ref[...]) * (1.0 - lam_init)).astype(BF16)


def prompt_attention(qs, kb, vb, lams, subln, lam_init):
    B, heads, T, _ = qs.shape
    tq = min(T, 1024)
    rs = min(tq, 256)
    nq = T // tq
    pairs = [(i, j) for i in range(nq) for j in range(i + 1)]
    qi = jnp.asarray(np.array([p[0] for p in pairs], np.int32))
    kj = jnp.asarray(np.array([p[1] for p in pairs], np.int32))
    lam_spec = pl.BlockSpec((1, QK_HALF), lambda b, h, n, qi, kj: (0, 0))
    npairs = len(pairs)
    kc_spec = pl.BlockSpec((None, None, tq, HEAD), lambda b, h, n, qi, kj: (b, h, kj[n], 0))
    kn_spec = pl.BlockSpec((None, None, tq, HEAD),
                           lambda b, h, n, qi, kj: (b, h, kj[jnp.minimum(n + 1, npairs - 1)], 0))
    grid_spec = pltpu.PrefetchScalarGridSpec(
        num_scalar_prefetch=2,
        grid=(B, heads, len(pairs)),
        in_specs=[pl.BlockSpec((None, None, tq, HEAD), lambda b, h, n, qi, kj: (b, h, qi[n], 0)),
                  kc_spec, kn_spec,
                  pl.BlockSpec((None, None, tq, 2 * HEAD), lambda b, h, n, qi, kj: (b, h, kj[n], 0)),
                  lam_spec, lam_spec, lam_spec, lam_spec,
                  pl.BlockSpec((1, HEAD), lambda b, h, n, qi, kj: (0, h))],
        out_specs=pl.BlockSpec((None, tq, HEAD), lambda b, h, n, qi, kj: (b, qi[n], h)),
        scratch_shapes=[pltpu.VMEM((2 * tq, HEAD), BF16),
                        pltpu.VMEM((2 * tq, HEAD), F32),
                        pltpu.VMEM((2 * tq, 2 * HEAD), F32),
                        pltpu.VMEM((2 * tq, tq), F32),
                        pltpu.VMEM((2 * tq, tq), F32)],
    )
    return pl.pallas_call(
        functools.partial(_pattn_kernel, tq=tq, rs=rs, lam_init=lam_init),
        grid_spec=grid_spec,
        out_shape=jax.ShapeDtypeStruct((B, T, heads * HEAD), BF16),
        compiler_params=_params(("parallel", "parallel", "arbitrary")),
        name="prompt_attention",
    )(qi, kj, qs, kb, kb, vb, *lams, subln)


def _sattn_kernel(pt_ref, q_ref, kn_ref, vn_ref, lq1, lk1, lq2, lk2, sub_ref, *rest,
                  heads, tnew, lam_init, pps, j=None, nj=None, phases=PHASES):
    k_refs = rest[:pps]
    v_refs = rest[pps:2 * pps]
    o_ref = rest[2 * pps]
    qbd, m_ref, l_ref, acc_ref = rest[2 * pps + 1:]
    if j is None:
        j, nj = pl.program_id(1), pl.num_programs(1)
    R = heads * 2 * SUBLANES
    W = heads * HEAD

    def setup():
        q = q_ref[...]
        qrep = jnp.concatenate([q] * (heads * 2), axis=0)
        lane = lax.broadcasted_iota(jnp.int32, (R, W), 1)
        rowi = lax.broadcasted_iota(jnp.int32, (R, W), 0)
        qbd[...] = jnp.where((lane >> 6) == (rowi >> 3), qrep, 0.0)
        m_ref[...] = jnp.full(m_ref.shape, MASK_VALUE, F32)
        l_ref[...] = jnp.zeros(l_ref.shape, F32)
        acc_ref[...] = jnp.zeros(acc_ref.shape, F32)

    def page_rows(ref):
        return jnp.concatenate([ref[h] for h in range(heads)], axis=1).astype(BF16)

    def sweep():
        qb = qbd[...].astype(BF16)
        k_all = jnp.concatenate([page_rows(r) for r in k_refs], axis=0)
        s = lax.dot_general(qb, k_all, (((1,), (1,)), ((), ())), preferred_element_type=F32)
        m_prev = m_ref[...]
        m_next = jnp.maximum(m_prev, s.max(axis=1, keepdims=True))
        alpha = jnp.exp(m_prev - m_next)
        pr = jnp.exp(s - m_next[:, 0:1])
        v_all = jnp.concatenate([page_rows(r) for r in v_refs], axis=0)
        m_ref[...] = m_next
        l_ref[...] = alpha * l_ref[...] + jnp.sum(pr, axis=1, keepdims=True)
        acc_ref[...] = alpha[:, 0:1] * acc_ref[...] + jnp.dot(pr.astype(BF16), v_all,
                                                              preferred_element_type=F32)

    def finish():
        qf = qbd[...]
        tok = lax.broadcasted_iota(jnp.int32, (R, LANES), 0) & (SUBLANES - 1)
        m_p = m_ref[...]
        sn = []
        for jn in range(tnew):
            sj = jnp.sum(qf * kn_ref[jn:jn + 1, :], axis=1, keepdims=True)
            sn.append(jnp.where(tok >= jn, sj, MASK_VALUE))
        m_n = m_p
        for sj in sn:
            m_n = jnp.maximum(m_n, sj)
        al = jnp.exp(m_p - m_n)
        l_f = al * l_ref[...]
        acc_f = al[:, 0:1] * acc_ref[...]
        for jn in range(tnew):
            pj = jnp.exp(sn[jn] - m_n)
            l_f = l_f + pj
            acc_f = acc_f + pj[:, 0:1] * vn_ref[jn:jn + 1, :]
        lam = _lambda(lq1[...], lk1[...], lq2[...], lk2[...], lam_init)
        for h in range(heads):
            cols = slice(h * HEAD, (h + 1) * HEAD)
            r1 = slice(h * 2 * SUBLANES, h * 2 * SUBLANES + SUBLANES)
            r2 = slice(h * 2 * SUBLANES + SUBLANES, (h + 1) * 2 * SUBLANES)
            o1 = acc_f[r1, cols] / l_f[r1, :]
            o2 = acc_f[r2, cols] / l_f[r2, :]
            o = o1 - lam * o2
            o_ref[:, cols] = (_rms(o, sub_ref[:, cols]) * (1.0 - lam_init)).astype(BF16)

    if "init" in phases:
        pl.when(j == 0)(setup)
    if "main" in phases:
        sweep()
    if "final" in phases:
        pl.when(j == nj - 1)(finish)


def _hgrn_sattn_kernel(pt_ref, *refs, n_pages_in, hg_kw, sa_kw, NH, NA, nt_h, nj_a):
    h_in, refs = refs[:7], refs[7:]
    a_in, refs = refs[:8 + n_pages_in], refs[8 + n_pages_in:]
    h_out, a_out, h_scr, a_scr = refs[:2], refs[2:3], refs[3:4], refs[4:]
    s = pl.program_id(0)

    def recurrence(phases=PHASES):
        _hgrn_kernel(*h_in, *h_out, *h_scr, t=s % nt_h, nt=nt_h, phases=phases, **hg_kw)

    def attention(phases=PHASES):
        _sattn_kernel(pt_ref, *a_in, *a_out, *a_scr, j=s % nj_a, nj=nj_a, phases=phases, **sa_kw)

    if NH == NA:
        for ph in PHASES:
            recurrence((ph,))
            attention((ph,))
    else:
        pl.when(s < NH)(recurrence)
        pl.when(s < NA)(attention)


def hgrn2_with_sample_attention(main, s0, lb, gn, chunk, qs8, kn8, vn8, cache_k, cache_v, layer, page_table,
                                lams, subln, lam_init, tnew):
    B, T, _ = main.shape
    heads = s0.shape[1]
    W = heads * HEAD
    DB, _, WA = qs8.shape
    aheads = WA // HEAD
    page = cache_k.shape[3]
    n_pages = page_table.shape[1]
    pps = math.gcd(PAGES_PER_STEP, n_pages)
    nj_a = n_pages // pps
    NA = DB * nj_a
    cands = [c for c in (chunk, 2 * chunk, 4 * chunk) if T % c == 0]
    tt = min(cands, key=lambda c: abs(math.log((B * T // c) / NA)))
    nt_h = T // tt
    NH = B * nt_h
    R = aheads * 2 * SUBLANES

    def hb(s):
        return jnp.minimum(s, NH - 1) // nt_h

    def ht(s):
        return jnp.minimum(s, NH - 1) % nt_h

    def ab(s):
        return jnp.minimum(s, NA - 1) // nj_a

    def aj(s):
        return jnp.minimum(s, NA - 1) % nj_a

    blk = lambda c: pl.BlockSpec((None, tt, W), lambda s, pt: (hb(s), ht(s), c))
    vec = pl.BlockSpec((1, W), lambda s, pt: (0, 0))
    sblk = pl.BlockSpec((None, heads, HEAD, HEAD), lambda s, pt: (hb(s), 0, 0, 0))
    tok_spec = pl.BlockSpec((None, SUBLANES, WA), lambda s, pt: (ab(s), 0, 0))
    lam_spec = pl.BlockSpec((1, QK_HALF), lambda s, pt: (0, 0))

    def page_spec(p):
        return pl.BlockSpec((None, None, aheads, page, HEAD),
                            lambda s, pt: (layer, pt[ab(s), aj(s) * pps + p], 0, 0, 0))

    grid_spec = pltpu.PrefetchScalarGridSpec(
        num_scalar_prefetch=1,
        grid=(max(NH, NA),),
        in_specs=[blk(0), blk(1), blk(2), blk(3), sblk, vec, vec,
                  tok_spec, tok_spec, tok_spec, lam_spec, lam_spec, lam_spec, lam_spec,
                  pl.BlockSpec((1, WA), lambda s, pt: (0, 0))]
                 + [page_spec(p) for p in range(pps)] + [page_spec(p) for p in range(pps)],
        out_specs=[blk(0), sblk, tok_spec],
        scratch_shapes=[pltpu.VMEM((heads, HEAD, HEAD), F32),
                        pltpu.VMEM((R, WA), F32),
                        pltpu.VMEM((R, LANES), F32),
                        pltpu.VMEM((R, LANES), F32),
                        pltpu.VMEM((R, WA), F32)],
    )
    return pl.pallas_call(
        functools.partial(_hgrn_sattn_kernel, n_pages_in=2 * pps,
                          hg_kw=dict(tt=tt, chunk=chunk, heads=heads, valid=None),
                          sa_kw=dict(heads=aheads, tnew=tnew, lam_init=lam_init, pps=pps),
                          NH=NH, NA=NA, nt_h=nt_h, nj_a=nj_a),
        grid_spec=grid_spec,
        out_shape=[jax.ShapeDtypeStruct((B, T, W), BF16), jax.ShapeDtypeStruct(s0.shape, F32),
                   jax.ShapeDtypeStruct((DB, SUBLANES, WA), BF16)],
        compiler_params=_params(("arbitrary",)),
        name="hgrn2_sample_attention",
    )(page_table, main, main, main, main, s0, lb, gn, qs8, kn8, vn8, *lams, subln,
      *([cache_k] * pps), *([cache_v] * pps))


def _oproj_kernel(a_ref, b_ref, c_ref, w_ref, x_ref, gpost_ref, gnext_ref, xo_ref, h_ref, *, wa, wb):
    tm = x_ref.shape[0]
    rg = min(tm, OPROJ_ROWS)
    ys = []
    for r in range(tm // rg):
        rows = slice(r * rg, (r + 1) * rg)
        y = jnp.dot(a_ref[rows, :], w_ref[0:wa, :], preferred_element_type=F32)
        y = y + jnp.dot(b_ref[rows, :], w_ref[wa:wa + wb, :], preferred_element_type=F32)
        ys.append(y + jnp.dot(c_ref[rows, :], w_ref[wa + wb:, :], preferred_element_type=F32))
    for r in range(tm // rg):
        rows = slice(r * rg, (r + 1) * rg)
        x = x_ref[rows, :] + _rms(ys[r], gpost_ref[...])
        xo_ref[rows, :] = x
        h_ref[rows, :] = _rms(x, gnext_ref[...]).astype(BF16)


def out_proj(a, b, c, w, layer, x, gpost, gnext, tm):
    M, D = x.shape
    wa, wb, wc = a.shape[1], b.shape[1], c.shape[1]
    row = lambda width: pl.BlockSpec((tm, width), lambda i: (i, 0))
    vec = pl.BlockSpec((1, D), lambda i: (0, 0))
    return pl.pallas_call(
        functools.partial(_oproj_kernel, wa=wa, wb=wb),
        grid=(M // tm,),
        in_specs=[row(wa), row(wb), row(wc), pl.BlockSpec((None, D, D), lambda i: (layer, 0, 0)), row(D), vec, vec],
        out_specs=[row(D), row(D)],
        out_shape=[jax.ShapeDtypeStruct((M, D), F32), jax.ShapeDtypeStruct((M, D), BF16)],
        compiler_params=_params(("parallel",)),
        name="out_proj",
    )(a, b, c, w, x, gpost, gnext)


def _ffn_up_seq_kernel(h_ref, wg_ref, wu_ref, cw_ref, cb_ref, prev_ref, act_ref, st_ref,
                       wgb, wub, buf, *, tm, groups):
    b = pl.program_id(1)
    t = pl.program_id(2)
    H = SUBLANES
    tn = buf.shape[1]
    gw = tn // groups

    @pl.when(t == 0)
    def _():
        @pl.when(b == 0)
        def _():
            wgb[...] = wg_ref[...].astype(BF16)
            wub[...] = wu_ref[...].astype(BF16)

        buf[0:H - 2, :] = jnp.zeros((H - 2, tn), F32)
        buf[H - 2:H, :] = prev_ref[...]

    rg = min(tm, FFN_UP_ROWS)
    blocks = [(r, g) for r in range(tm // rg) for g in range(groups)]
    prods = []
    for r, g in blocks:
        cols = slice(g * gw, (g + 1) * gw)
        h = h_ref[r * rg:(r + 1) * rg, :]
        prods.append((jnp.dot(h, wgb[:, cols], preferred_element_type=F32),
                      jnp.dot(h, wub[:, cols], preferred_element_type=F32)))
    for (r, g), (u, up) in zip(blocks, prods):
        cols = slice(g * gw, (g + 1) * gw)
        r0 = H + r * rg
        buf[r0:r0 + rg, cols] = u
        c = cb_ref[:, cols] + cw_ref[2:3, cols] * u
        c = c + cw_ref[1:2, cols] * buf[r0 - 1:r0 - 1 + rg, cols]
        c = c + cw_ref[0:1, cols] * buf[r0 - 2:r0 - 2 + rg, cols]
        act_ref[r * rg:(r + 1) * rg, cols] = (c * (1.0 / (1.0 + jnp.exp(-c))) * up).astype(BF16)
    tail = buf[tm:tm + H, :]
    st_ref[...] = tail[H - 2:, :]
    buf[0:H, :] = tail


def ffn_up_seq(h3, wg, wu, layer, cw, cb, prev, tn):
    B, T, D = h3.shape
    Fd = wg.shape[2]
    tm = min(T, 1024)
    groups = max(1, tn // MXU_COLS)
    wspec = pl.BlockSpec((None, D, tn), lambda j, b, t: (layer, 0, j))
    return pl.pallas_call(
        functools.partial(_ffn_up_seq_kernel, tm=tm, groups=groups),
        grid=(Fd // tn, B, T // tm),
        in_specs=[pl.BlockSpec((None, tm, D), lambda j, b, t: (b, t, 0)), wspec, wspec,
                  pl.BlockSpec((CONV_W, tn), lambda j, b, t: (0, j)),
                  pl.BlockSpec((1, tn), lambda j, b, t: (0, j)),
                  pl.BlockSpec((None, CONV_W - 1, tn), lambda j, b, t: (b, 0, j))],
        out_specs=[pl.BlockSpec((None, tm, tn), lambda j, b, t: (b, t, j)),
                   pl.BlockSpec((None, CONV_W - 1, tn), lambda j, b, t: (b, 0, j))],
        out_shape=[jax.ShapeDtypeStruct((B, T, Fd), BF16),
                   jax.ShapeDtypeStruct((B, CONV_W - 1, Fd), F32)],
        scratch_shapes=[pltpu.VMEM((D, tn), BF16), pltpu.VMEM((D, tn), BF16),
                        pltpu.VMEM((SUBLANES + tm, tn), F32)],
        compiler_params=_params(("arbitrary", "arbitrary", "arbitrary")),
        name="ffn_up_prompt",
    )(h3, wg, wu, cw, cb, prev)


def _ffn_up_short_kernel(h_ref, wg_ref, wu_ref, cw_ref, cb_ref, p1_ref, p2_ref, act_ref, u_ref, *, T):
    h = h_ref[...]
    u = jnp.dot(h, wg_ref[...].astype(BF16), preferred_element_type=F32)
    up = jnp.dot(h, wu_ref[...].astype(BF16), preferred_element_type=F32)
    u_ref[...] = u
    r = lax.broadcasted_iota(jnp.int32, u.shape, 0) & (T - 1)
    um1 = jnp.where(r >= 1, pltpu.roll(u, 1, axis=0), p1_ref[...])
    um2 = jnp.where(r >= 2, pltpu.roll(u, 2, axis=0), p2_ref[...])
    c = cb_ref[...] + cw_ref[2:3, :] * u + cw_ref[1:2, :] * um1 + cw_ref[0:1, :] * um2
    sg, _ = _sigmoid_pair(c)
    act_ref[...] = (c * sg * up).astype(BF16)


def ffn_up_short(h, wg, wu, layer, cw, cb, p1, p2, T, tn):
    M, D = h.shape
    Fd = wg.shape[2]
    wspec = pl.BlockSpec((None, D, tn), lambda j: (layer, 0, j))
    col = pl.BlockSpec((M, tn), lambda j: (0, j))
    return pl.pallas_call(
        functools.partial(_ffn_up_short_kernel, T=T),
        grid=(Fd // tn,),
        in_specs=[pl.BlockSpec((M, D), lambda j: (0, 0)), wspec, wspec,
                  pl.BlockSpec((CONV_W, tn), lambda j: (0, j)),
                  pl.BlockSpec((1, tn), lambda j: (0, j)), col, col],
        out_specs=[col, col],
        out_shape=[jax.ShapeDtypeStruct((M, Fd), BF16), jax.ShapeDtypeStruct((M, Fd), F32)],
        compiler_params=_params(("parallel",)),
        name="ffn_up_sample",
    )(h, wg, wu, cw, cb, p1, p2)


def _ffn_down_kernel(a_ref, w_ref, x_ref, gpost_ref, gnext_ref, xo_ref, h_ref):
    y = jnp.dot(a_ref[...], w_ref[...], preferred_element_type=F32)
    x = x_ref[...] + _rms(y, gpost_ref[...])
    xo_ref[...] = x
    h_ref[...] = _rms(x, gnext_ref[...]).astype(BF16)


def ffn_down(act, w, layer, x, gpost, gnext, tm):
    M, D = x.shape
    Fd = act.shape[1]
    row = pl.BlockSpec((tm, D), lambda i: (i, 0))
    vec = pl.BlockSpec((1, D), lambda i: (0, 0))
    return pl.pallas_call(
        _ffn_down_kernel,
        grid=(M // tm,),
        in_specs=[pl.BlockSpec((tm, Fd), lambda i: (i, 0)),
                  pl.BlockSpec((None, Fd, D), lambda i: (layer, 0, 0), pipeline_mode=pl.Buffered(1)),
                  row, vec, vec],
        out_specs=[row, row],
        out_shape=[jax.ShapeDtypeStruct((M, D), F32), jax.ShapeDtypeStruct((M, D), BF16)],
        compiler_params=_params(("arbitrary",)),
        name="ffn_down",
    )(act, w, x, gpost, gnext)


def _largest_tile(n, cap, quantum):
    best = quantum
    for t in range(quantum, cap + 1, quantum):
        if n % t == 0:
            best = t
    return best


def _project(h, l, lw, B, T, with_pool):
    M = B * T
    tm = min(M, 1024)
    WA = lw['pool_scale'].shape[1]
    WH = lw['hgrn_norm'].shape[1]
    WM = lw['w_in'].shape[2] - WA
    xa = matmul(h, lw['w_in'], l, 0, WA, tm, WA, "proj_pool").reshape(B, T, WA) if with_pool else None
    main = matmul(h, lw['w_in'], l, WA, WM, tm, WH, "proj_main").reshape(B, T, WM)
    return xa, main


def _finish(x, a_out, o_b, o_c, l, lw, B, T, prev_conv, short):
    M, D = x.shape
    WA, WH, WC = a_out.shape[-1], o_b.shape[-1], o_c.shape[-1]
    x, h2 = out_proj(a_out.reshape(M, WA), o_b.reshape(M, WH), o_c.reshape(M, WC), lw['w_out'], l,
                     x, lw['norm_mix_post'], lw['norm_ffn_pre'], min(M, 2 * OPROJ_ROWS))
    Fd = lw['w_gate'].shape[2]
    tn = _largest_tile(Fd, 512, LANES)
    if not short:
        act, conv_new = ffn_up_seq(h2.reshape(B, T, D), lw['w_gate'], lw['w_up'], l, lw['conv_w'],
                                   lw['conv_b'], prev_conv, tn)
        act = act.reshape(M, Fd)
    else:
        z = jnp.zeros((B, T, Fd), F32)
        p1 = z.at[:, 0].set(prev_conv[:, 1]).reshape(M, Fd)
        p2 = z.at[:, 0].set(prev_conv[:, 0]).at[:, 1].set(prev_conv[:, 1]).reshape(M, Fd)
        act, u = ffn_up_short(h2, lw['w_gate'], lw['w_up'], l, lw['conv_w'], lw['conv_b'], p1, p2, T, tn)
        conv_new = u.reshape(B, T, Fd)[:, T - (CONV_W - 1):]
    x, h_next = ffn_down(act, lw['w_down'], l, x, lw['norm_ffn_post'], lw['norm_next'], min(M, FFN_DOWN_ROWS))
    return x, h_next, conv_new


def _layer(l, lw, xp, hp, xs, hs, *, B, T, DB, TS, P, zero_states, states, paged, kv_stacks):
    WC = lw['attn_subln'].shape[1]
    lam_init = 0.8 - 0.6 * math.exp(-0.3 * l)
    lams = (lw['lam_q1'], lw['lam_k1'], lw['lam_q2'], lw['lam_k2'])
    zp_pool, zp_S, zp_conv = zero_states
    sp_pool, sp_S, sp_conv = states
    cache_k, cache_v, page_table = paged

    _, main_p = _project(hp, l, lw, B, T, False)
    xa_s, main_s = _project(hs, l, lw, DB, TS, True)
    a_p, pool_p = pool_mixer(hp.reshape(B, T, -1), zp_pool, lw['pool_w'], lw['pool_scale'], 0,
                             w_in=lw['w_in'], layer=l)
    a_s, pool_s = pool_mixer(xa_s, sp_pool, lw['pool_w'], lw['pool_scale'], P)

    pad = ((0, 0), (0, SUBLANES - TS), (0, 0))
    main8 = jnp.pad(main_s, pad)
    ob8, S_s = hgrn2(main8, sp_S, lw['hgrn_lb'], lw['hgrn_norm'], SUBLANES, TS)
    qs8, k_rot8, _ = rope(main8, 4, 5, WC, lw['rope_tabs'], P)
    v_s = main_s[:, :, 6 * WC:]

    qs, kb, vb, k_stack, v_stack = rope_heads(main_p, 4, WC, lw['rope_tabs'], 0, *kv_stacks, l)
    ob_p, S_p, oc8 = hgrn2_with_sample_attention(
        main_p, zp_S, lw['hgrn_lb'], lw['hgrn_norm'], min(HG_CHUNK, T),
        qs8.astype(F32), k_rot8, jnp.pad(v_s, pad), cache_k, cache_v, l, page_table,
        lams, lw['attn_subln'], lam_init, TS)
    oc_p = prompt_attention(qs, kb, vb, lams, lw['attn_subln'], lam_init)

    xp, hp, conv_p = _finish(xp, a_p, ob_p, oc_p, l, lw, B, T, zp_conv, False)
    xs, hs, conv_s = _finish(xs, a_s, ob8[:, :TS], oc8[:, :TS], l, lw, DB, TS, sp_conv, True)
    return (xp, hp, xs, hs, (k_stack, v_stack), (pool_p, S_p, conv_p),
            (k_rot8[:, :TS], v_s, pool_s, S_s, conv_s))


def kernel(x_prompt, x_sample, cache_k, cache_v, state_pool, state_hgrn, state_conv, page_table, w_in, w_out, norm_mix_pre, norm_mix_post, norm_ffn_pre, norm_ffn_post, pool_w, pool_scale, hgrn_lower_bounds, hgrn_norm, attn_lam_q1, attn_lam_k1, attn_lam_q2, attn_lam_k2, attn_subln, ffn_w_gate, ffn_w_up, ffn_conv_w, ffn_conv_b, ffn_w_down):
    depth = w_in.shape[0]
    B, T, D = x_prompt.shape
    DB, TS, _ = x_sample.shape
    page = cache_k.shape[2]
    P = page_table.shape[1] * page
    WA = pool_scale.shape[1]
    WH = hgrn_norm.shape[1]
    WC = attn_subln.shape[1]
    HH = WH // HEAD
    Fd = ffn_w_gate.shape[2]
    assert TS >= CONV_W - 1 and TS <= SUBLANES and (TS & (TS - 1)) == 0

    sm = jax.nn.softmax(hgrn_lower_bounds.astype(F32), axis=0)
    lbs = jnp.cumsum(sm, axis=0) - sm[0]

    half = ROT_DIM // 2
    freqs = jnp.power(ROPE_THETA, -jnp.arange(0, ROT_DIM, 2, dtype=F32) / ROT_DIM)
    lane = np.arange(LANES)
    in_rot = (lane % QK_HALF) < ROT_DIM
    fl = jnp.where(jnp.asarray(in_rot), jnp.tile(freqs, LANES // half), 0.0).reshape(1, LANES)
    s1 = jnp.asarray(np.where(in_rot & ((lane % QK_HALF) < half), -1.0, 0.0), F32).reshape(1, LANES)
    s2 = jnp.asarray(np.where(in_rot & ((lane % QK_HALF) >= half), 1.0, 0.0), F32).reshape(1, LANES)

    ones = jnp.ones((1, D), F32)
    HC = WC // HEAD
    ck = jnp.swapaxes(cache_k, 2, 3)
    cv = jnp.swapaxes(cache_v, 2, 3)
    kv = (jnp.zeros((depth, B, HC, T, HEAD), F32), jnp.zeros((depth, B, HC, T, HEAD), F32))

    w_out_b = w_out.astype(BF16)
    w_down_b = ffn_w_down.astype(BF16)

    def layer_weights(l):
        return {
            'w_in': w_in,
            'w_out': w_out_b,
            'w_gate': ffn_w_gate,
            'w_up': ffn_w_up,
            'w_down': w_down_b,
            'pool_w': pool_w[l],
            'pool_scale': pool_scale[l].reshape(1, WA),
            'hgrn_lb': lbs[l].reshape(1, WH),
            'hgrn_norm': hgrn_norm[l].reshape(1, WH),
            'attn_subln': attn_subln[l].reshape(1, WC),
            'lam_q1': attn_lam_q1[l].reshape(1, QK_HALF),
            'lam_k1': attn_lam_k1[l].reshape(1, QK_HALF),
            'lam_q2': attn_lam_q2[l].reshape(1, QK_HALF),
            'lam_k2': attn_lam_k2[l].reshape(1, QK_HALF),
            'norm_mix_post': norm_mix_post[l].reshape(1, D),
            'norm_ffn_pre': norm_ffn_pre[l].reshape(1, D),
            'norm_ffn_post': norm_ffn_post[l].reshape(1, D),
            'norm_next': norm_mix_pre[l + 1].reshape(1, D) if l + 1 < depth else ones,
            'conv_w': ffn_conv_w[l],
            'conv_b': ffn_conv_b[l].reshape(1, Fd),
            'rope_tabs': (fl, s1, s2),
        }

    xp = x_prompt.reshape(B * T, D)
    xs = x_sample.reshape(DB * TS, D)
    g0 = norm_mix_pre[0].reshape(1, D)
    hp = rmsnorm_bf16(xp, g0, min(B * T, 512))
    hs = rmsnorm_bf16(xs, g0, DB * TS)
    zp_pool = jnp.zeros((B, POOL_STATE, WA), F32)
    zp_S = jnp.zeros((B, HH, HEAD, HEAD), F32)
    zp_conv = jnp.zeros((B, CONV_W - 1, Fd), F32)

    st_p, st_s = [], []
    for l in range(depth):
        xp, hp, xs, hs, kv, sp, ss = _layer(
            l, layer_weights(l), xp, hp, xs, hs, B=B, T=T, DB=DB, TS=TS, P=P,
            zero_states=(zp_pool, zp_S, zp_conv), states=(state_pool[l], state_hgrn[l], state_conv[l]),
            paged=(ck, cv, page_table), kv_stacks=kv)
        st_p.append(sp)
        st_s.append(ss)

    def stack(sts, i, shape):
        return jnp.stack([s[i] for s in sts]).reshape(shape)

    return (xp.reshape(B, T, D), xs.reshape(DB, TS, D),
            jnp.swapaxes(kv[0], 2, 3), jnp.swapaxes(kv[1], 2, 3),
            stack(st_p, 0, (depth, B, POOL_STATE, WA)), stack(st_p, 1, (depth, B, HH, HEAD, HEAD)),
            stack(st_p, 2, (depth, B, CONV_W - 1, Fd)),
            stack(st_s, 0, (depth, DB, TS, HC, HEAD)), stack(st_s, 1, (depth, DB, TS, HC, HEAD)),
            stack(st_s, 2, (depth, DB, POOL_STATE, WA)), stack(st_s, 3, (depth, DB, HH, HEAD, HEAD)),
            stack(st_s, 4, (depth, DB, CONV_W - 1, Fd)))
```
